```python
import jax, jax.numpy as jnp
from jax import lax
import numpy as np

D_MODEL = 2048
BATCH = 8
SEQ = 8192
DEPTH = 4

CHUNK = 64
D_MIX = D_MODEL
ATTN_HEAD_DIM = 128
ATTN_HEADS = (D_MIX // 2) // ATTN_HEAD_DIM
D_ATTN = ATTN_HEADS * ATTN_HEAD_DIM
D_CONF = D_MIX // 4
D_SCONV = D_MIX - D_ATTN - D_CONF
CONF_WIDTH = 31
SCONV_WIDTH = 3
FFN_WIDTH = 3
D_FF = 5632
Q_BLOCK = 128
N_ADA = 6
RMS_EPS = 1e-6
LN_EPS = 1e-5
IN_COLS = 3 * D_ATTN + ATTN_HEADS + 2 * D_CONF + 3 * D_SCONV

kernel_name = "hybrid_fox_conformer_shortconv_trunk"


def rms_norm(x, g):
    xf = x.astype(jnp.float32)
    y = xf * lax.rsqrt(jnp.mean(xf * xf, axis=-1, keepdims=True) + RMS_EPS)
    return (y * g.astype(jnp.float32)).astype(x.dtype)


def layer_norm(x, g, b):
    xf = x.astype(jnp.float32)
    mu = jnp.mean(xf, axis=-1, keepdims=True)
    xc = xf - mu
    y = xc * lax.rsqrt(jnp.mean(xc * xc, axis=-1, keepdims=True) + LN_EPS)
    return (y * g.astype(jnp.float32) + b.astype(jnp.float32)).astype(x.dtype)


def modulate(h, shift, scale):
    return h * (1 + scale[:, None, :]) + shift[:, None, :]


def causal_dwconv(x, w, b=None):
    K, C = w.shape
    xp = jnp.pad(x, ((0, 0), (K - 1, 0), (0, 0)))
    y = lax.conv_general_dilated(xp, w[:, None, :], window_strides=(1,), padding='VALID',
                                 dimension_numbers=('NWC', 'WIO', 'NWC'),
                                 feature_group_count=C)
    if b is not None:
        y = y + b
    return y


def forgetting_attention(q, k, v, log_f):
    B, S, H, Dh = q.shape
    nb = S // Q_BLOCK
    F = jnp.cumsum(log_f, axis=1)
    Fk = jnp.transpose(F, (0, 2, 1))[:, :, None, :]
    qb = jnp.transpose(q.reshape(B, nb, Q_BLOCK, H, Dh), (1, 0, 2, 3, 4))
    Fqb = jnp.transpose(F.reshape(B, nb, Q_BLOCK, H), (1, 0, 3, 2))
    qpos = jnp.arange(S).reshape(nb, Q_BLOCK)
    kpos = jnp.arange(S)
    scale = Dh ** -0.5

    def one_block(args):
        q_blk, Fq_blk, qp = args
        s = jnp.einsum('bqhd,bkhd->bhqk', q_blk, k,
                       preferred_element_type=jnp.float32) * scale
        s = s + Fq_blk[..., None] - Fk
        mask = kpos[None, :] <= qp[:, None]
        s = jnp.where(mask, s, -jnp.inf)
        p = jax.nn.softmax(s, axis=-1)
        return jnp.einsum('bhqk,bkhd->bqhd', p.astype(v.dtype), v)

    out = lax.map(one_block, (qb, Fqb, qpos))
    return jnp.transpose(out, (1, 0, 2, 3, 4)).reshape(B, S, H * Dh)


def _fwd_setup_inputs(seed: int = 0) -> dict:
    key = jax.random.key(seed)
    ks = jax.random.split(key, 24)
    L, D = DEPTH, D_MODEL
    nrm = jax.random.normal
    x = nrm(ks[0], (BATCH, SEQ, D), jnp.float32)
    c = nrm(ks[1], (BATCH, D), jnp.float32)
    ada_w = nrm(ks[2], (L, D, N_ADA * D), jnp.float32) * (0.5 * D ** -0.5)
    ada_b = nrm(ks[3], (L, N_ADA * D), jnp.float32) * 0.02
    mix_norm_g = 1.0 + 0.05 * nrm(ks[4], (L, D), jnp.float32)
    w_in = nrm(ks[5], (L, D, IN_COLS), jnp.float32) * D ** -0.5
    b_forget = jax.random.uniform(ks[6], (L, ATTN_HEADS), jnp.float32, 1.0, 4.0)
    conf_dw_w = nrm(ks[7], (L, CONF_WIDTH, D_CONF), jnp.float32) * CONF_WIDTH ** -0.5
    conf_dw_b = nrm(ks[8], (L, D_CONF), jnp.float32) * 0.02
    conf_ln_g = 1.0 + 0.05 * nrm(ks[9], (L, D_CONF), jnp.float32)
    conf_ln_b = nrm(ks[10], (L, D_CONF), jnp.float32) * 0.02
    sc_dw_w = nrm(ks[11], (L, SCONV_WIDTH, D_SCONV), jnp.float32) * SCONV_WIDTH ** -0.5
    w_out = nrm(ks[12], (L, D_MIX, D), jnp.float32) * D_MIX ** -0.5
    ffn_norm_g = 1.0 + 0.05 * nrm(ks[13], (L, D), jnp.float32)
    w_up = nrm(ks[14], (L, D, 2 * D_FF), jnp.float32) * D ** -0.5
    ffn_dw_w = nrm(ks[15], (L, FFN_WIDTH, 2 * D_FF), jnp.float32) * FFN_WIDTH ** -0.5
    ffn_dw_b = nrm(ks[16], (L, 2 * D_FF), jnp.float32) * 0.02
    w_down = nrm(ks[17], (L, D_FF, D), jnp.float32) * D_FF ** -0.5
    final_norm_g = 1.0 + 0.05 * nrm(ks[18], (D,), jnp.float32)
    return {"x": x, "c": c, "ada_w": ada_w, "ada_b": ada_b, "mix_norm_g": mix_norm_g,
            "w_in": w_in, "b_forget": b_forget, "conf_dw_w": conf_dw_w, "conf_dw_b": conf_dw_b,
            "conf_ln_g": conf_ln_g, "conf_ln_b": conf_ln_b, "sc_dw_w": sc_dw_w, "w_out": w_out,
            "ffn_norm_g": ffn_norm_g, "w_up": w_up, "ffn_dw_w": ffn_dw_w, "ffn_dw_b": ffn_dw_b,
            "w_down": w_down, "final_norm_g": final_norm_g}


def _fwd_reference(x, c, ada_w, ada_b, mix_norm_g, w_in, b_forget, conf_dw_w, conf_dw_b,
              conf_ln_g, conf_ln_b, sc_dw_w, w_out, ffn_norm_g, w_up, ffn_dw_w, ffn_dw_b,
              w_down, final_norm_g):
    B, S, _ = x.shape
    split_at = list(np.cumsum([D_ATTN, D_ATTN, D_ATTN, ATTN_HEADS,
                               D_CONF, D_CONF, D_SCONV, D_SCONV]))
    c_act = jax.nn.silu(c)
    for l in range(DEPTH):
        ada = c_act @ ada_w[l] + ada_b[l]
        sh_m, sc_m, g_m, sh_f, sc_f, g_f = jnp.split(ada, N_ADA, axis=-1)

        h = modulate(rms_norm(x, mix_norm_g[l]), sh_m, sc_m)
        proj = h @ w_in[l]
        q, k, v, f_logit, cv, cg, s_x, s_b, s_c = jnp.split(proj, split_at, axis=-1)

        log_f = jax.nn.log_sigmoid((f_logit + b_forget[l]).astype(jnp.float32))
        attn = forgetting_attention(q.reshape(B, S, ATTN_HEADS, ATTN_HEAD_DIM),
                                    k.reshape(B, S, ATTN_HEADS, ATTN_HEAD_DIM),
                                    v.reshape(B, S, ATTN_HEADS, ATTN_HEAD_DIM), log_f)

        conf = cv * jax.nn.sigmoid(cg)
        conf = causal_dwconv(conf, conf_dw_w[l], conf_dw_b[l])
        conf = jax.nn.silu(layer_norm(conf, conf_ln_g[l], conf_ln_b[l]))

        sconv = s_b * causal_dwconv(s_c * s_x, sc_dw_w[l])

        mixed = jnp.concatenate([attn, conf, sconv], axis=-1) @ w_out[l]
        x = x + g_m[:, None, :] * mixed

        h = modulate(rms_norm(x, ffn_norm_g[l]), sh_f, sc_f)
        u = causal_dwconv(h @ w_up[l], ffn_dw_w[l], ffn_dw_b[l])
        gate, val = jnp.split(u, 2, axis=-1)
        x = x + g_f[:, None, :] * ((jax.nn.silu(gate) * val) @ w_down[l])

    return rms_norm(x, final_norm_g)


import jax as _jax
import jax.numpy as _jnp

TWIN_FORMAT = 'train_step'
FWD_PARAMS = ['x', 'c', 'ada_w', 'ada_b', 'mix_norm_g', 'w_in', 'b_forget', 'conf_dw_w', 'conf_dw_b', 'conf_ln_g', 'conf_ln_b', 'sc_dw_w', 'w_out', 'ffn_norm_g', 'w_up', 'ffn_dw_w', 'ffn_dw_b', 'w_down', 'final_norm_g']
TWIN_WEIGHTS = ['ada_w', 'ada_b', 'mix_norm_g', 'w_in', 'b_forget', 'conf_dw_w', 'conf_dw_b', 'conf_ln_g', 'conf_ln_b', 'sc_dw_w', 'w_out', 'ffn_norm_g', 'w_up', 'ffn_dw_w', 'ffn_dw_b', 'w_down', 'final_norm_g']
TWIN_DIFF_INPUT = 'x'
TWIN_INPUTS = ['x', 'c', 'ada_w', 'ada_b', 'mix_norm_g', 'w_in', 'b_forget', 'conf_dw_w', 'conf_dw_b', 'conf_ln_g', 'conf_ln_b', 'sc_dw_w', 'w_out', 'ffn_norm_g', 'w_up', 'ffn_dw_w', 'ffn_dw_b', 'w_down', 'final_norm_g', 'loss_target', 'm_ada_w', 'm_ada_b', 'm_mix_norm_g', 'm_w_in', 'm_b_forget', 'm_conf_dw_w', 'm_conf_dw_b', 'm_conf_ln_g', 'm_conf_ln_b', 'm_sc_dw_w', 'm_w_out', 'm_ffn_norm_g', 'm_w_up', 'm_ffn_dw_w', 'm_ffn_dw_b', 'm_w_down', 'm_final_norm_g', 'v_ada_w', 'v_ada_b', 'v_mix_norm_g', 'v_w_in', 'v_b_forget', 'v_conf_dw_w', 'v_conf_dw_b', 'v_conf_ln_g', 'v_conf_ln_b', 'v_sc_dw_w', 'v_w_out', 'v_ffn_norm_g', 'v_w_up', 'v_ffn_dw_w', 'v_ffn_dw_b', 'v_w_down', 'v_final_norm_g']
TWIN_OUTPUTS = ['loss', 'grad_x', 'grad_ada_w', 'grad_ada_b', 'grad_mix_norm_g', 'grad_w_in', 'grad_b_forget', 'grad_conf_dw_w', 'grad_conf_dw_b', 'grad_conf_ln_g', 'grad_conf_ln_b', 'grad_sc_dw_w', 'grad_w_out', 'grad_ffn_norm_g', 'grad_w_up', 'grad_ffn_dw_w', 'grad_ffn_dw_b', 'grad_w_down', 'grad_final_norm_g', 'delta_ada_w', 'delta_ada_b', 'delta_mix_norm_g', 'delta_w_in', 'delta_b_forget', 'delta_conf_dw_w', 'delta_conf_dw_b', 'delta_conf_ln_g', 'delta_conf_ln_b', 'delta_sc_dw_w', 'delta_w_out', 'delta_ffn_norm_g', 'delta_w_up', 'delta_ffn_dw_w', 'delta_ffn_dw_b', 'delta_w_down', 'delta_final_norm_g', 'new_m_ada_w', 'new_m_ada_b', 'new_m_mix_norm_g', 'new_m_w_in', 'new_m_b_forget', 'new_m_conf_dw_w', 'new_m_conf_dw_b', 'new_m_conf_ln_g', 'new_m_conf_ln_b', 'new_m_sc_dw_w', 'new_m_w_out', 'new_m_ffn_norm_g', 'new_m_w_up', 'new_m_ffn_dw_w', 'new_m_ffn_dw_b', 'new_m_w_down', 'new_m_final_norm_g', 'new_v_ada_w', 'new_v_ada_b', 'new_v_mix_norm_g', 'new_v_w_in', 'new_v_b_forget', 'new_v_conf_dw_w', 'new_v_conf_dw_b', 'new_v_conf_ln_g', 'new_v_conf_ln_b', 'new_v_sc_dw_w', 'new_v_w_out', 'new_v_ffn_norm_g', 'new_v_w_up', 'new_v_ffn_dw_w', 'new_v_ffn_dw_b', 'new_v_w_down', 'new_v_final_norm_g']
TWIN_LEAF_KINDS = {'loss': 'loss', 'grad_x': 'grad_x', 'grad_ada_w': 'grad_w', 'grad_ada_b': 'grad_w', 'grad_mix_norm_g': 'grad_w', 'grad_w_in': 'grad_w', 'grad_b_forget': 'grad_w', 'grad_conf_dw_w': 'grad_w', 'grad_conf_dw_b': 'grad_w', 'grad_conf_ln_g': 'grad_w', 'grad_conf_ln_b': 'grad_w', 'grad_sc_dw_w': 'grad_w', 'grad_w_out': 'grad_w', 'grad_ffn_norm_g': 'grad_w', 'grad_w_up': 'grad_w', 'grad_ffn_dw_w': 'grad_w', 'grad_ffn_dw_b': 'grad_w', 'grad_w_down': 'grad_w', 'grad_final_norm_g': 'grad_w', 'delta_ada_w': 'delta_w', 'delta_ada_b': 'delta_w', 'delta_mix_norm_g': 'delta_w', 'delta_w_in': 'delta_w', 'delta_b_forget': 'delta_w', 'delta_conf_dw_w': 'delta_w', 'delta_conf_dw_b': 'delta_w', 'delta_conf_ln_g': 'delta_w', 'delta_conf_ln_b': 'delta_w', 'delta_sc_dw_w': 'delta_w', 'delta_w_out': 'delta_w', 'delta_ffn_norm_g': 'delta_w', 'delta_w_up': 'delta_w', 'delta_ffn_dw_w': 'delta_w', 'delta_ffn_dw_b': 'delta_w', 'delta_w_down': 'delta_w', 'delta_final_norm_g': 'delta_w', 'new_m_ada_w': 'new_m', 'new_m_ada_b': 'new_m', 'new_m_mix_norm_g': 'new_m', 'new_m_w_in': 'new_m', 'new_m_b_forget': 'new_m', 'new_m_conf_dw_w': 'new_m', 'new_m_conf_dw_b': 'new_m', 'new_m_conf_ln_g': 'new_m', 'new_m_conf_ln_b': 'new_m', 'new_m_sc_dw_w': 'new_m', 'new_m_w_out': 'new_m', 'new_m_ffn_norm_g': 'new_m', 'new_m_w_up': 'new_m', 'new_m_ffn_dw_w': 'new_m', 'new_m_ffn_dw_b': 'new_m', 'new_m_w_down': 'new_m', 'new_m_final_norm_g': 'new_m', 'new_v_ada_w': 'new_v', 'new_v_ada_b': 'new_v', 'new_v_mix_norm_g': 'new_v', 'new_v_w_in': 'new_v', 'new_v_b_forget': 'new_v', 'new_v_conf_dw_w': 'new_v', 'new_v_conf_dw_b': 'new_v', 'new_v_conf_ln_g': 'new_v', 'new_v_conf_ln_b': 'new_v', 'new_v_sc_dw_w': 'new_v', 'new_v_w_out': 'new_v', 'new_v_ffn_norm_g': 'new_v', 'new_v_w_up': 'new_v', 'new_v_ffn_dw_w': 'new_v', 'new_v_ffn_dw_b': 'new_v', 'new_v_w_down': 'new_v', 'new_v_final_norm_g': 'new_v'}


def _forward(args):
    return _fwd_reference(*[args[k] for k in FWD_PARAMS])


def _output_shape():
    def fwd():
        inp = _fwd_setup_inputs(0)
        return _fwd_reference(*[inp[k] for k in FWD_PARAMS])
    out = _jax.eval_shape(fwd)
    return out.shape, out.dtype

N_MICROBATCH = 1
ADAM_LR = 0.001
ADAM_B1 = 0.9
ADAM_B2 = 0.999
ADAM_EPS = 1e-08
ADAM_WD = 0.01
ADAM_STEP = 10
PER_EXAMPLE_BATCH_AXIS = {'x': 0, 'c': 0, 'loss_target': 0}
SHARED_INPUTS = []
_WEIGHT_DTYPES = {'ada_w': _jnp.float32, 'ada_b': _jnp.float32, 'mix_norm_g': _jnp.float32, 'w_in': _jnp.float32, 'b_forget': _jnp.float32, 'conf_dw_w': _jnp.float32, 'conf_dw_b': _jnp.float32, 'conf_ln_g': _jnp.float32, 'conf_ln_b': _jnp.float32, 'sc_dw_w': _jnp.float32, 'w_out': _jnp.float32, 'ffn_norm_g': _jnp.float32, 'w_up': _jnp.float32, 'ffn_dw_w': _jnp.float32, 'ffn_dw_b': _jnp.float32, 'w_down': _jnp.float32, 'final_norm_g': _jnp.float32}
MOMENT_SCALE = {'ada_w': 4.823745e-02, 'ada_b': 8.954616e-02, 'mix_norm_g': 4.502088e-02, 'w_in': 2.885756e-02, 'b_forget': 9.739437e-02, 'conf_dw_w': 2.431674e-02, 'conf_dw_b': 4.778212e-02, 'conf_ln_g': 3.024812e-02, 'conf_ln_b': 2.755593e-02, 'sc_dw_w': 4.983752e-02, 'w_out': 3.080044e-02, 'ffn_norm_g': 3.706652e-02, 'w_up': 1.648512e-02, 'ffn_dw_w': 1.651688e-02, 'ffn_dw_b': 1.536523e-02, 'w_down': 2.706594e-02, 'final_norm_g': 3.209598e+01}


def _to_microbatches(a, axis):
    t = _jnp.moveaxis(a, axis, 0)
    t = t.reshape((N_MICROBATCH, t.shape[0] // N_MICROBATCH) + t.shape[1:])
    return _jnp.moveaxis(t, 1, axis + 1)


def setup_inputs(seed: int = 0) -> dict:
    inp = _fwd_setup_inputs(seed)
    key = _jax.random.fold_in(_jax.random.key(seed), 7919)
    shape, _ = _output_shape()
    out = dict(inp)
    out["loss_target"] = _jax.random.normal(_jax.random.fold_in(key, 0), shape, _jnp.float32)
    for i, name in enumerate(TWIN_WEIGHTS):
        w = inp[name].astype(_jnp.float32)
        if MOMENT_SCALE is None:
            s = _jnp.sqrt(_jnp.mean(_jnp.square(w)) + 1e-30)
        else:
            s = MOMENT_SCALE[name]
        km, kv = _jax.random.split(_jax.random.fold_in(key, i + 1))
        out[name] = w
        out["m_" + name] = s * _jax.random.normal(km, w.shape, _jnp.float32)
        out["v_" + name] = (s * s) * _jax.random.uniform(kv, w.shape, _jnp.float32, 0.5, 1.5)
    if N_MICROBATCH > 1:
        for name, axis in PER_EXAMPLE_BATCH_AXIS.items():
            out[name] = _to_microbatches(out[name], axis)
    return {'x': out['x'], 'c': out['c'], 'ada_w': out['ada_w'], 'ada_b': out['ada_b'], 'mix_norm_g': out['mix_norm_g'], 'w_in': out['w_in'], 'b_forget': out['b_forget'], 'conf_dw_w': out['conf_dw_w'], 'conf_dw_b': out['conf_dw_b'], 'conf_ln_g': out['conf_ln_g'], 'conf_ln_b': out['conf_ln_b'], 'sc_dw_w': out['sc_dw_w'], 'w_out': out['w_out'], 'ffn_norm_g': out['ffn_norm_g'], 'w_up': out['w_up'], 'ffn_dw_w': out['ffn_dw_w'], 'ffn_dw_b': out['ffn_dw_b'], 'w_down': out['w_down'], 'final_norm_g': out['final_norm_g'], 'loss_target': out['loss_target'], 'm_ada_w': out['m_ada_w'], 'm_ada_b': out['m_ada_b'], 'm_mix_norm_g': out['m_mix_norm_g'], 'm_w_in': out['m_w_in'], 'm_b_forget': out['m_b_forget'], 'm_conf_dw_w': out['m_conf_dw_w'], 'm_conf_dw_b': out['m_conf_dw_b'], 'm_conf_ln_g': out['m_conf_ln_g'], 'm_conf_ln_b': out['m_conf_ln_b'], 'm_sc_dw_w': out['m_sc_dw_w'], 'm_w_out': out['m_w_out'], 'm_ffn_norm_g': out['m_ffn_norm_g'], 'm_w_up': out['m_w_up'], 'm_ffn_dw_w': out['m_ffn_dw_w'], 'm_ffn_dw_b': out['m_ffn_dw_b'], 'm_w_down': out['m_w_down'], 'm_final_norm_g': out['m_final_norm_g'], 'v_ada_w': out['v_ada_w'], 'v_ada_b': out['v_ada_b'], 'v_mix_norm_g': out['v_mix_norm_g'], 'v_w_in': out['v_w_in'], 'v_b_forget': out['v_b_forget'], 'v_conf_dw_w': out['v_conf_dw_w'], 'v_conf_dw_b': out['v_conf_dw_b'], 'v_conf_ln_g': out['v_conf_ln_g'], 'v_conf_ln_b': out['v_conf_ln_b'], 'v_sc_dw_w': out['v_sc_dw_w'], 'v_w_out': out['v_w_out'], 'v_ffn_norm_g': out['v_ffn_norm_g'], 'v_w_up': out['v_w_up'], 'v_ffn_dw_w': out['v_ffn_dw_w'], 'v_ffn_dw_b': out['v_ffn_dw_b'], 'v_w_down': out['v_w_down'], 'v_final_norm_g': out['v_final_norm_g']}


def _loss(weights, diff, rest, loss_target):
    with _jax.named_scope("forward"):
        args = {**rest, TWIN_DIFF_INPUT: diff, **{k: w.astype(_WEIGHT_DTYPES[k]) for k, w in weights.items()}}
        y = _forward(args)
    with _jax.named_scope("loss_head"):
        err = _jnp.square(y.astype(_jnp.float32) - loss_target)
        return 0.5 * _jnp.sum(_jnp.mean(err, axis=-1)) if err.ndim else 0.5 * err


def _adamw(w, g, m, v):
    m = ADAM_B1 * m + (1.0 - ADAM_B1) * g
    v = ADAM_B2 * v + (1.0 - ADAM_B2) * _jnp.square(g)
    m_hat = m / (1.0 - ADAM_B1 ** ADAM_STEP)
    v_hat = v / (1.0 - ADAM_B2 ** ADAM_STEP)
    delta = -ADAM_LR * (m_hat / (_jnp.sqrt(v_hat) + ADAM_EPS) + ADAM_WD * w)
    return delta, m, v


def reference(x, c, ada_w, ada_b, mix_norm_g, w_in, b_forget, conf_dw_w, conf_dw_b, conf_ln_g, conf_ln_b, sc_dw_w, w_out, ffn_norm_g, w_up, ffn_dw_w, ffn_dw_b, w_down, final_norm_g, loss_target, m_ada_w, m_ada_b, m_mix_norm_g, m_w_in, m_b_forget, m_conf_dw_w, m_conf_dw_b, m_conf_ln_g, m_conf_ln_b, m_sc_dw_w, m_w_out, m_ffn_norm_g, m_w_up, m_ffn_dw_w, m_ffn_dw_b, m_w_down, m_final_norm_g, v_ada_w, v_ada_b, v_mix_norm_g, v_w_in, v_b_forget, v_conf_dw_w, v_conf_dw_b, v_conf_ln_g, v_conf_ln_b, v_sc_dw_w, v_w_out, v_ffn_norm_g, v_w_up, v_ffn_dw_w, v_ffn_dw_b, v_w_down, v_final_norm_g):
    given = dict(x=x, c=c, ada_w=ada_w, ada_b=ada_b, mix_norm_g=mix_norm_g, w_in=w_in, b_forget=b_forget, conf_dw_w=conf_dw_w, conf_dw_b=conf_dw_b, conf_ln_g=conf_ln_g, conf_ln_b=conf_ln_b, sc_dw_w=sc_dw_w, w_out=w_out, ffn_norm_g=ffn_norm_g, w_up=w_up, ffn_dw_w=ffn_dw_w, ffn_dw_b=ffn_dw_b, w_down=w_down, final_norm_g=final_norm_g, loss_target=loss_target, m_ada_w=m_ada_w, m_ada_b=m_ada_b, m_mix_norm_g=m_mix_norm_g, m_w_in=m_w_in, m_b_forget=m_b_forget, m_conf_dw_w=m_conf_dw_w, m_conf_dw_b=m_conf_dw_b, m_conf_ln_g=m_conf_ln_g, m_conf_ln_b=m_conf_ln_b, m_sc_dw_w=m_sc_dw_w, m_w_out=m_w_out, m_ffn_norm_g=m_ffn_norm_g, m_w_up=m_w_up, m_ffn_dw_w=m_ffn_dw_w, m_ffn_dw_b=m_ffn_dw_b, m_w_down=m_w_down, m_final_norm_g=m_final_norm_g, v_ada_w=v_ada_w, v_ada_b=v_ada_b, v_mix_norm_g=v_mix_norm_g, v_w_in=v_w_in, v_b_forget=v_b_forget, v_conf_dw_w=v_conf_dw_w, v_conf_dw_b=v_conf_dw_b, v_conf_ln_g=v_conf_ln_g, v_conf_ln_b=v_conf_ln_b, v_sc_dw_w=v_sc_dw_w, v_w_out=v_w_out, v_ffn_norm_g=v_ffn_norm_g, v_w_up=v_w_up, v_ffn_dw_w=v_ffn_dw_w, v_ffn_dw_b=v_ffn_dw_b, v_w_down=v_w_down, v_final_norm_g=v_final_norm_g)
    weights = {n: given[n] for n in TWIN_WEIGHTS}
    shared = {n: given[n] for n in SHARED_INPUTS}
    per_example = {n: given[n] for n in ['x', 'c']}
    grad_fn = _jax.value_and_grad(_loss, argnums=(0, 1))

    def one_microbatch(ex, loss_target):
        ex = dict(ex)
        diff = ex.pop(TWIN_DIFF_INPUT)
        return grad_fn(weights, diff, {**shared, **ex}, loss_target)

    if N_MICROBATCH == 1:
        loss, (grad_w, grad_x) = one_microbatch(per_example, given["loss_target"])
    else:
        def body(carry, xs):
            loss_sum, grad_sum = carry
            l_k, (gw_k, gx_k) = one_microbatch(xs[0], xs[1])
            with _jax.named_scope("update"):
                return (loss_sum + l_k, _jax.tree.map(_jnp.add, grad_sum, gw_k)), gx_k

        init = (_jnp.zeros((), _jnp.float32), _jax.tree.map(_jnp.zeros_like, weights))
        (loss, grad_w), grad_x = _jax.lax.scan(body, init, (per_example, given["loss_target"]))
    with _jax.named_scope("update"):
        delta_w, new_m, new_v = {}, {}, {}
        for n in TWIN_WEIGHTS:
            delta_w[n], new_m[n], new_v[n] = _adamw(weights[n], grad_w[n], given["m_" + n], given["v_" + n])
    return (loss, grad_x, *[grad_w[n] for n in TWIN_WEIGHTS], *[delta_w[n] for n in TWIN_WEIGHTS],
            *[new_m[n] for n in TWIN_WEIGHTS], *[new_v[n] for n in TWIN_WEIGHTS])
```

```python
import functools
import math

import jax
import jax.numpy as jnp
from jax import lax
from jax.experimental import pallas as pl
from jax.experimental.pallas import tpu as pltpu

_MM = jnp.bfloat16
_F32 = jnp.float32
VMEM_LIMIT_V7X = 48 * 1024 * 1024
LANES = 128
SUBLANES = 8
BF16_ROWS = 16
HEAD_DIM = 128
RMS_EPS = 1e-6
LN_EPS = 1e-5
NEG = -1e30
HIGHEST = lax.Precision.HIGHEST

ADAM_LR = 0.001
ADAM_B1 = 0.9
ADAM_B2 = 0.999
ADAM_EPS = 1e-08
ADAM_WD = 0.01
ADAM_STEP = 10

N_DEV = 8
N_CHIP = 4
PACK_COLS = 1024

ATT_BLOCK = 512
CONV_TILE = 512
CONV_HALO = 32
FFN_HALO = BF16_ROWS
NORM_TILE = 256
MM_TM = 1024
MM_TN = 1024
MM_TK = 1024

MESH = pl.DeviceIdType.MESH


def _tile(dim, pref, mult=LANES):
    t = (min(pref, dim) // mult) * mult
    while t >= mult:
        if dim % t == 0:
            return t
        t -= mult
    return dim


def _cparams(sem):
    return pltpu.CompilerParams(dimension_semantics=sem, vmem_limit_bytes=VMEM_LIMIT_V7X)


def _sigmoid(x):
    return 1.0 / (1.0 + jnp.exp(-x))


def _colsum(x):
    return jnp.sum(x, axis=0, keepdims=True)


def matmul(a, b, *, out_dtype, name, trans_a=False, trans_b=False, tm=MM_TM, tn=MM_TN, tk=MM_TK,
           a_silu=False, precision=None, b_cols=None):
    M, K = (a.shape[1], a.shape[0]) if trans_a else a.shape
    N = b.shape[0] if trans_b else b.shape[1]
    assert (b.shape[1] if trans_b else b.shape[0]) == K
    col0 = 0
    if b_cols is not None:
        assert not trans_b
        col0, N = b_cols
    tm = _tile(M, tm, SUBLANES if (M % LANES) else LANES)
    tn = _tile(math.gcd(N, col0) if col0 else N, tn)
    tk = _tile(K, tk)
    nk = K // tk
    jb = col0 // tn
    dims = (((0 if trans_a else 1,), (1 if trans_b else 0,)), ((), ()))

    def body(a_ref, b_ref, o_ref, *scratch):
        av = a_ref[...]
        if a_silu:
            av = av * _sigmoid(av)
        part = lax.dot_general(av, b_ref[...], dims, preferred_element_type=_F32, precision=precision)
        if nk == 1:
            o_ref[...] = part.astype(o_ref.dtype)
        else:
            acc_ref, = scratch
            k = pl.program_id(2)

            @pl.when(k == 0)
            def _():
                acc_ref[...] = part

            @pl.when(k > 0)
            def _():
                acc_ref[...] += part

            @pl.when(k == nk - 1)
            def _():
                o_ref[...] = acc_ref[...].astype(o_ref.dtype)

    a_spec = (pl.BlockSpec((tk, tm), lambda i, j, k: (k, i)) if trans_a
              else pl.BlockSpec((tm, tk), lambda i, j, k: (i, k)))
    b_spec = (pl.BlockSpec((tn, tk), lambda i, j, k: (j, k)) if trans_b
              else pl.BlockSpec((tk, tn), lambda i, j, k: (k, j + jb)))
    return pl.pallas_call(
        body, name=name, grid=(M // tm, N // tn, nk),
        in_specs=[a_spec, b_spec],
        out_specs=pl.BlockSpec((tm, tn), lambda i, j, k: (i, j)),
        out_shape=jax.ShapeDtypeStruct((M, N), out_dtype),
        scratch_shapes=[pltpu.VMEM((tm, tn), _F32)] if nk > 1 else [],
        compiler_params=_cparams(("parallel", "parallel", "arbitrary")),
    )(a, b)


def resid_norm_fwd(x, a, sh, branch=None, gate=None, *, name):
    T, D = x.shape
    tm = _tile(T, NORM_TILE, BF16_ROWS)
    has_res = branch is not None

    def body(*refs):
        if has_res:
            x_ref, br_ref, g_ref, a_ref, sh_ref, xo_ref, h_ref = refs
            xv = x_ref[...] + g_ref[...] * br_ref[...]
            xo_ref[...] = xv
        else:
            x_ref, a_ref, sh_ref, h_ref = refs
            xv = x_ref[...]
        r = lax.rsqrt(jnp.mean(xv * xv, axis=-1, keepdims=True) + RMS_EPS)
        h_ref[...] = (xv * r * a_ref[...] + sh_ref[...]).astype(h_ref.dtype)

    row = pl.BlockSpec((tm, D), lambda i: (i, 0))
    vec = pl.BlockSpec((1, D), lambda i: (0, 0))
    if has_res:
        xo, h = pl.pallas_call(
            body, name=name, grid=(T // tm,), in_specs=[row, row, vec, vec, vec], out_specs=[row, row],
            out_shape=[jax.ShapeDtypeStruct((T, D), _F32), jax.ShapeDtypeStruct((T, D), _MM)],
            compiler_params=_cparams(("parallel",)))(x, branch, gate, a, sh)
        return xo, h
    h = pl.pallas_call(
        body, name=name, grid=(T // tm,), in_specs=[row, vec, vec], out_specs=row,
        out_shape=jax.ShapeDtypeStruct((T, D), _MM),
        compiler_params=_cparams(("parallel",)))(x, a, sh)
    return x, h


def norm_bwd(x, dh, dx_in, a, *, name):
    T, D = x.shape
    tm = _tile(T, NORM_TILE, BF16_ROWS)

    def body(x_ref, dh_ref, dxi_ref, a_ref, dxo_ref, dsh_ref, da_ref):
        i = pl.program_id(0)
        xv = x_ref[...]
        r = lax.rsqrt(jnp.mean(xv * xv, axis=-1, keepdims=True) + RMS_EPS)
        n = xv * r
        dhv = dh_ref[...].astype(_F32)
        dn = dhv * a_ref[...]
        dxo_ref[...] = dxi_ref[...] + r * (dn - n * jnp.mean(dn * n, axis=-1, keepdims=True))

        @pl.when(i == 0)
        def _():
            dsh_ref[...] = jnp.zeros_like(dsh_ref)
            da_ref[...] = jnp.zeros_like(da_ref)

        dsh_ref[...] += _colsum(dhv)
        da_ref[...] += _colsum(dhv * n)

    row = pl.BlockSpec((tm, D), lambda i: (i, 0))
    vec = pl.BlockSpec((1, D), lambda i: (0, 0))
    return pl.pallas_call(
        body, name=name, grid=(T // tm,), in_specs=[row, row, row, vec], out_specs=[row, vec, vec],
        out_shape=[jax.ShapeDtypeStruct((T, D), _F32), jax.ShapeDtypeStruct((1, D), _F32),
                   jax.ShapeDtypeStruct((1, D), _F32)],
        compiler_params=_cparams(("arbitrary",)))(x, dh, dx_in, a)


def gate_bwd(dx, branch, gate, *, name):
    T, D = dx.shape
    tm = _tile(T, NORM_TILE, BF16_ROWS)

    def body(dx_ref, br_ref, g_ref, db_ref, dg_ref):
        i = pl.program_id(0)
        dxv = dx_ref[...]
        db_ref[...] = (dxv * g_ref[...]).astype(db_ref.dtype)

        @pl.when(i == 0)
        def _():
            dg_ref[...] = jnp.zeros_like(dg_ref)

        dg_ref[...] += _colsum(dxv * br_ref[...])

    row = pl.BlockSpec((tm, D), lambda i: (i, 0))
    vec = pl.BlockSpec((1, D), lambda i: (0, 0))
    return pl.pallas_call(
        body, name=name, grid=(T // tm,), in_specs=[row, row, vec], out_specs=[row, vec],
        out_shape=[jax.ShapeDtypeStruct((T, D), _MM), jax.ShapeDtypeStruct((1, D), _F32)],
        compiler_params=_cparams(("arbitrary",)))(dx, branch, gate)


def final_loss_bwd(x, branch, gate, gfin, tgt, *, name):
    T, D = x.shape
    tm = _tile(T, NORM_TILE, BF16_ROWS)

    def body(x_ref, br_ref, g_ref, gf_ref, t_ref, dx_ref, loss_ref, dgf_ref):
        i = pl.program_id(0)
        xv = x_ref[...] + g_ref[...] * br_ref[...]
        r = lax.rsqrt(jnp.mean(xv * xv, axis=-1, keepdims=True) + RMS_EPS)
        n = xv * r
        e = n * gf_ref[...] - t_ref[...]
        dy = e * (1.0 / D)
        dn = dy * gf_ref[...]
        dx_ref[...] = r * (dn - n * jnp.mean(dn * n, axis=-1, keepdims=True))

        @pl.when(i == 0)
        def _():
            loss_ref[...] = jnp.zeros_like(loss_ref)
            dgf_ref[...] = jnp.zeros_like(dgf_ref)

        per_row = jnp.mean(e * e, axis=-1, keepdims=True)
        loss_ref[...] += jnp.broadcast_to(0.5 * _colsum(per_row), loss_ref.shape)
        dgf_ref[...] += _colsum(dy * n)

    row = pl.BlockSpec((tm, D), lambda i: (i, 0))
    vec = pl.BlockSpec((1, D), lambda i: (0, 0))
    lvec = pl.BlockSpec((1, LANES), lambda i: (0, 0))
    return pl.pallas_call(
        body, name=name, grid=(T // tm,), in_specs=[row, row, vec, vec, row], out_specs=[row, lvec, vec],
        out_shape=[jax.ShapeDtypeStruct((T, D), _F32), jax.ShapeDtypeStruct((1, LANES), _F32),
                   jax.ShapeDtypeStruct((1, D), _F32)],
        compiler_params=_cparams(("arbitrary",)))(x, branch, gate, gfin, tgt)


def _log_sigmoid(x):
    return jnp.minimum(x, 0.0) - jnp.log(1.0 + jnp.exp(-jnp.abs(x)))


def fgate_fwd(flog, bf, *, name):
    T = flog.shape[0]
    tt = _tile(T, 256)

    def body(x_ref, b_ref, f_ref, carry):
        i = pl.program_id(0)

        @pl.when(i == 0)
        def _():
            carry[...] = jnp.zeros_like(carry)

        lf = _log_sigmoid(x_ref[...] + b_ref[...])
        rows = lax.broadcasted_iota(jnp.int32, (tt, tt), 0)
        cols = lax.broadcasted_iota(jnp.int32, (tt, tt), 1)
        tri = (cols <= rows).astype(_F32)
        f_ref[...] = jnp.dot(tri, lf, preferred_element_type=_F32, precision=HIGHEST) + carry[0:1, :]
        carry[0:1, :] = f_ref[tt - 1:tt, :]

    return pl.pallas_call(
        body, name=name, grid=(T // tt,),
        in_specs=[pl.BlockSpec((tt, LANES), lambda i: (i, 0)), pl.BlockSpec((1, LANES), lambda i: (0, 0))],
        out_specs=pl.BlockSpec((tt, LANES), lambda i: (i, 0)),
        out_shape=jax.ShapeDtypeStruct((T, LANES), _F32),
        scratch_shapes=[pltpu.VMEM((SUBLANES, LANES), _F32)],
        compiler_params=_cparams(("arbitrary",)))(flog, bf)


def fgate_bwd(dfk, flog, bf, *, name):
    T = flog.shape[0]
    tt = _tile(T, 256)
    nb = T // tt

    def body(d_ref, x_ref, b_ref, o_ref, db_ref, carry):
        i = pl.program_id(0)

        @pl.when(i == 0)
        def _():
            carry[...] = jnp.zeros_like(carry)
            db_ref[...] = jnp.zeros_like(db_ref)

        rows = lax.broadcasted_iota(jnp.int32, (tt, tt), 0)
        cols = lax.broadcasted_iota(jnp.int32, (tt, tt), 1)
        upper = (cols >= rows).astype(_F32)
        dlf = jnp.dot(upper, d_ref[...], preferred_element_type=_F32, precision=HIGHEST) + carry[0:1, :]
        carry[0:1, :] = dlf[0:1, :]
        dfl = dlf * _sigmoid(-(x_ref[...] + b_ref[...]))
        o_ref[...] = dfl.astype(o_ref.dtype)
        db_ref[...] += _colsum(dfl)

    rev = pl.BlockSpec((tt, LANES), lambda i: (nb - 1 - i, 0))
    vec = pl.BlockSpec((1, LANES), lambda i: (0, 0))
    return pl.pallas_call(
        body, name=name, grid=(nb,), in_specs=[rev, rev, vec], out_specs=[rev, vec],
        out_shape=[jax.ShapeDtypeStruct((T, LANES), _MM), jax.ShapeDtypeStruct((1, LANES), _F32)],
        scratch_shapes=[pltpu.VMEM((SUBLANES, LANES), _F32)],
        compiler_params=_cparams(("arbitrary",)))(dfk, flog, bf)


def _att_scores(q, k, fq, fk, rep, masked):
    s = lax.dot_general(q, k, (((1,), (1,)), ((), ())), preferred_element_type=_F32)
    s = s * (HEAD_DIM ** -0.5) + (jnp.tile(fq, (1, rep)) - fk)
    if masked:
        rows = lax.broadcasted_iota(jnp.int32, s.shape, 0)
        cols = lax.broadcasted_iota(jnp.int32, s.shape, 1)
        s = jnp.where(cols <= rows, s, NEG)
    return s


def attn_fwd(proj, fq, fk, *, heads, name):
    T = proj.shape[0]
    H = heads
    tb = _tile(T, ATT_BLOCK)
    nb = T // tb
    rep = tb // LANES

    def body(q_ref, k_ref, v_ref, fq_ref, fk_ref, o_ref, lse_ref, m_s, l_s, acc_s):
        i = pl.program_id(1)
        j = pl.program_id(2)

        @pl.when(j == 0)
        def _():
            m_s[...] = jnp.full_like(m_s, NEG)
            l_s[...] = jnp.zeros_like(l_s)
            acc_s[...] = jnp.zeros_like(acc_s)

        def step(masked):
            s = _att_scores(q_ref[...], k_ref[...], fq_ref[...], fk_ref[...], rep, masked)
            m_prev = m_s[...]
            m_new = jnp.maximum(m_prev, jnp.max(s, axis=-1, keepdims=True))
            alpha = jnp.exp(m_prev - m_new)
            p = jnp.exp(s - jnp.tile(m_new, (1, rep)))
            l_s[...] = alpha * l_s[...] + jnp.sum(p, axis=-1, keepdims=True)
            v = v_ref[...]
            acc_s[...] = alpha * acc_s[...] + jnp.dot(p.astype(v.dtype), v, preferred_element_type=_F32)
            m_s[...] = m_new

        @pl.when(j < i)
        def _():
            step(False)

        @pl.when(j == i)
        def _():
            step(True)

        @pl.when(j == nb - 1)
        def _():
            o_ref[...] = (acc_s[...] / l_s[...]).astype(o_ref.dtype)
            lse_ref[...] = m_s[...] + jnp.log(l_s[...])

    qs = pl.BlockSpec((tb, HEAD_DIM), lambda h, i, j: (i, h))
    ks = pl.BlockSpec((tb, HEAD_DIM), lambda h, i, j: (jnp.minimum(j, i), H + h))
    vs = pl.BlockSpec((tb, HEAD_DIM), lambda h, i, j: (jnp.minimum(j, i), 2 * H + h))
    fqs = pl.BlockSpec((None, tb, LANES), lambda h, i, j: (h, i, 0))
    fks = pl.BlockSpec((None, 1, tb), lambda h, i, j: (h, 0, jnp.minimum(j, i)))
    return pl.pallas_call(
        body, name=name, grid=(H, nb, nb),
        in_specs=[qs, ks, vs, fqs, fks],
        out_specs=[pl.BlockSpec((tb, HEAD_DIM), lambda h, i, j: (i, h)), fqs],
        out_shape=[jax.ShapeDtypeStruct((T, H * HEAD_DIM), _MM), jax.ShapeDtypeStruct((H, T, LANES), _F32)],
        scratch_shapes=[pltpu.VMEM((tb, LANES), _F32), pltpu.VMEM((tb, LANES), _F32),
                        pltpu.VMEM((tb, HEAD_DIM), _F32)],
        compiler_params=_cparams(("parallel", "parallel", "arbitrary")))(proj, proj, proj, fq, fk)


def _att_p_ds(q, k, v, do, o, fq, fk, lse, rep, masked):
    s = _att_scores(q, k, fq, fk, rep, masked)
    p = jnp.exp(s - jnp.tile(lse, (1, rep)))
    delta = jnp.sum(do.astype(_F32) * o.astype(_F32), axis=-1, keepdims=True)
    dp = lax.dot_general(do, v, (((1,), (1,)), ((), ())), preferred_element_type=_F32)
    ds = p * (dp - delta)
    return p, ds


def attn_bwd_dkv(proj, dcat, attn, fq, fk, lse, *, heads, name):
    T = proj.shape[0]
    H = heads
    tb = _tile(T, ATT_BLOCK)
    nb = T // tb
    rep = tb // LANES
    scale = HEAD_DIM ** -0.5

    def body(q_ref, k_ref, v_ref, do_ref, o_ref, fq_ref, fk_ref, lse_ref, dk_ref, dv_ref, dfk_ref,
             dk_s, dv_s, dfk_s):
        j = pl.program_id(1)
        i = pl.program_id(2)

        @pl.when(i == 0)
        def _():
            dk_s[...] = jnp.zeros_like(dk_s)
            dv_s[...] = jnp.zeros_like(dv_s)
            dfk_s[...] = jnp.zeros_like(dfk_s)

        def step(masked):
            q = q_ref[...]
            do = do_ref[...]
            p, ds = _att_p_ds(q, k_ref[...], v_ref[...], do, o_ref[...], fq_ref[...], fk_ref[...],
                              lse_ref[...], rep, masked)
            tn = (((0,), (0,)), ((), ()))
            dv_s[...] += lax.dot_general(p.astype(do.dtype), do, tn, preferred_element_type=_F32)
            dk_s[...] += lax.dot_general(ds.astype(q.dtype), q, tn, preferred_element_type=_F32)
            dfk_s[0:1, :] += -_colsum(ds)

        @pl.when(i > j)
        def _():
            step(False)

        @pl.when(i == j)
        def _():
            step(True)

        @pl.when(i == nb - 1)
        def _():
            dk_ref[...] = (dk_s[...] * scale).astype(dk_ref.dtype)
            dv_ref[...] = dv_s[...].astype(dv_ref.dtype)
            dfk_ref[...] = dfk_s[0:1, :]

    qrow = lambda h, j, i: (jnp.maximum(i, j), h)
    qs = pl.BlockSpec((tb, HEAD_DIM), qrow)
    ks = pl.BlockSpec((tb, HEAD_DIM), lambda h, j, i: (j, H + h))
    vs = pl.BlockSpec((tb, HEAD_DIM), lambda h, j, i: (j, 2 * H + h))
    stat = pl.BlockSpec((None, tb, LANES), lambda h, j, i: (h, jnp.maximum(i, j), 0))
    fks = pl.BlockSpec((None, 1, tb), lambda h, j, i: (h, 0, j))
    kout = pl.BlockSpec((tb, HEAD_DIM), lambda h, j, i: (j, h))
    A = H * HEAD_DIM
    return pl.pallas_call(
        body, name=name, grid=(H, nb, nb),
        in_specs=[qs, ks, vs, qs, qs, stat, fks, stat],
        out_specs=[kout, kout, fks],
        out_shape=[jax.ShapeDtypeStruct((T, A), _MM), jax.ShapeDtypeStruct((T, A), _MM),
                   jax.ShapeDtypeStruct((H, 1, T), _F32)],
        scratch_shapes=[pltpu.VMEM((tb, HEAD_DIM), _F32), pltpu.VMEM((tb, HEAD_DIM), _F32),
                        pltpu.VMEM((SUBLANES, tb), _F32)],
        compiler_params=_cparams(("parallel", "parallel", "arbitrary")))(
            proj, proj, proj, dcat, attn, fq, fk, lse)


def attn_bwd_dq(proj, dcat, attn, fq, fk, lse, *, heads, name):
    T = proj.shape[0]
    H = heads
    tb = _tile(T, ATT_BLOCK)
    nb = T // tb
    rep = tb // LANES
    scale = HEAD_DIM ** -0.5

    def body(q_ref, k_ref, v_ref, do_ref, o_ref, fq_ref, fk_ref, lse_ref, dq_ref, dfq_ref, dq_s, dfq_s):
        i = pl.program_id(1)
        j = pl.program_id(2)

        @pl.when(j == 0)
        def _():
            dq_s[...] = jnp.zeros_like(dq_s)
            dfq_s[...] = jnp.zeros_like(dfq_s)

        def step(masked):
            k = k_ref[...]
            _, ds = _att_p_ds(q_ref[...], k, v_ref[...], do_ref[...], o_ref[...], fq_ref[...], fk_ref[...],
                              lse_ref[...], rep, masked)
            dq_s[...] += jnp.dot(ds.astype(k.dtype), k, preferred_element_type=_F32)
            dfq_s[...] += jnp.sum(ds, axis=-1, keepdims=True)

        @pl.when(j < i)
        def _():
            step(False)

        @pl.when(j == i)
        def _():
            step(True)

        @pl.when(j == nb - 1)
        def _():
            dq_ref[...] = (dq_s[...] * scale).astype(dq_ref.dtype)
            dfq_ref[...] = dfq_s[...]

    qs = pl.BlockSpec((tb, HEAD_DIM), lambda h, i, j: (i, h))
    ks = pl.BlockSpec((tb, HEAD_DIM), lambda h, i, j: (jnp.minimum(j, i), H + h))
    vs = pl.BlockSpec((tb, HEAD_DIM), lambda h, i, j: (jnp.minimum(j, i), 2 * H + h))
    stat = pl.BlockSpec((None, tb, LANES), lambda h, i, j: (h, i, 0))
    fks = pl.BlockSpec((None, 1, tb), lambda h, i, j: (h, 0, jnp.minimum(j, i)))
    return pl.pallas_call(
        body, name=name, grid=(H, nb, nb),
        in_specs=[qs, ks, vs, qs, qs, stat, fks, stat],
        out_specs=[qs, stat],
        out_shape=[jax.ShapeDtypeStruct((T, H * HEAD_DIM), _MM), jax.ShapeDtypeStruct((H, T, LANES), _F32)],
        scratch_shapes=[pltpu.VMEM((tb, HEAD_DIM), _F32), pltpu.VMEM((tb, LANES), _F32)],
        compiler_params=_cparams(("parallel", "parallel", "arbitrary")))(
            proj, proj, proj, dcat, attn, fq, fk, lse)


def _causal_taps(w_ref, scr, halo, tt, width):
    acc = w_ref[width - 1:width, :] * scr[halo:halo + tt, :]
    for j in range(1, width):
        acc = acc + w_ref[width - 1 - j:width - j, :] * scr[halo - j:halo - j + tt, :]
    return acc


def _anticausal_taps(w_ref, scr, tt, width):
    acc = w_ref[width - 1:width, :] * scr[0:tt, :]
    for j in range(1, width):
        acc = acc + w_ref[width - 1 - j:width - j, :] * scr[j:j + tt, :]
    return acc


def _ln_fwd(cc, g, b):
    mu = jnp.mean(cc, axis=-1, keepdims=True)
    xc = cc - mu
    rstd = lax.rsqrt(jnp.mean(xc * xc, axis=-1, keepdims=True) + LN_EPS)
    xhat = xc * rstd
    return xhat, rstd, xhat * g + b


def _ln_silu_bwd(cc, dconf, g, b):
    xhat, rstd, ln = _ln_fwd(cc, g, b)
    s = _sigmoid(ln)
    dln = dconf * (s * (1.0 + ln * (1.0 - s)))
    dxh = dln * g
    dcc = rstd * (dxh - jnp.mean(dxh, axis=-1, keepdims=True)
                  - xhat * jnp.mean(dxh * xhat, axis=-1, keepdims=True))
    return dcc, dln, xhat


def mixer_misc_fwd(proj, cw, cb, lg, lb, sw, *, width, base_col, name):
    T = proj.shape[0]
    C = width
    tt = _tile(T, CONV_TILE)
    HB = CONV_HALO
    per = tt // HB
    KC, KS = cw.shape[0], sw.shape[0]
    b0 = base_col // C

    def body(cv, cg, sx, sb, sc, cvh, cgh, sxh, sch, cw_ref, cb_ref, lg_ref, lb_ref, sw_ref,
             cm_ref, cc_ref, zc_ref, gscr, zscr):
        keep = (pl.program_id(0) > 0).astype(_F32)
        f = lambda r: r[...].astype(_F32)
        gscr[0:HB, :] = f(cvh) * _sigmoid(f(cgh)) * keep
        gscr[HB:HB + tt, :] = f(cv) * _sigmoid(f(cg))
        cc = _causal_taps(cw_ref, gscr, HB, tt, KC) + cb_ref[...]
        cc_ref[...] = cc
        _, _, ln = _ln_fwd(cc, lg_ref[...], lb_ref[...])
        cm_ref[:, 0:C] = (ln * _sigmoid(ln)).astype(cm_ref.dtype)
        zscr[0:HB, :] = f(sch) * f(sxh) * keep
        zscr[HB:HB + tt, :] = f(sc) * f(sx)
        zc = _causal_taps(sw_ref, zscr, HB, tt, KS)
        zc_ref[...] = zc
        cm_ref[:, C:2 * C] = (f(sb) * zc).astype(cm_ref.dtype)

    main = lambda k: pl.BlockSpec((tt, C), lambda i: (i, b0 + k))
    halo = lambda k: pl.BlockSpec((HB, C), lambda i: (jnp.maximum(i * per - 1, 0), b0 + k))
    full = lambda a: pl.BlockSpec(a.shape, lambda i: (0, 0))
    return pl.pallas_call(
        body, name=name, grid=(T // tt,),
        in_specs=[main(0), main(1), main(2), main(3), main(4), halo(0), halo(1), halo(2), halo(4),
                  full(cw), full(cb), full(lg), full(lb), full(sw)],
        out_specs=[pl.BlockSpec((tt, 2 * C), lambda i: (i, 0)), pl.BlockSpec((tt, C), lambda i: (i, 0)),
                   pl.BlockSpec((tt, C), lambda i: (i, 0))],
        out_shape=[jax.ShapeDtypeStruct((T, 2 * C), _MM), jax.ShapeDtypeStruct((T, C), _F32),
                   jax.ShapeDtypeStruct((T, C), _F32)],
        scratch_shapes=[pltpu.VMEM((tt + HB, C), _F32), pltpu.VMEM((tt + HB, C), _F32)],
        compiler_params=_cparams(("parallel",)))(
            proj, proj, proj, proj, proj, proj, proj, proj, proj, cw, cb, lg, lb, sw)


def mixer_misc_bwd(proj, dcat, cc, zc, cw, lg, lb, sw, *, width, base_col, dbase_col, name):
    T = proj.shape[0]
    C = width
    tt = _tile(T, CONV_TILE)
    nt = T // tt
    HB = CONV_HALO
    per = tt // HB
    KC, KS = cw.shape[0], sw.shape[0]
    b0 = base_col // C
    d0 = dbase_col // C
    last_hb = T // HB - 1

    def body(cv, cg, sx, sb, sc, cvh, cgh, sxh, sch, sbn, dcf, dsv, dcfn, dsvn, cc_ref, ccn_ref, zc_ref,
             cw_ref, lg_ref, lb_ref, sw_ref,
             dm_ref, dcw_ref, dcb_ref, dlg_ref, dlb_ref, dsw_ref, gscr, dscr, zscr, zdscr):
        i = pl.program_id(0)
        keep = (i > 0).astype(_F32)
        ahead = (i < nt - 1).astype(_F32)
        f = lambda r: r[...].astype(_F32)

        @pl.when(i == 0)
        def _():
            for r in (dcw_ref, dcb_ref, dlg_ref, dlb_ref, dsw_ref):
                r[...] = jnp.zeros_like(r)

        g, b = lg_ref[...], lb_ref[...]
        dcc, dln, xhat = _ln_silu_bwd(cc_ref[...], f(dcf), g, b)
        dcc_next, _, _ = _ln_silu_bwd(ccn_ref[...], f(dcfn), g, b)
        dlg_ref[...] += _colsum(dln * xhat)
        dlb_ref[...] += _colsum(dln)
        dcb_ref[...] += _colsum(dcc)
        dscr[0:tt, :] = dcc
        dscr[tt:tt + HB, :] = dcc_next * ahead
        dglu = _anticausal_taps(cw_ref, dscr, tt, KC)
        cvv = f(cv)
        sig = _sigmoid(f(cg))
        dm_ref[:, 0:C] = (dglu * sig).astype(dm_ref.dtype)
        dm_ref[:, C:2 * C] = (dglu * cvv * sig * (1.0 - sig)).astype(dm_ref.dtype)
        gscr[0:HB, :] = f(cvh) * _sigmoid(f(cgh)) * keep
        gscr[HB:HB + tt, :] = cvv * sig
        for j in range(KC):
            dcw_ref[KC - 1 - j:KC - j, :] += _colsum(dcc * gscr[HB - j:HB - j + tt, :])
        dsc_out = f(dsv)
        sbv = f(sb)
        dzc = dsc_out * sbv
        zdscr[0:tt, :] = dzc
        zdscr[tt:tt + HB, :] = f(dsvn) * f(sbn) * ahead
        dz = _anticausal_taps(sw_ref, zdscr, tt, KS)
        sxv, scv = f(sx), f(sc)
        dm_ref[:, 2 * C:3 * C] = (dz * scv).astype(dm_ref.dtype)
        dm_ref[:, 3 * C:4 * C] = (dsc_out * zc_ref[...]).astype(dm_ref.dtype)
        dm_ref[:, 4 * C:5 * C] = (dz * sxv).astype(dm_ref.dtype)
        zscr[0:HB, :] = f(sch) * f(sxh) * keep
        zscr[HB:HB + tt, :] = scv * sxv
        for j in range(KS):
            dsw_ref[KS - 1 - j:KS - j, :] += _colsum(dzc * zscr[HB - j:HB - j + tt, :])

    main = lambda col: pl.BlockSpec((tt, C), lambda i: (i, col))
    prev = lambda col: pl.BlockSpec((HB, C), lambda i: (jnp.maximum(i * per - 1, 0), col))
    nxt = lambda col: pl.BlockSpec((HB, C), lambda i: (jnp.minimum((i + 1) * per, last_hb), col))
    full = lambda a: pl.BlockSpec(a.shape, lambda i: (0, 0))
    vec = pl.BlockSpec((1, C), lambda i: (0, 0))
    return pl.pallas_call(
        body, name=name, grid=(nt,),
        in_specs=[main(b0), main(b0 + 1), main(b0 + 2), main(b0 + 3), main(b0 + 4),
                  prev(b0), prev(b0 + 1), prev(b0 + 2), prev(b0 + 4), nxt(b0 + 3),
                  main(d0), main(d0 + 1), nxt(d0), nxt(d0 + 1),
                  main(0), nxt(0), main(0),
                  full(cw), full(lg), full(lb), full(sw)],
        out_specs=[pl.BlockSpec((tt, 5 * C), lambda i: (i, 0)), full(cw), vec, vec, vec, full(sw)],
        out_shape=[jax.ShapeDtypeStruct((T, 5 * C), _MM), jax.ShapeDtypeStruct(cw.shape, _F32),
                   jax.ShapeDtypeStruct((1, C), _F32), jax.ShapeDtypeStruct((1, C), _F32),
                   jax.ShapeDtypeStruct((1, C), _F32), jax.ShapeDtypeStruct(sw.shape, _F32)],
        scratch_shapes=[pltpu.VMEM((tt + HB, C), _F32)] * 4,
        compiler_params=_cparams(("arbitrary",)))(
            proj, proj, proj, proj, proj, proj, proj, proj, proj, proj,
            dcat, dcat, dcat, dcat, cc, cc, zc, cw, lg, lb, sw)


def _ffn_u(main_ref, halo_ref, w_ref, b_ref, scr, keep, tt, width):
    HB = FFN_HALO
    scr[0:HB, :] = halo_ref[...].astype(_F32) * keep
    scr[HB:HB + tt, :] = main_ref[...].astype(_F32)
    return _causal_taps(w_ref, scr, HB, tt, width) + b_ref[...]


def ffn_act_fwd(up, w, b, *, name):
    T, F2 = up.shape
    F = F2 // 2
    K = w.shape[0]
    tt = _tile(T, CONV_TILE)
    tc = _tile(F, 512)
    nb = F // tc
    per = tt // FFN_HALO

    def body(g_ref, v_ref, gh_ref, vh_ref, wg_ref, wv_ref, bg_ref, bv_ref, o_ref, gscr, vscr):
        keep = (pl.program_id(0) > 0).astype(_F32)
        ug = _ffn_u(g_ref, gh_ref, wg_ref, bg_ref, gscr, keep, tt, K)
        uv = _ffn_u(v_ref, vh_ref, wv_ref, bv_ref, vscr, keep, tt, K)
        o_ref[...] = (ug * _sigmoid(ug) * uv).astype(o_ref.dtype)

    main = lambda off: pl.BlockSpec((tt, tc), lambda i, j: (i, j + off))
    halo = lambda off: pl.BlockSpec((FFN_HALO, tc), lambda i, j: (jnp.maximum(i * per - 1, 0), j + off))
    wsp = lambda off: pl.BlockSpec((K, tc), lambda i, j: (0, j + off))
    bsp = lambda off: pl.BlockSpec((1, tc), lambda i, j: (0, j + off))
    return pl.pallas_call(
        body, name=name, grid=(T // tt, nb),
        in_specs=[main(0), main(nb), halo(0), halo(nb), wsp(0), wsp(nb), bsp(0), bsp(nb)],
        out_specs=pl.BlockSpec((tt, tc), lambda i, j: (i, j)),
        out_shape=jax.ShapeDtypeStruct((T, F), _MM),
        scratch_shapes=[pltpu.VMEM((tt + FFN_HALO, tc), _F32)] * 2,
        compiler_params=_cparams(("parallel", "parallel")))(up, up, up, up, w, w, b, b)


def ffn_bwd_du(up, dact, w, b, *, name):
    T, F2 = up.shape
    F = F2 // 2
    K = w.shape[0]
    tt = _tile(T, CONV_TILE)
    tc = _tile(F, 512)
    nb = F // tc
    per = tt // FFN_HALO
    HB = FFN_HALO

    def body(g_ref, v_ref, gh_ref, vh_ref, da_ref, wg_ref, wv_ref, bg_ref, bv_ref, du_ref, dwb_ref,
             gscr, vscr):
        jj = pl.program_id(0)
        i = pl.program_id(1)
        keep = (i > 0).astype(_F32)
        ug = _ffn_u(g_ref, gh_ref, wg_ref, bg_ref, gscr, keep, tt, K)
        uv = _ffn_u(v_ref, vh_ref, wv_ref, bv_ref, vscr, keep, tt, K)
        s = _sigmoid(ug)
        da = da_ref[...].astype(_F32)
        sel = (jj < nb).astype(_F32)
        du = sel * (da * uv * s * (1.0 + ug * (1.0 - s))) + (1.0 - sel) * (da * ug * s)
        du_ref[...] = du.astype(du_ref.dtype)

        @pl.when(i == 0)
        def _():
            dwb_ref[...] = jnp.zeros_like(dwb_ref)

        for j in range(K):
            xs = sel * gscr[HB - j:HB - j + tt, :] + (1.0 - sel) * vscr[HB - j:HB - j + tt, :]
            dwb_ref[K - 1 - j:K - j, :] += _colsum(du * xs)
        dwb_ref[K:K + 1, :] += _colsum(du)

    gcol = lambda jj: lax.rem(jj, nb)
    main = lambda off: pl.BlockSpec((tt, tc), lambda jj, i: (i, gcol(jj) + off))
    halo = lambda off: pl.BlockSpec((HB, tc), lambda jj, i: (jnp.maximum(i * per - 1, 0), gcol(jj) + off))
    wsp = lambda off: pl.BlockSpec((K, tc), lambda jj, i: (0, gcol(jj) + off))
    bsp = lambda off: pl.BlockSpec((1, tc), lambda jj, i: (0, gcol(jj) + off))
    return pl.pallas_call(
        body, name=name, grid=(2 * nb, T // tt),
        in_specs=[main(0), main(nb), halo(0), halo(nb), pl.BlockSpec((tt, tc), lambda jj, i: (i, gcol(jj))),
                  wsp(0), wsp(nb), bsp(0), bsp(nb)],
        out_specs=[pl.BlockSpec((tt, tc), lambda jj, i: (i, jj)),
                   pl.BlockSpec((SUBLANES, tc), lambda jj, i: (0, jj))],
        out_shape=[jax.ShapeDtypeStruct((T, F2), _MM), jax.ShapeDtypeStruct((SUBLANES, F2), _F32)],
        scratch_shapes=[pltpu.VMEM((tt + HB, tc), _F32)] * 2,
        compiler_params=_cparams(("parallel", "arbitrary")))(up, up, up, up, dact, w, w, b, b)


def dwconv_transpose(du, w, *, name):
    T, N = du.shape
    K = w.shape[0]
    tt = _tile(T, CONV_TILE)
    tc = _tile(N, 512)
    per = tt // FFN_HALO
    HB = FFN_HALO
    nt = T // tt
    last_hb = T // HB - 1

    def body(d_ref, dn_ref, w_ref, o_ref, scr):
        ahead = (pl.program_id(0) < nt - 1).astype(_F32)
        scr[0:tt, :] = d_ref[...].astype(_F32)
        scr[tt:tt + HB, :] = dn_ref[...].astype(_F32) * ahead
        o_ref[...] = _anticausal_taps(w_ref, scr, tt, K).astype(o_ref.dtype)

    return pl.pallas_call(
        body, name=name, grid=(nt, N // tc),
        in_specs=[pl.BlockSpec((tt, tc), lambda i, j: (i, j)),
                  pl.BlockSpec((HB, tc), lambda i, j: (jnp.minimum((i + 1) * per, last_hb), j)),
                  pl.BlockSpec((K, tc), lambda i, j: (0, j))],
        out_specs=pl.BlockSpec((tt, tc), lambda i, j: (i, j)),
        out_shape=jax.ShapeDtypeStruct((T, N), _MM),
        scratch_shapes=[pltpu.VMEM((tt + HB, tc), _F32)],
        compiler_params=_cparams(("parallel", "parallel")))(du, du, w)


def _adamw_math(w, g, m, v):
    m = ADAM_B1 * m + (1.0 - ADAM_B1) * g
    v = ADAM_B2 * v + (1.0 - ADAM_B2) * (g * g)
    m_hat = m / (1.0 - ADAM_B1 ** ADAM_STEP)
    v_hat = v / (1.0 - ADAM_B2 ** ADAM_STEP)
    delta = -ADAM_LR * (m_hat / (jnp.sqrt(v_hat) + ADAM_EPS) + ADAM_WD * w)
    return delta, m, v


def _as2d(a):
    return a.reshape(1, -1) if a.ndim == 1 else a.reshape(-1, a.shape[-1])


def adamw(w, g, m, v, *, name):
    shape = w.shape
    w2, g2, m2, v2 = _as2d(w), _as2d(g), _as2d(m), _as2d(v)
    R, C = w2.shape
    lanes = -(-C // LANES) * LANES
    tr = _tile(R, max(SUBLANES, (1 << 20) // (4 * lanes)), SUBLANES)

    def body(w_ref, g_ref, m_ref, v_ref, d_ref, mo_ref, vo_ref):
        d, mn, vn = _adamw_math(w_ref[...], g_ref[...], m_ref[...], v_ref[...])
        d_ref[...] = d
        mo_ref[...] = mn
        vo_ref[...] = vn

    blk = pl.BlockSpec((tr, C), lambda i: (i, 0))
    outs = pl.pallas_call(
        body, name=name, grid=(R // tr,), in_specs=[blk] * 4, out_specs=[blk] * 3,
        out_shape=[jax.ShapeDtypeStruct((R, C), _F32)] * 3,
        compiler_params=_cparams(("parallel",)))(w2, g2, m2, v2)
    return tuple(o.reshape(shape) for o in outs)


def ada_w_update(c_t, d_ada, w, m, v, *, name):
    L, D, N = w.shape
    B = c_t.shape[1]
    tr = _tile(D, 256, SUBLANES)
    tn = _tile(N, 1024)

    def body(c_ref, a_ref, w_ref, m_ref, v_ref, g_ref, d_ref, mo_ref, vo_ref):
        cv = c_ref[...]
        cv = cv * _sigmoid(cv)
        g = jnp.dot(cv, a_ref[...], preferred_element_type=_F32, precision=HIGHEST)
        g_ref[...] = g
        d, mn, vn = _adamw_math(w_ref[...], g, m_ref[...], v_ref[...])
        d_ref[...] = d
        mo_ref[...] = mn
        vo_ref[...] = vn

    blk = pl.BlockSpec((None, tr, tn), lambda l, i, j: (l, i, j))
    return pl.pallas_call(
        body, name=name, grid=(L, D // tr, N // tn),
        in_specs=[pl.BlockSpec((tr, B), lambda l, i, j: (i, 0)),
                  pl.BlockSpec((None, B, tn), lambda l, i, j: (l, 0, j)), blk, blk, blk],
        out_specs=[blk] * 4,
        out_shape=[jax.ShapeDtypeStruct((L, D, N), _F32)] * 4,
        compiler_params=_cparams(("parallel", "parallel", "parallel")))(c_t, d_ada, w, m, v)


def _coords():
    return lax.axis_index("x"), lax.axis_index("y"), lax.axis_index("c")


def allgather_small(x, *, with_sum, name):
    R, C = x.shape

    def body(x_ref, out_ref, *rest):
        if with_sum:
            sum_ref, send_sems, recv_sems, local_sem = rest
        else:
            send_sems, recv_sems, local_sem = rest
        px, py, pc = _coords()
        me, sibling = (px, py, pc), (px, py, 1 - pc)
        chips = [(1 - px, py), (px, 1 - py), (1 - px, 1 - py)]

        def rows(qx, qy, qc):
            return out_ref.at[4 * qx + 2 * qy + qc]

        def copy(k, block, to, src=None):
            return pltpu.make_async_remote_copy(
                src_ref=rows(*block) if src is None else src, dst_ref=rows(*block),
                send_sem=send_sems.at[k], recv_sem=recv_sems.at[k], device_id=to, device_id_type=MESH)

        mine = pltpu.make_async_copy(x_ref, rows(*me), local_sem)
        mine.start()
        first = [copy(0, me, sibling, src=x_ref)]
        first += [copy(1 + j, me, (*chip, pc), src=x_ref) for j, chip in enumerate(chips)]
        for cp in first:
            cp.start()
        passed = [copy(4 + j, (*chip, pc), sibling) for j, chip in enumerate(chips)]
        for j, chip in enumerate(chips):
            copy(1 + j, (*chip, pc), me).wait_recv()
            passed[j].start()
        copy(0, sibling, me).wait_recv()
        for j, chip in enumerate(chips):
            copy(4 + j, (*chip, 1 - pc), me).wait_recv()
        for cp in first + passed:
            cp.wait_send()
        mine.wait()
        if with_sum:
            acc = out_ref[0]
            for k in range(1, N_DEV):
                acc = acc + out_ref[k]
            sum_ref[...] = acc

    vm = pl.BlockSpec(memory_space=pltpu.VMEM)
    out_shape = [jax.ShapeDtypeStruct((N_DEV, R, C), x.dtype)]
    if with_sum:
        out_shape.append(jax.ShapeDtypeStruct((R, C), x.dtype))
    outs = pl.pallas_call(
        body, name=name, in_specs=[vm], out_specs=[vm] * len(out_shape), out_shape=out_shape,
        scratch_shapes=[pltpu.SemaphoreType.DMA((7,)), pltpu.SemaphoreType.DMA((7,)), pltpu.SemaphoreType.DMA],
        compiler_params=pltpu.CompilerParams(vmem_limit_bytes=VMEM_LIMIT_V7X))(x)
    return outs if with_sum else outs[0]


def _at(start, size, align):
    return pl.ds(pl.multiple_of(start, align) if align > 1 else start, size)


class _BigLayout:
    def __init__(self, D, INs, Ds, F2s, Fs):
        self.D, self.INs, self.Ds, self.F2s, self.Fs = D, INs, Ds, F2s, Fs
        self.Dh, self.Dsh, self.Fsh = D // 2, Ds // 2, Fs // 2
        self.piece_shapes = [(self.Dh, INs), (self.Dsh, D), (self.Dh, F2s), (self.Fsh, D)]

    def in_full(self, a, ref, k, h):
        if a == 0:
            return ref.at[k, _at(h * self.Dh, self.Dh, self.Dh), :]
        if a == 1:
            return ref.at[_at(k * self.Ds + h * self.Dsh, self.Dsh, self.Dsh), :]
        if a == 2:
            return ref.at[_at(h * self.Dh, self.Dh, self.Dh), _at(k * self.F2s, self.F2s, self.F2s)]
        return ref.at[_at(k * self.Fs + h * self.Fsh, self.Fsh, self.Fsh), :]

    def in_shard(self, a, ref, h):
        rows = self.piece_shapes[a][0]
        return ref.at[_at(h * rows, rows, rows), :]


def gather_weights(shards, *, name):
    D, INs = shards[0].shape
    lay = _BigLayout(D, INs, shards[1].shape[0], shards[2].shape[1], shards[3].shape[0])
    n_arr = len(shards)

    def body(*refs):
        ins, outs = refs[:n_arr], refs[n_arr:2 * n_arr]
        send_sems, recv_sems, local_sems = refs[2 * n_arr:]
        px, py, pc = _coords()
        my_chip = 2 * px + py
        sibling = (px, py, 1 - pc)
        chips = [(1 - px, py), (px, 1 - py), (1 - px, 1 - py)]

        def copy(a, kk, k, h, to, src=None):
            dst = lay.in_full(a, outs[a], k, h)
            return pltpu.make_async_remote_copy(
                src_ref=dst if src is None else src, dst_ref=dst,
                send_sem=send_sems.at[7 * a + kk], recv_sem=recv_sems.at[7 * a + kk],
                device_id=to, device_id_type=MESH)

        sends, mine = [], []
        for a in range(n_arr):
            own = lay.in_shard(a, ins[a], pc)
            mine.append(pltpu.make_async_copy(own, lay.in_full(a, outs[a], my_chip, pc), local_sems.at[a]))
            sends.append(copy(a, 0, my_chip, pc, sibling, src=own))
            sends += [copy(a, 1 + j, my_chip, pc, (*chip, pc), src=own) for j, chip in enumerate(chips)]
        for cp in mine + sends:
            cp.start()
        for j, (qx, qy) in enumerate(chips):
            for a in range(n_arr):
                copy(a, 1 + j, 2 * qx + qy, pc, sibling).wait_recv()
                fwd = copy(a, 4 + j, 2 * qx + qy, pc, sibling)
                fwd.start()
                sends.append(fwd)
        for a in range(n_arr):
            copy(a, 0, my_chip, 1 - pc, sibling).wait_recv()
            for j, (qx, qy) in enumerate(chips):
                copy(a, 4 + j, 2 * qx + qy, 1 - pc, sibling).wait_recv()
        for cp in sends:
            cp.wait_send()
        for cp in mine:
            cp.wait()

    hbm = pl.BlockSpec(memory_space=pl.ANY)
    dt = shards[0].dtype
    out_shape = [jax.ShapeDtypeStruct((N_CHIP, D, INs), dt), jax.ShapeDtypeStruct((D, D), dt),
                 jax.ShapeDtypeStruct((D, lay.F2s * N_CHIP), dt), jax.ShapeDtypeStruct((lay.Fs * N_CHIP, D), dt)]
    return pl.pallas_call(
        body, name=name, in_specs=[hbm] * n_arr, out_specs=[hbm] * n_arr, out_shape=out_shape,
        scratch_shapes=[pltpu.SemaphoreType.DMA((7 * n_arr,)), pltpu.SemaphoreType.DMA((7 * n_arr,)),
                        pltpu.SemaphoreType.DMA((n_arr,))],
    )(*shards)


def scatter_grads(partials, *, name):
    _, D, INs = partials[0].shape
    lay = _BigLayout(D, INs, partials[1].shape[0] // N_CHIP, partials[2].shape[1] // N_CHIP,
                     partials[3].shape[0] // N_CHIP)
    n_arr = len(partials)

    def body(*refs):
        ins, outs = refs[:n_arr], refs[n_arr:2 * n_arr]
        send_sems, recv_sems, local_sems = refs[2 * n_arr:]
        px, py, pc = _coords()
        me = 4 * px + 2 * py + pc
        copies, mine = [], []
        for a in range(n_arr):
            mine.append(pltpu.make_async_copy(lay.in_full(a, ins[a], 2 * px + py, pc), outs[a].at[me],
                                              local_sems.at[a]))
            for mask in range(1, N_DEV):
                qx = 1 - px if (mask >> 2) & 1 else px
                qy = 1 - py if (mask >> 1) & 1 else py
                qc = 1 - pc if mask & 1 else pc
                copies.append(pltpu.make_async_remote_copy(
                    src_ref=lay.in_full(a, ins[a], 2 * qx + qy, qc), dst_ref=outs[a].at[me],
                    send_sem=send_sems.at[7 * a + mask - 1], recv_sem=recv_sems.at[7 * a + mask - 1],
                    device_id=(qx, qy, qc), device_id_type=MESH))
        for cp in mine + copies:
            cp.start()
        for cp in copies:
            cp.wait_recv()
        for cp in copies:
            cp.wait_send()
        for cp in mine:
            cp.wait()

    hbm = pl.BlockSpec(memory_space=pl.ANY)
    dt = partials[0].dtype
    return pl.pallas_call(
        body, name=name, in_specs=[hbm] * n_arr, out_specs=[hbm] * n_arr,
        out_shape=[jax.ShapeDtypeStruct((N_DEV, *s), dt) for s in lay.piece_shapes],
        scratch_shapes=[pltpu.SemaphoreType.DMA((7 * n_arr,)), pltpu.SemaphoreType.DMA((7 * n_arr,)),
                        pltpu.SemaphoreType.DMA((n_arr,))],
    )(*partials)


SIBLING_CHUNKS = 4


def sibling_exchange(halves, *, name):
    n_arr = len(halves)
    n_ch = [max(n for n in (SIBLING_CHUNKS, 2, 1) if x.shape[0] % (n * SUBLANES) == 0 or n == 1) for x in halves]
    offs = [sum(n_ch[:a]) for a in range(n_arr)]

    def body(*refs):
        ins, outs = refs[:n_arr], refs[n_arr:2 * n_arr]
        send_sems, recv_sems, local_sems = refs[2 * n_arr:]
        px, py, pc = _coords()
        copies, mine = [], []
        for a in range(n_arr):
            mine.append(pltpu.make_async_copy(ins[a], outs[a].at[pc], local_sems.at[a]))
            rows = halves[a].shape[0] // n_ch[a]
            for q in range(n_ch[a]):
                copies.append(pltpu.make_async_remote_copy(
                    src_ref=ins[a].at[pl.ds(q * rows, rows), :], dst_ref=outs[a].at[pc, pl.ds(q * rows, rows), :],
                    send_sem=send_sems.at[offs[a] + q], recv_sem=recv_sems.at[offs[a] + q],
                    device_id=(px, py, 1 - pc), device_id_type=MESH))
        for cp in mine + copies:
            cp.start()
        for cp in copies:
            cp.wait_recv()
        for cp in copies:
            cp.wait_send()
        for cp in mine:
            cp.wait()

    hbm = pl.BlockSpec(memory_space=pl.ANY)
    return pl.pallas_call(
        body, name=name, in_specs=[hbm] * n_arr, out_specs=[hbm] * n_arr,
        out_shape=[jax.ShapeDtypeStruct((2, *x.shape), x.dtype) for x in halves],
        scratch_shapes=[pltpu.SemaphoreType.DMA((sum(n_ch),)), pltpu.SemaphoreType.DMA((sum(n_ch),)),
                        pltpu.SemaphoreType.DMA((n_arr,))],
    )(*halves)


def sum_slots(x, *, name):
    n, R, C = x.shape
    lanes = -(-C // LANES) * LANES
    tr = _tile(R, max(BF16_ROWS, (4 << 20) // (n * 2 * lanes)), BF16_ROWS)

    def body(x_ref, o_ref):
        acc = x_ref[0].astype(_F32)
        for k in range(1, n):
            acc = acc + x_ref[k].astype(_F32)
        o_ref[...] = acc

    return pl.pallas_call(
        body, name=name, grid=(R // tr,),
        in_specs=[pl.BlockSpec((n, tr, C), lambda i: (0, i, 0))],
        out_specs=pl.BlockSpec((tr, C), lambda i: (i, 0)),
        out_shape=jax.ShapeDtypeStruct((R, C), _F32),
        compiler_params=_cparams(("parallel",)))(x)


def _pack_flat(arrays, quantum):
    flat = jnp.concatenate([a.reshape(-1) for a in arrays])
    pad = (-flat.shape[0]) % quantum
    return jnp.pad(flat, (0, pad)) if pad else flat


def _unpack_flat(flat, shapes):
    out, off = [], 0
    for s in shapes:
        n = math.prod(s)
        out.append(flat[off:off + n].reshape(s))
        off += n
    return out


def _small_pack(arrays):
    return _pack_flat([a.astype(_F32) for a in arrays], SUBLANES * LANES).reshape(-1, LANES)


def kernel(x, c, ada_w, ada_b, mix_norm_g, w_in, b_forget, conf_dw_w, conf_dw_b, conf_ln_g, conf_ln_b, sc_dw_w, w_out, ffn_norm_g, w_up, ffn_dw_w, ffn_dw_b, w_down, final_norm_g, loss_target, m_ada_w, m_ada_b, m_mix_norm_g, m_w_in, m_b_forget, m_conf_dw_w, m_conf_dw_b, m_conf_ln_g, m_conf_ln_b, m_sc_dw_w, m_w_out, m_ffn_norm_g, m_w_up, m_ffn_dw_w, m_ffn_dw_b, m_w_down, m_final_norm_g, v_ada_w, v_ada_b, v_mix_norm_g, v_w_in, v_b_forget, v_conf_dw_w, v_conf_dw_b, v_conf_ln_g, v_conf_ln_b, v_sc_dw_w, v_w_out, v_ffn_norm_g, v_w_up, v_ffn_dw_w, v_ffn_dw_b, v_w_down, v_final_norm_g):
    _, T, D = x.shape
    L = ada_w.shape[0]
    A = D // 2
    H = A // HEAD_DIM
    C = D // 4
    assert D - A - C == C
    IN = 3 * A + H + 5 * C
    NM = 3 * A + 5 * C
    NP = NM + LANES
    F2 = w_up.shape[2] * N_CHIP
    F = F2 // 2
    NA = ada_w.shape[2]
    assert NA * N_CHIP == 6 * D and w_in.shape[2] * N_CHIP == IN

    px, py, pc = _coords()
    chip = 2 * px + py
    me = 2 * chip + pc

    x0 = x[0]
    tgt = loss_target[0]

    c_all = allgather_small(c.reshape(-1, LANES), with_sum=False, name="gather_c").reshape(N_DEV, D)
    parts = [matmul(c_all, ada_w[l], out_dtype=_F32, name="ada_fwd", tm=N_DEV, tn=512, tk=D,
                    a_silu=True, precision=HIGHEST) for l in range(L)]
    parts = jnp.stack(parts)
    got = allgather_small(parts.reshape(-1, LANES), with_sum=False, name="gather_ada")
    got = got.reshape(N_DEV, L, N_DEV, NA)[0::2]
    mine = lax.dynamic_index_in_dim(got, me, axis=2, keepdims=False)
    ada = jnp.transpose(mine, (1, 0, 2)).reshape(L, 6 * D) + ada_b

    INs = IN // N_CHIP
    wp_l, wout_l, wup_l, wdown_l = [], [], [], []
    for l in range(L):
        wi4, wo, wu, wd = gather_weights(
            [w_in[l].astype(_MM), w_out[l].astype(_MM), w_up[l].astype(_MM), w_down[l].astype(_MM)],
            name="gather_weights")
        wi = jnp.transpose(wi4, (1, 0, 2)).reshape(D, IN)
        wp_l.append(jnp.concatenate(
            [wi[:, :3 * A], wi[:, 3 * A + H:], jnp.pad(wi[:, 3 * A:3 * A + H], ((0, 0), (0, LANES - H)))], axis=1))
        wout_l.append(wo)
        wup_l.append(wu)
        wdown_l.append(wd)

    small_w = _small_pack([conf_dw_w, sc_dw_w, ffn_dw_w])
    sw_all = allgather_small(small_w, with_sum=False, name="gather_small_w")[0::2].reshape(N_CHIP, -1)
    sw_shapes = [conf_dw_w.shape, sc_dw_w.shape, ffn_dw_w.shape]
    sw_parts = [_unpack_flat(sw_all[k], sw_shapes) for k in range(N_CHIP)]
    conf_w_full = jnp.concatenate([p[0] for p in sw_parts], axis=-1)
    sc_w_full = jnp.concatenate([p[1] for p in sw_parts], axis=-1)
    ffn_w_full = jnp.concatenate([p[2] for p in sw_parts], axis=-1)
    bf_pad = jnp.pad(b_forget, ((0, 0), (0, LANES - H)))

    row = lambda a: a.reshape(1, -1)

    saved = []
    x_cur, branch, gate = x0, None, None
    for l in range(L):
        sh_m, sc_m, g_m, sh_f, sc_f, g_f = [row(ada[l, k * D:(k + 1) * D]) for k in range(6)]
        a1 = row(mix_norm_g[l]) * (1.0 + sc_m)
        a2 = row(ffn_norm_g[l]) * (1.0 + sc_f)
        x_in, h1 = resid_norm_fwd(x_cur, a1, sh_m, branch, gate, name="norm_mix_fwd")
        wp = wp_l[l]
        proj = matmul(h1, wp, out_dtype=_MM, name="proj_fwd", tn=512, tk=D, b_cols=(0, NM))
        flog = matmul(h1, wp, out_dtype=_F32, name="fgate_logits", tn=LANES, tk=D, b_cols=(NM, LANES))
        bf = row(bf_pad[l])
        fcum = fgate_fwd(flog, bf, name="fgate_fwd")
        f_t = fcum[:, :H].T
        fq = jnp.broadcast_to(f_t[:, :, None], (H, T, LANES))
        fk = f_t[:, None, :]
        attn, lse = attn_fwd(proj, fq, fk, heads=H, name="attn_fwd")
        cw, cb = conf_w_full[l], row(conf_dw_b[l])
        lg, lb, sw = row(conf_ln_g[l]), row(conf_ln_b[l]), sc_w_full[l]
        cm, cc, zc = mixer_misc_fwd(proj, cw, cb, lg, lb, sw, width=C, base_col=3 * A, name="misc_fwd")
        cat = jnp.concatenate([attn, cm], axis=1)
        mixed = matmul(cat, wout_l[l], out_dtype=_F32, name="wout_fwd", tn=512, tk=D)
        x_mid, h2 = resid_norm_fwd(x_in, a2, sh_f, mixed, g_m, name="norm_ffn_fwd")
        up = matmul(h2, wup_l[l], out_dtype=_MM, name="wup_fwd", tn=512, tk=D)
        fw, fb = ffn_w_full[l], row(ffn_dw_b[l])
        act = ffn_act_fwd(up, fw, fb, name="ffn_act_fwd")
        dn = matmul(act, wdown_l[l], out_dtype=_F32, name="wdown_fwd", tk=512)
        saved.append(dict(x_in=x_in, h1=h1, proj=proj, flog=flog, fq=fq, fk=fk, attn=attn, lse=lse, cc=cc, zc=zc,
                          cat=cat, mixed=mixed, x_mid=x_mid, h2=h2, up=up, act=act, dn=dn, a1=a1, a2=a2,
                          g_m=g_m, g_f=g_f, sc_m=sc_m, sc_f=sc_f, bf=bf, cw=cw, lg=lg, lb=lb, sw=sw, fw=fw, fb=fb))
        x_cur, branch, gate = x_mid, dn, g_f

    dx, loss_row, d_final_g = final_loss_bwd(x_cur, branch, gate, row(final_norm_g), tgt, name="loss_bwd")

    KF = ffn_dw_w.shape[1]
    g_big = [None] * L
    d_ada, d_g1, d_g2, d_bf, d_cw, d_cb, d_lg, d_lb, d_sw, d_fw, d_fb = ([None] * L for _ in range(11))
    for l in reversed(range(L)):
        s = saved[l]
        ddn, dg_f = gate_bwd(dx, s["dn"], s["g_f"], name="gate_ffn_bwd")
        dact = matmul(ddn, wdown_l[l], out_dtype=_MM, name="wdown_dgrad", trans_b=True, tn=512, tk=D)
        gw_down = matmul(s["act"], ddn, out_dtype=_MM, name="wdown_wgrad", trans_a=True, tm=512)
        du, dwb = ffn_bwd_du(s["up"], dact, s["fw"], s["fb"], name="ffn_bwd_du")
        dup = dwconv_transpose(du, s["fw"], name="ffn_bwd_dup")
        dh2 = matmul(dup, wup_l[l], out_dtype=_MM, name="wup_dgrad", trans_b=True)
        gw_up = matmul(s["h2"], dup, out_dtype=_MM, name="wup_wgrad", trans_a=True)
        dx_mid, dsh_f, da2 = norm_bwd(s["x_mid"], dh2, dx, s["a2"], name="norm_ffn_bwd")
        dmixed, dg_m = gate_bwd(dx_mid, s["mixed"], s["g_m"], name="gate_mix_bwd")
        dcat = matmul(dmixed, wout_l[l], out_dtype=_MM, name="wout_dgrad", trans_b=True, tn=512, tk=D)
        gw_out = matmul(s["cat"], dmixed, out_dtype=_MM, name="wout_wgrad", trans_a=True)
        dk, dv, dfk = attn_bwd_dkv(s["proj"], dcat, s["attn"], s["fq"], s["fk"], s["lse"], heads=H,
                                   name="attn_bwd_dkv")
        dq, dfq = attn_bwd_dq(s["proj"], dcat, s["attn"], s["fq"], s["fk"], s["lse"], heads=H,
                              name="attn_bwd_dq")
        dmisc, d_cw[l], d_cb[l], d_lg[l], d_lb[l], d_sw[l] = mixer_misc_bwd(
            s["proj"], dcat, s["cc"], s["zc"], s["cw"], s["lg"], s["lb"], s["sw"],
            width=C, base_col=3 * A, dbase_col=A, name="misc_bwd")
        dfk_pad = jnp.pad((dfk[:, 0, :] + dfq[:, :, 0]).T, ((0, 0), (0, LANES - H)))
        dflog, dbf = fgate_bwd(dfk_pad, s["flog"], s["bf"], name="fgate_bwd")
        dproj = jnp.concatenate([dq, dk, dv, dmisc, dflog], axis=1)
        dh1 = matmul(dproj, wp_l[l], out_dtype=_MM, name="proj_dgrad", trans_b=True, tk=640)
        gwp = matmul(s["h1"], dproj, out_dtype=_MM, name="proj_wgrad", trans_a=True, tn=640)
        dx, dsh_m, da1 = norm_bwd(s["x_in"], dh1, dx_mid, s["a1"], name="norm_mix_bwd")

        g1, g2 = row(mix_norm_g[l]), row(ffn_norm_g[l])
        d_ada[l] = jnp.concatenate([dsh_m, da1 * g1, dg_m, dsh_f, da2 * g2, dg_f], axis=1)[0]
        d_g1[l] = (da1 * (1.0 + s["sc_m"]))[0]
        d_g2[l] = (da2 * (1.0 + s["sc_f"]))[0]
        d_bf[l] = dbf[0, :H]
        d_fw[l] = dwb[:KF]
        d_fb[l] = dwb[KF]
        gw_in = jnp.concatenate([gwp[:, :3 * A], gwp[:, NM:NM + H], gwp[:, 3 * A:NM]], axis=1)
        g_big[l] = (gw_in, gw_out, gw_up, gw_down)

    small = [loss_row[0], jnp.stack(d_g1), jnp.stack(d_bf), jnp.stack(d_cw), jnp.stack(d_cb)[:, 0],
             jnp.stack(d_lg)[:, 0], jnp.stack(d_lb)[:, 0], jnp.stack(d_sw), jnp.stack(d_g2), jnp.stack(d_fw),
             jnp.stack(d_fb), d_final_g[0], jnp.stack(d_ada)]
    small_shapes = [a.shape for a in small]
    sm_all, sm_sum = allgather_small(_small_pack(small), with_sum=True, name="reduce_small")
    (loss_v, g_mix_norm, g_bf, g_cw_full, g_cb, g_lg, g_lb, g_sw_full, g_ffn_norm, g_fw_full, g_fb, g_final,
     g_ada_b) = _unpack_flat(sm_sum.reshape(-1), small_shapes)
    loss = loss_v[0]
    n_ada = L * 6 * D
    off_ada = sum(math.prod(sh) for sh in small_shapes[:-1])
    d_ada_all = sm_all.reshape(N_DEV, -1)[:, off_ada:off_ada + n_ada].reshape(N_DEV, L, 6 * D)
    d_ada_chip = lax.dynamic_slice_in_dim(d_ada_all, chip * NA, NA, axis=2)
    d_ada_chip = jnp.transpose(d_ada_chip, (1, 0, 2))
    cshard = lambda a: lax.dynamic_slice_in_dim(a, chip * (a.shape[-1] // N_CHIP), a.shape[-1] // N_CHIP,
                                                axis=a.ndim - 1)
    g_conf_dw_w, g_sc_dw_w, g_ffn_dw_w = cshard(g_cw_full), cshard(g_sw_full), cshard(g_fw_full)

    g_in_l, g_out_l, g_up_l, g_down_l = [], [], [], []
    for l in range(L):
        gw_in, gw_out, gw_up, gw_down = g_big[l]
        gw_in4 = jnp.transpose(gw_in.reshape(D, N_CHIP, INs), (1, 0, 2))
        recv = scatter_grads([gw_in4, gw_out, gw_up, gw_down], name="scatter_grads")
        red = [sum_slots(r, name="sum_grads") for r in recv]
        gi, go, gu, gd = sibling_exchange(red, name="sibling_grads")
        g_in_l.append(gi.reshape(w_in.shape[1:])); g_out_l.append(go.reshape(w_out.shape[1:]))
        g_up_l.append(gu.reshape(w_up.shape[1:])); g_down_l.append(gd.reshape(w_down.shape[1:]))
    g_w_in, g_w_out, g_w_up, g_w_down = (jnp.stack(t) for t in (g_in_l, g_out_l, g_up_l, g_down_l))

    g_ada_w, dl_ada_w, nm_ada_w, nv_ada_w = ada_w_update(c_all.T, d_ada_chip, ada_w, m_ada_w, v_ada_w,
                                                          name="ada_w_update")

    grads = dict(ada_b=g_ada_b, mix_norm_g=g_mix_norm, w_in=g_w_in, b_forget=g_bf, conf_dw_w=g_conf_dw_w,
                 conf_dw_b=g_cb, conf_ln_g=g_lg, conf_ln_b=g_lb, sc_dw_w=g_sc_dw_w, w_out=g_w_out,
                 ffn_norm_g=g_ffn_norm, w_up=g_w_up, ffn_dw_w=g_ffn_dw_w, ffn_dw_b=g_fb, w_down=g_w_down,
                 final_norm_g=g_final)
    weights = dict(ada_b=(ada_b, m_ada_b, v_ada_b), mix_norm_g=(mix_norm_g, m_mix_norm_g, v_mix_norm_g),
                   w_in=(w_in, m_w_in, v_w_in), b_forget=(b_forget, m_b_forget, v_b_forget),
                   conf_dw_w=(conf_dw_w, m_conf_dw_w, v_conf_dw_w), conf_dw_b=(conf_dw_b, m_conf_dw_b, v_conf_dw_b),
                   conf_ln_g=(conf_ln_g, m_conf_ln_g, v_conf_ln_g), conf_ln_b=(conf_ln_b, m_conf_ln_b, v_conf_ln_b),
                   sc_dw_w=(sc_dw_w, m_sc_dw_w, v_sc_dw_w), w_out=(w_out, m_w_out, v_w_out),
                   ffn_norm_g=(ffn_norm_g, m_ffn_norm_g, v_ffn_norm_g), w_up=(w_up, m_w_up, v_w_up),
                   ffn_dw_w=(ffn_dw_w, m_ffn_dw_w, v_ffn_dw_w), ffn_dw_b=(ffn_dw_b, m_ffn_dw_b, v_ffn_dw_b),
                   w_down=(w_down, m_w_down, v_w_down), final_norm_g=(final_norm_g, m_final_norm_g, v_final_norm_g))
    order = ["ada_w", "ada_b", "mix_norm_g", "w_in", "b_forget", "conf_dw_w", "conf_dw_b", "conf_ln_g", "conf_ln_b",
             "sc_dw_w", "w_out", "ffn_norm_g", "w_up", "ffn_dw_w", "ffn_dw_b", "w_down", "final_norm_g"]
    g_out, d_out, m_out, v_out = {}, {}, {}, {}
    g_out["ada_w"], d_out["ada_w"], m_out["ada_w"], v_out["ada_w"] = g_ada_w, dl_ada_w, nm_ada_w, nv_ada_w
    for n in order[1:]:
        w, m, v = weights[n]
        g = grads[n].reshape(w.shape)
        g_out[n] = g
        d_out[n], m_out[n], v_out[n] = adamw(w, g, m, v, name="adamw_" + n)

    return (loss, dx[None], *[g_out[n] for n in order], *[d_out[n] for n in order],
            *[m_out[n] for n in order], *[v_out[n] for n in order])
```

```python
import functools
import math

import jax
import jax.numpy as jnp
from jax import lax
from jax.experimental import pallas as pl
from jax.experimental.pallas import tpu as pltpu

_MM = jnp.bfloat16
_F32 = jnp.float32
VMEM_LIMIT_V7X = 48 * 1024 * 1024
LANES = 128
SUBLANES = 8
BF16_ROWS = 16
HEAD_DIM = 128
RMS_EPS = 1e-6
LN_EPS = 1e-5
NEG = -1e30
HIGHEST = lax.Precision.HIGHEST

ADAM_LR = 0.001
ADAM_B1 = 0.9
ADAM_B2 = 0.999
ADAM_EPS = 1e-08
ADAM_WD = 0.01
ADAM_STEP = 10

N_DEV = 8
N_CHIP = 4

ATT_BLOCK = 512
CONV_TILE = 512
CONV_HALO = 32
FFN_HALO = BF16_ROWS
NORM_TILE = 256
MM_TM = 1024
MM_TN = 1024
MM_TK = 1024

MESH = pl.DeviceIdType.MESH


def _tile(dim, pref, mult=LANES):
    t = (min(pref, dim) // mult) * mult
    while t >= mult:
        if dim % t == 0:
            return t
        t -= mult
    return dim


def _cparams(sem):
    return pltpu.CompilerParams(dimension_semantics=sem, vmem_limit_bytes=VMEM_LIMIT_V7X)


def _sigmoid(x):
    return 1.0 / (1.0 + jnp.exp(-x))


def _colsum(x):
    return jnp.sum(x, axis=0, keepdims=True)


def matmul(a, b, *, out_dtype, name, trans_a=False, trans_b=False, tm=MM_TM, tn=MM_TN, tk=MM_TK,
           a_silu=False, precision=None, b_cols=None):
    M, K = (a.shape[1], a.shape[0]) if trans_a else a.shape
    N = b.shape[0] if trans_b else b.shape[1]
    assert (b.shape[1] if trans_b else b.shape[0]) == K
    col0 = 0
    if b_cols is not None:
        assert not trans_b
        col0, N = b_cols
    tm = _tile(M, tm, SUBLANES if (M % LANES) else LANES)
    tn = _tile(math.gcd(N, col0) if col0 else N, tn)
    tk = _tile(K, tk)
    nk = K // tk
    jb = col0 // tn
    dims = (((0 if trans_a else 1,), (1 if trans_b else 0,)), ((), ()))

    def body(a_ref, b_ref, o_ref, *scratch):
        av = a_ref[...]
        if a_silu:
            av = av * _sigmoid(av)
        part = lax.dot_general(av, b_ref[...], dims, preferred_element_type=_F32, precision=precision)
        if nk == 1:
            o_ref[...] = part.astype(o_ref.dtype)
        else:
            acc_ref, = scratch
            k = pl.program_id(2)

            @pl.when(k == 0)
            def _():
                acc_ref[...] = part

            @pl.when(k > 0)
            def _():
                acc_ref[...] += part

            @pl.when(k == nk - 1)
            def _():
                o_ref[...] = acc_ref[...].astype(o_ref.dtype)

    a_spec = (pl.BlockSpec((tk, tm), lambda i, j, k: (k, i)) if trans_a
              else pl.BlockSpec((tm, tk), lambda i, j, k: (i, k)))
    b_spec = (pl.BlockSpec((tn, tk), lambda i, j, k: (j, k)) if trans_b
              else pl.BlockSpec((tk, tn), lambda i, j, k: (k, j + jb)))
    return pl.pallas_call(
        body, name=name, grid=(M // tm, N // tn, nk),
        in_specs=[a_spec, b_spec],
        out_specs=pl.BlockSpec((tm, tn), lambda i, j, k: (i, j)),
        out_shape=jax.ShapeDtypeStruct((M, N), out_dtype),
        scratch_shapes=[pltpu.VMEM((tm, tn), _F32)] if nk > 1 else [],
        compiler_params=_cparams(("parallel", "parallel", "arbitrary")),
    )(a, b)


def resid_norm_fwd(x, a, sh, branch=None, gate=None, *, name):
    T, D = x.shape
    tm = _tile(T, NORM_TILE, BF16_ROWS)
    has_res = branch is not None

    def body(*refs):
        if has_res:
            x_ref, br_ref, g_ref, a_ref, sh_ref, xo_ref, h_ref = refs
            xv = x_ref[...] + g_ref[...] * br_ref[...]
            xo_ref[...] = xv
        else:
            x_ref, a_ref, sh_ref, h_ref = refs
            xv = x_ref[...]
        r = lax.rsqrt(jnp.mean(xv * xv, axis=-1, keepdims=True) + RMS_EPS)
        h_ref[...] = (xv * r * a_ref[...] + sh_ref[...]).astype(h_ref.dtype)

    row = pl.BlockSpec((tm, D), lambda i: (i, 0))
    vec = pl.BlockSpec((1, D), lambda i: (0, 0))
    if has_res:
        xo, h = pl.pallas_call(
            body, name=name, grid=(T // tm,), in_specs=[row, row, vec, vec, vec], out_specs=[row, row],
            out_shape=[jax.ShapeDtypeStruct((T, D), _F32), jax.ShapeDtypeStruct((T, D), _MM)],
            compiler_params=_cparams(("parallel",)))(x, branch, gate, a, sh)
        return xo, h
    h = pl.pallas_call(
        body, name=name, grid=(T // tm,), in_specs=[row, vec, vec], out_specs=row,
        out_shape=jax.ShapeDtypeStruct((T, D), _MM),
        compiler_params=_cparams(("parallel",)))(x, a, sh)
    return x, h


def norm_bwd(x, dh, dx_in, a, *, name):
    T, D = x.shape
    tm = _tile(T, NORM_TILE, BF16_ROWS)

    def body(x_ref, dh_ref, dxi_ref, a_ref, dxo_ref, dsh_ref, da_ref):
        i = pl.program_id(0)
        xv = x_ref[...]
        r = lax.rsqrt(jnp.mean(xv * xv, axis=-1, keepdims=True) + RMS_EPS)
        n = xv * r
        dhv = dh_ref[...].astype(_F32)
        dn = dhv * a_ref[...]
        dxo_ref[...] = dxi_ref[...] + r * (dn - n * jnp.mean(dn * n, axis=-1, keepdims=True))

        @pl.when(i == 0)
        def _():
            dsh_ref[...] = jnp.zeros_like(dsh_ref)
            da_ref[...] = jnp.zeros_like(da_ref)

        dsh_ref[...] += _colsum(dhv)
        da_ref[...] += _colsum(dhv * n)

    row = pl.BlockSpec((tm, D), lambda i: (i, 0))
    vec = pl.BlockSpec((1, D), lambda i: (0, 0))
    return pl.pallas_call(
        body, name=name, grid=(T // tm,), in_specs=[row, row, row, vec], out_specs=[row, vec, vec],
        out_shape=[jax.ShapeDtypeStruct((T, D), _F32), jax.ShapeDtypeStruct((1, D), _F32),
                   jax.ShapeDtypeStruct((1, D), _F32)],
        compiler_params=_cparams(("arbitrary",)))(x, dh, dx_in, a)


def gate_bwd(dx, branch, gate, *, name):
    T, D = dx.shape
    tm = _tile(T, NORM_TILE, BF16_ROWS)

    def body(dx_ref, br_ref, g_ref, db_ref, dg_ref):
        i = pl.program_id(0)
        dxv = dx_ref[...]
        db_ref[...] = (dxv * g_ref[...]).astype(db_ref.dtype)

        @pl.when(i == 0)
        def _():
            dg_ref[...] = jnp.zeros_like(dg_ref)

        dg_ref[...] += _colsum(dxv * br_ref[...])

    row = pl.BlockSpec((tm, D), lambda i: (i, 0))
    vec = pl.BlockSpec((1, D), lambda i: (0, 0))
    return pl.pallas_call(
        body, name=name, grid=(T // tm,), in_specs=[row, row, vec], out_specs=[row, vec],
        out_shape=[jax.ShapeDtypeStruct((T, D), _MM), jax.ShapeDtypeStruct((1, D), _F32)],
        compiler_params=_cparams(("arbitrary",)))(dx, branch, gate)


def final_loss_bwd(x, branch, gate, gfin, tgt, *, name):
    T, D = x.shape
    tm = _tile(T, NORM_TILE, BF16_ROWS)

    def body(x_ref, br_ref, g_ref, gf_ref, t_ref, dx_ref, loss_ref, dgf_ref):
        i = pl.program_id(0)
        xv = x_ref[...] + g_ref[...] * br_ref[...]
        r = lax.rsqrt(jnp.mean(xv * xv, axis=-1, keepdims=True) + RMS_EPS)
        n = xv * r
        e = n * gf_ref[...] - t_ref[...]
        dy = e * (1.0 / D)
        dn = dy * gf_ref[...]
        dx_ref[...] = r * (dn - n * jnp.mean(dn * n, axis=-1, keepdims=True))

        @pl.when(i == 0)
        def _():
            loss_ref[...] = jnp.zeros_like(loss_ref)
            dgf_ref[...] = jnp.zeros_like(dgf_ref)

        per_row = jnp.mean(e * e, axis=-1, keepdims=True)
        loss_ref[...] += jnp.broadcast_to(0.5 * _colsum(per_row), loss_ref.shape)
        dgf_ref[...] += _colsum(dy * n)

    row = pl.BlockSpec((tm, D), lambda i: (i, 0))
    vec = pl.BlockSpec((1, D), lambda i: (0, 0))
    lvec = pl.BlockSpec((1, LANES), lambda i: (0, 0))
    return pl.pallas_call(
        body, name=name, grid=(T // tm,), in_specs=[row, row, vec, vec, row], out_specs=[row, lvec, vec],
        out_shape=[jax.ShapeDtypeStruct((T, D), _F32), jax.ShapeDtypeStruct((1, LANES), _F32),
                   jax.ShapeDtypeStruct((1, D), _F32)],
        compiler_params=_cparams(("arbitrary",)))(x, branch, gate, gfin, tgt)


def _log_sigmoid(x):
    return jnp.minimum(x, 0.0) - jnp.log(1.0 + jnp.exp(-jnp.abs(x)))


def fgate_fwd(flog, bf, *, name):
    T = flog.shape[0]
    tt = _tile(T, 256)

    def body(x_ref, b_ref, f_ref, carry):
        i = pl.program_id(0)

        @pl.when(i == 0)
        def _():
            carry[...] = jnp.zeros_like(carry)

        lf = _log_sigmoid(x_ref[...] + b_ref[...])
        rows = lax.broadcasted_iota(jnp.int32, (tt, tt), 0)
        cols = lax.broadcasted_iota(jnp.int32, (tt, tt), 1)
        tri = (cols <= rows).astype(_F32)
        f_ref[...] = jnp.dot(tri, lf, preferred_element_type=_F32, precision=HIGHEST) + carry[0:1, :]
        carry[0:1, :] = f_ref[tt - 1:tt, :]

    return pl.pallas_call(
        body, name=name, grid=(T // tt,),
        in_specs=[pl.BlockSpec((tt, LANES), lambda i: (i, 0)), pl.BlockSpec((1, LANES), lambda i: (0, 0))],
        out_specs=pl.BlockSpec((tt, LANES), lambda i: (i, 0)),
        out_shape=jax.ShapeDtypeStruct((T, LANES), _F32),
        scratch_shapes=[pltpu.VMEM((SUBLANES, LANES), _F32)],
        compiler_params=_cparams(("arbitrary",)))(flog, bf)


def fgate_bwd(dfk, flog, bf, *, name):
    T = flog.shape[0]
    tt = _tile(T, 256)
    nb = T // tt

    def body(d_ref, x_ref, b_ref, o_ref, db_ref, carry):
        i = pl.program_id(0)

        @pl.when(i == 0)
        def _():
            carry[...] = jnp.zeros_like(carry)
            db_ref[...] = jnp.zeros_like(db_ref)

        rows = lax.broadcasted_iota(jnp.int32, (tt, tt), 0)
        cols = lax.broadcasted_iota(jnp.int32, (tt, tt), 1)
        upper = (cols >= rows).astype(_F32)
        dlf = jnp.dot(upper, d_ref[...], preferred_element_type=_F32, precision=HIGHEST) + carry[0:1, :]
        carry[0:1, :] = dlf[0:1, :]
        dfl = dlf * _sigmoid(-(x_ref[...] + b_ref[...]))
        o_ref[...] = dfl.astype(o_ref.dtype)
        db_ref[...] += _colsum(dfl)

    rev = pl.BlockSpec((tt, LANES), lambda i: (nb - 1 - i, 0))
    vec = pl.BlockSpec((1, LANES), lambda i: (0, 0))
    return pl.pallas_call(
        body, name=name, grid=(nb,), in_specs=[rev, rev, vec], out_specs=[rev, vec],
        out_shape=[jax.ShapeDtypeStruct((T, LANES), _MM), jax.ShapeDtypeStruct((1, LANES), _F32)],
        scratch_shapes=[pltpu.VMEM((SUBLANES, LANES), _F32)],
        compiler_params=_cparams(("arbitrary",)))(dfk, flog, bf)


def _att_scores(q, k, fq, fk, rep, masked):
    s = lax.dot_general(q, k, (((1,), (1,)), ((), ())), preferred_element_type=_F32)
    s = s * (HEAD_DIM ** -0.5) + (jnp.tile(fq, (1, rep)) - fk)
    if masked:
        rows = lax.broadcasted_iota(jnp.int32, s.shape, 0)
        cols = lax.broadcasted_iota(jnp.int32, s.shape, 1)
        s = jnp.where(cols <= rows, s, NEG)
    return s


def _fold_q(r, t, nb):
    first = t <= r
    return jnp.where(first, r, nb - 1 - r), jnp.where(first, t, t - r - 1)


def _fold_k(r, t, nb):
    first = t < nb - r
    return jnp.where(first, r, nb - 1 - r), jnp.where(first, r + t, t - 1)


def attn_fwd(proj, fq, fk, *, heads, name):
    T = proj.shape[0]
    H = heads
    tb = _tile(T, ATT_BLOCK)
    nb = T // tb
    assert nb % 2 == 0
    rep = tb // LANES

    def body(q_ref, k_ref, v_ref, fq_ref, fk_ref, o_ref, lse_ref, m_s, l_s, acc_s):
        i, j = _fold_q(pl.program_id(1), pl.program_id(2), nb)

        @pl.when(j == 0)
        def _():
            m_s[...] = jnp.full_like(m_s, NEG)
            l_s[...] = jnp.zeros_like(l_s)
            acc_s[...] = jnp.zeros_like(acc_s)

        def step(masked):
            s = _att_scores(q_ref[...], k_ref[...], fq_ref[...], fk_ref[...], rep, masked)
            m_prev = m_s[...]
            m_new = jnp.maximum(m_prev, jnp.max(s, axis=-1, keepdims=True))
            alpha = jnp.exp(m_prev - m_new)
            p = jnp.exp(s - jnp.tile(m_new, (1, rep)))
            l_s[...] = alpha * l_s[...] + jnp.sum(p, axis=-1, keepdims=True)
            v = v_ref[...]
            acc_s[...] = alpha * acc_s[...] + jnp.dot(p.astype(v.dtype), v, preferred_element_type=_F32)
            m_s[...] = m_new

        @pl.when(j < i)
        def _():
            step(False)

        @pl.when(j == i)
        def _():
            step(True)
            o_ref[...] = (acc_s[...] / l_s[...]).astype(o_ref.dtype)
            lse_ref[...] = m_s[...] + jnp.log(l_s[...])

    qi = lambda r, t: _fold_q(r, t, nb)[0]
    kj = lambda r, t: _fold_q(r, t, nb)[1]
    qs = pl.BlockSpec((tb, HEAD_DIM), lambda h, r, t: (qi(r, t), h))
    ks = pl.BlockSpec((tb, HEAD_DIM), lambda h, r, t: (kj(r, t), H + h))
    vs = pl.BlockSpec((tb, HEAD_DIM), lambda h, r, t: (kj(r, t), 2 * H + h))
    fqs = pl.BlockSpec((None, tb, LANES), lambda h, r, t: (h, qi(r, t), 0))
    fks = pl.BlockSpec((None, 1, tb), lambda h, r, t: (h, 0, kj(r, t)))
    return pl.pallas_call(
        body, name=name, grid=(H, nb // 2, nb + 1),
        in_specs=[qs, ks, vs, fqs, fks],
        out_specs=[qs, fqs],
        out_shape=[jax.ShapeDtypeStruct((T, H * HEAD_DIM), _MM), jax.ShapeDtypeStruct((H, T, LANES), _F32)],
        scratch_shapes=[pltpu.VMEM((tb, LANES), _F32), pltpu.VMEM((tb, LANES), _F32),
                        pltpu.VMEM((tb, HEAD_DIM), _F32)],
        compiler_params=_cparams(("parallel", "parallel", "arbitrary")))(proj, proj, proj, fq, fk)


def _att_p_ds(q, k, v, do, o, fq, fk, lse, rep, masked):
    s = _att_scores(q, k, fq, fk, rep, masked)
    p = jnp.exp(s - jnp.tile(lse, (1, rep)))
    delta = jnp.sum(do.astype(_F32) * o.astype(_F32), axis=-1, keepdims=True)
    dp = lax.dot_general(do, v, (((1,), (1,)), ((), ())), preferred_element_type=_F32)
    ds = p * (dp - delta)
    return p, ds


def attn_bwd_dkv(proj, dcat, attn, fq, fk, lse, *, heads, name):
    T = proj.shape[0]
    H = heads
    tb = _tile(T, ATT_BLOCK)
    nb = T // tb
    assert nb % 2 == 0
    rep = tb // LANES
    scale = HEAD_DIM ** -0.5

    def body(q_ref, k_ref, v_ref, do_ref, o_ref, fq_ref, fk_ref, lse_ref, dk_ref, dv_ref, dfk_ref,
             dk_s, dv_s, dfk_s):
        j, i = _fold_k(pl.program_id(1), pl.program_id(2), nb)

        @pl.when(i == j)
        def _():
            dk_s[...] = jnp.zeros_like(dk_s)
            dv_s[...] = jnp.zeros_like(dv_s)
            dfk_s[...] = jnp.zeros_like(dfk_s)

        def step(masked):
            q = q_ref[...]
            do = do_ref[...]
            p, ds = _att_p_ds(q, k_ref[...], v_ref[...], do, o_ref[...], fq_ref[...], fk_ref[...],
                              lse_ref[...], rep, masked)
            tn = (((0,), (0,)), ((), ()))
            dv_s[...] += lax.dot_general(p.astype(do.dtype), do, tn, preferred_element_type=_F32)
            dk_s[...] += lax.dot_general(ds.astype(q.dtype), q, tn, preferred_element_type=_F32)
            dfk_s[0:1, :] += -_colsum(ds)

        @pl.when(i > j)
        def _():
            step(False)

        @pl.when(i == j)
        def _():
            step(True)

        @pl.when(i == nb - 1)
        def _():
            dk_ref[...] = (dk_s[...] * scale).astype(dk_ref.dtype)
            dv_ref[...] = dv_s[...].astype(dv_ref.dtype)
            dfk_ref[...] = dfk_s[0:1, :]

    kj = lambda r, t: _fold_k(r, t, nb)[0]
    qi = lambda r, t: _fold_k(r, t, nb)[1]
    qs = pl.BlockSpec((tb, HEAD_DIM), lambda h, r, t: (qi(r, t), h))
    ks = pl.BlockSpec((tb, HEAD_DIM), lambda h, r, t: (kj(r, t), H + h))
    vs = pl.BlockSpec((tb, HEAD_DIM), lambda h, r, t: (kj(r, t), 2 * H + h))
    stat = pl.BlockSpec((None, tb, LANES), lambda h, r, t: (h, qi(r, t), 0))
    fks = pl.BlockSpec((None, 1, tb), lambda h, r, t: (h, 0, kj(r, t)))
    kout = pl.BlockSpec((tb, HEAD_DIM), lambda h, r, t: (kj(r, t), h))
    A = H * HEAD_DIM
    return pl.pallas_call(
        body, name=name, grid=(H, nb // 2, nb + 1),
        in_specs=[qs, ks, vs, qs, qs, stat, fks, stat],
        out_specs=[kout, kout, fks],
        out_shape=[jax.ShapeDtypeStruct((T, A), _MM), jax.ShapeDtypeStruct((T, A), _MM),
                   jax.ShapeDtypeStruct((H, 1, T), _F32)],
        scratch_shapes=[pltpu.VMEM((tb, HEAD_DIM), _F32), pltpu.VMEM((tb, HEAD_DIM), _F32),
                        pltpu.VMEM((SUBLANES, tb), _F32)],
        compiler_params=_cparams(("parallel", "parallel", "arbitrary")))(
            proj, proj, proj, dcat, attn, fq, fk, lse)


def attn_bwd_dq(proj, dcat, attn, fq, fk, lse, *, heads, name):
    T = proj.shape[0]
    H = heads
    tb = _tile(T, ATT_BLOCK)
    nb = T // tb
    assert nb % 2 == 0
    rep = tb // LANES
    scale = HEAD_DIM ** -0.5

    def body(q_ref, k_ref, v_ref, do_ref, o_ref, fq_ref, fk_ref, lse_ref, dq_ref, dfq_ref, dq_s, dfq_s):
        i, j = _fold_q(pl.program_id(1), pl.program_id(2), nb)

        @pl.when(j == 0)
        def _():
            dq_s[...] = jnp.zeros_like(dq_s)
            dfq_s[...] = jnp.zeros_like(dfq_s)

        def step(masked):
            k = k_ref[...]
            _, ds = _att_p_ds(q_ref[...], k, v_ref[...], do_ref[...], o_ref[...], fq_ref[...], fk_ref[...],
                              lse_ref[...], rep, masked)
            dq_s[...] += jnp.dot(ds.astype(k.dtype), k, preferred_element_type=_F32)
            dfq_s[...] += jnp.sum(ds, axis=-1, keepdims=True)

        @pl.when(j < i)
        def _():
            step(False)

        @pl.when(j == i)
        def _():
            step(True)
            dq_ref[...] = (dq_s[...] * scale).astype(dq_ref.dtype)
            dfq_ref[...] = dfq_s[...]

    qi = lambda r, t: _fold_q(r, t, nb)[0]
    kj = lambda r, t: _fold_q(r, t, nb)[1]
    qs = pl.BlockSpec((tb, HEAD_DIM), lambda h, r, t: (qi(r, t), h))
    ks = pl.BlockSpec((tb, HEAD_DIM), lambda h, r, t: (kj(r, t), H + h))
    vs = pl.BlockSpec((tb, HEAD_DIM), lambda h, r, t: (kj(r, t), 2 * H + h))
    stat = pl.BlockSpec((None, tb, LANES), lambda h, r, t: (h, qi(r, t), 0))
    fks = pl.BlockSpec((None, 1, tb), lambda h, r, t: (h, 0, kj(r, t)))
    return pl.pallas_call(
        body, name=name, grid=(H, nb // 2, nb + 1),
        in_specs=[qs, ks, vs, qs, qs, stat, fks, stat],
        out_specs=[qs, stat],
        out_shape=[jax.ShapeDtypeStruct((T, H * HEAD_DIM), _MM), jax.ShapeDtypeStruct((H, T, LANES), _F32)],
        scratch_shapes=[pltpu.VMEM((tb, HEAD_DIM), _F32), pltpu.VMEM((tb, LANES), _F32)],
        compiler_params=_cparams(("parallel", "parallel", "arbitrary")))(
            proj, proj, proj, dcat, attn, fq, fk, lse)


def _causal_taps(w_ref, scr, halo, tt, width):
    acc = w_ref[width - 1:width, :] * scr[halo:halo + tt, :]
    for j in range(1, width):
        acc = acc + w_ref[width - 1 - j:width - j, :] * scr[halo - j:halo - j + tt, :]
    return acc


def _anticausal_taps(w_ref, scr, tt, width):
    acc = w_ref[width - 1:width, :] * scr[0:tt, :]
    for j in range(1, width):
        acc = acc + w_ref[width - 1 - j:width - j, :] * scr[j:j + tt, :]
    return acc


def _ln_fwd(cc, g, b):
    mu = jnp.mean(cc, axis=-1, keepdims=True)
    xc = cc - mu
    rstd = lax.rsqrt(jnp.mean(xc * xc, axis=-1, keepdims=True) + LN_EPS)
    xhat = xc * rstd
    return xhat, rstd, xhat * g + b


def _ln_silu_bwd(cc, dconf, g, b):
    xhat, rstd, ln = _ln_fwd(cc, g, b)
    s = _sigmoid(ln)
    dln = dconf * (s * (1.0 + ln * (1.0 - s)))
    dxh = dln * g
    dcc = rstd * (dxh - jnp.mean(dxh, axis=-1, keepdims=True)
                  - xhat * jnp.mean(dxh * xhat, axis=-1, keepdims=True))
    return dcc, dln, xhat


def mixer_misc_fwd(proj, cw, cb, lg, lb, sw, *, width, base_col, name):
    T = proj.shape[0]
    C = width
    tt = _tile(T, CONV_TILE)
    HB = CONV_HALO
    per = tt // HB
    KC, KS = cw.shape[0], sw.shape[0]
    b0 = base_col // C

    def body(cv, cg, sx, sb, sc, cvh, cgh, sxh, sch, cw_ref, cb_ref, lg_ref, lb_ref, sw_ref,
             cm_ref, cc_ref, zc_ref, gscr, zscr):
        keep = (pl.program_id(0) > 0).astype(_F32)
        f = lambda r: r[...].astype(_F32)
        gscr[0:HB, :] = f(cvh) * _sigmoid(f(cgh)) * keep
        gscr[HB:HB + tt, :] = f(cv) * _sigmoid(f(cg))
        cc = _causal_taps(cw_ref, gscr, HB, tt, KC) + cb_ref[...]
        cc_ref[...] = cc
        _, _, ln = _ln_fwd(cc, lg_ref[...], lb_ref[...])
        cm_ref[:, 0:C] = (ln * _sigmoid(ln)).astype(cm_ref.dtype)
        zscr[0:HB, :] = f(sch) * f(sxh) * keep
        zscr[HB:HB + tt, :] = f(sc) * f(sx)
        zc = _causal_taps(sw_ref, zscr, HB, tt, KS)
        zc_ref[...] = zc
        cm_ref[:, C:2 * C] = (f(sb) * zc).astype(cm_ref.dtype)

    main = lambda k: pl.BlockSpec((tt, C), lambda i: (i, b0 + k))
    halo = lambda k: pl.BlockSpec((HB, C), lambda i: (jnp.maximum(i * per - 1, 0), b0 + k))
    full = lambda a: pl.BlockSpec(a.shape, lambda i: (0, 0))
    return pl.pallas_call(
        body, name=name, grid=(T // tt,),
        in_specs=[main(0), main(1), main(2), main(3), main(4), halo(0), halo(1), halo(2), halo(4),
                  full(cw), full(cb), full(lg), full(lb), full(sw)],
        out_specs=[pl.BlockSpec((tt, 2 * C), lambda i: (i, 0)), pl.BlockSpec((tt, C), lambda i: (i, 0)),
                   pl.BlockSpec((tt, C), lambda i: (i, 0))],
        out_shape=[jax.ShapeDtypeStruct((T, 2 * C), _MM), jax.ShapeDtypeStruct((T, C), _F32),
                   jax.ShapeDtypeStruct((T, C), _F32)],
        scratch_shapes=[pltpu.VMEM((tt + HB, C), _F32), pltpu.VMEM((tt + HB, C), _F32)],
        compiler_params=_cparams(("parallel",)))(
            proj, proj, proj, proj, proj, proj, proj, proj, proj, cw, cb, lg, lb, sw)


def mixer_misc_bwd(proj, dcat, cc, zc, cw, lg, lb, sw, *, width, base_col, dbase_col, name):
    T = proj.shape[0]
    C = width
    tt = _tile(T, CONV_TILE)
    nt = T // tt
    HB = CONV_HALO
    per = tt // HB
    KC, KS = cw.shape[0], sw.shape[0]
    b0 = base_col // C
    d0 = dbase_col // C
    last_hb = T // HB - 1

    def body(cv, cg, sx, sb, sc, cvh, cgh, sxh, sch, sbn, dcf, dsv, dcfn, dsvn, cc_ref, ccn_ref, zc_ref,
             cw_ref, lg_ref, lb_ref, sw_ref,
             dm_ref, dcw_ref, dcb_ref, dlg_ref, dlb_ref, dsw_ref, gscr, dscr, zscr, zdscr):
        i = pl.program_id(0)
        keep = (i > 0).astype(_F32)
        ahead = (i < nt - 1).astype(_F32)
        f = lambda r: r[...].astype(_F32)

        @pl.when(i == 0)
        def _():
            for r in (dcw_ref, dcb_ref, dlg_ref, dlb_ref, dsw_ref):
                r[...] = jnp.zeros_like(r)

        g, b = lg_ref[...], lb_ref[...]
        dcc, dln, xhat = _ln_silu_bwd(cc_ref[...], f(dcf), g, b)
        dcc_next, _, _ = _ln_silu_bwd(ccn_ref[...], f(dcfn), g, b)
        dlg_ref[...] += _colsum(dln * xhat)
        dlb_ref[...] += _colsum(dln)
        dcb_ref[...] += _colsum(dcc)
        dscr[0:tt, :] = dcc
        dscr[tt:tt + HB, :] = dcc_next * ahead
        dglu = _anticausal_taps(cw_ref, dscr, tt, KC)
        cvv = f(cv)
        sig = _sigmoid(f(cg))
        dm_ref[:, 0:C] = (dglu * sig).astype(dm_ref.dtype)
        dm_ref[:, C:2 * C] = (dglu * cvv * sig * (1.0 - sig)).astype(dm_ref.dtype)
        gscr[0:HB, :] = f(cvh) * _sigmoid(f(cgh)) * keep
        gscr[HB:HB + tt, :] = cvv * sig
        for j in range(KC):
            dcw_ref[KC - 1 - j:KC - j, :] += _colsum(dcc * gscr[HB - j:HB - j + tt, :])
        dsc_out = f(dsv)
        sbv = f(sb)
        dzc = dsc_out * sbv
        zdscr[0:tt, :] = dzc
        zdscr[tt:tt + HB, :] = f(dsvn) * f(sbn) * ahead
        dz = _anticausal_taps(sw_ref, zdscr, tt, KS)
        sxv, scv = f(sx), f(sc)
        dm_ref[:, 2 * C:3 * C] = (dz * scv).astype(dm_ref.dtype)
        dm_ref[:, 3 * C:4 * C] = (dsc_out * zc_ref[...]).astype(dm_ref.dtype)
        dm_ref[:, 4 * C:5 * C] = (dz * sxv).astype(dm_ref.dtype)
        zscr[0:HB, :] = f(sch) * f(sxh) * keep
        zscr[HB:HB + tt, :] = scv * sxv
        for j in range(KS):
            dsw_ref[KS - 1 - j:KS - j, :] += _colsum(dzc * zscr[HB - j:HB - j + tt, :])

    main = lambda col: pl.BlockSpec((tt, C), lambda i: (i, col))
    prev = lambda col: pl.BlockSpec((HB, C), lambda i: (jnp.maximum(i * per - 1, 0), col))
    nxt = lambda col: pl.BlockSpec((HB, C), lambda i: (jnp.minimum((i + 1) * per, last_hb), col))
    full = lambda a: pl.BlockSpec(a.shape, lambda i: (0, 0))
    vec = pl.BlockSpec((1, C), lambda i: (0, 0))
    return pl.pallas_call(
        body, name=name, grid=(nt,),
        in_specs=[main(b0), main(b0 + 1), main(b0 + 2), main(b0 + 3), main(b0 + 4),
                  prev(b0), prev(b0 + 1), prev(b0 + 2), prev(b0 + 4), nxt(b0 + 3),
                  main(d0), main(d0 + 1), nxt(d0), nxt(d0 + 1),
                  main(0), nxt(0), main(0),
                  full(cw), full(lg), full(lb), full(sw)],
        out_specs=[pl.BlockSpec((tt, 5 * C), lambda i: (i, 0)), full(cw), vec, vec, vec, full(sw)],
        out_shape=[jax.ShapeDtypeStruct((T, 5 * C), _MM), jax.ShapeDtypeStruct(cw.shape, _F32),
                   jax.ShapeDtypeStruct((1, C), _F32), jax.ShapeDtypeStruct((1, C), _F32),
                   jax.ShapeDtypeStruct((1, C), _F32), jax.ShapeDtypeStruct(sw.shape, _F32)],
        scratch_shapes=[pltpu.VMEM((tt + HB, C), _F32)] * 4,
        compiler_params=_cparams(("arbitrary",)))(
            proj, proj, proj, proj, proj, proj, proj, proj, proj, proj,
            dcat, dcat, dcat, dcat, cc, cc, zc, cw, lg, lb, sw)


def _ffn_u(main_ref, halo_ref, w_ref, b_ref, scr, keep, tt, width):
    HB = FFN_HALO
    scr[0:HB, :] = halo_ref[...].astype(_F32) * keep
    scr[HB:HB + tt, :] = main_ref[...].astype(_F32)
    return _causal_taps(w_ref, scr, HB, tt, width) + b_ref[...]


def ffn_act_fwd(up, w, b, *, name):
    T, F2 = up.shape
    F = F2 // 2
    K = w.shape[0]
    tt = _tile(T, CONV_TILE)
    tc = _tile(F, 512)
    nb = F // tc
    per = tt // FFN_HALO

    def body(g_ref, v_ref, gh_ref, vh_ref, wg_ref, wv_ref, bg_ref, bv_ref, o_ref, gscr, vscr):
        keep = (pl.program_id(0) > 0).astype(_F32)
        ug = _ffn_u(g_ref, gh_ref, wg_ref, bg_ref, gscr, keep, tt, K)
        uv = _ffn_u(v_ref, vh_ref, wv_ref, bv_ref, vscr, keep, tt, K)
        o_ref[...] = (ug * _sigmoid(ug) * uv).astype(o_ref.dtype)

    main = lambda off: pl.BlockSpec((tt, tc), lambda i, j: (i, j + off))
    halo = lambda off: pl.BlockSpec((FFN_HALO, tc), lambda i, j: (jnp.maximum(i * per - 1, 0), j + off))
    wsp = lambda off: pl.BlockSpec((K, tc), lambda i, j: (0, j + off))
    bsp = lambda off: pl.BlockSpec((1, tc), lambda i, j: (0, j + off))
    return pl.pallas_call(
        body, name=name, grid=(T // tt, nb),
        in_specs=[main(0), main(nb), halo(0), halo(nb), wsp(0), wsp(nb), bsp(0), bsp(nb)],
        out_specs=pl.BlockSpec((tt, tc), lambda i, j: (i, j)),
        out_shape=jax.ShapeDtypeStruct((T, F), _MM),
        scratch_shapes=[pltpu.VMEM((tt + FFN_HALO, tc), _F32)] * 2,
        compiler_params=_cparams(("parallel", "parallel")))(up, up, up, up, w, w, b, b)


def ffn_bwd_du(up, dact, w, b, *, name):
    T, F2 = up.shape
    F = F2 // 2
    K = w.shape[0]
    tt = _tile(T, CONV_TILE)
    tc = _tile(F, 512)
    nb = F // tc
    per = tt // FFN_HALO
    HB = FFN_HALO

    def body(g_ref, v_ref, gh_ref, vh_ref, da_ref, wg_ref, wv_ref, bg_ref, bv_ref, du_ref, dwb_ref,
             gscr, vscr):
        i = pl.program_id(1)
        keep = (i > 0).astype(_F32)
        ug = _ffn_u(g_ref, gh_ref, wg_ref, bg_ref, gscr, keep, tt, K)
        uv = _ffn_u(v_ref, vh_ref, wv_ref, bv_ref, vscr, keep, tt, K)
        s = _sigmoid(ug)
        da = da_ref[...].astype(_F32)
        du_g = da * uv * s * (1.0 + ug * (1.0 - s))
        du_v = da * ug * s
        du_ref[0] = du_g.astype(du_ref.dtype)
        du_ref[1] = du_v.astype(du_ref.dtype)

        @pl.when(i == 0)
        def _():
            dwb_ref[...] = jnp.zeros_like(dwb_ref)

        for half, (du, scr) in enumerate(((du_g, gscr), (du_v, vscr))):
            for j in range(K):
                dwb_ref[half, K - 1 - j:K - j, :] += _colsum(du * scr[HB - j:HB - j + tt, :])
            dwb_ref[half, K:K + 1, :] += _colsum(du)

    main = lambda off: pl.BlockSpec((tt, tc), lambda j, i: (i, j + off))
    halo = lambda off: pl.BlockSpec((HB, tc), lambda j, i: (jnp.maximum(i * per - 1, 0), j + off))
    wsp = lambda off: pl.BlockSpec((K, tc), lambda j, i: (0, j + off))
    bsp = lambda off: pl.BlockSpec((1, tc), lambda j, i: (0, j + off))
    return pl.pallas_call(
        body, name=name, grid=(nb, T // tt),
        in_specs=[main(0), main(nb), halo(0), halo(nb), main(0), wsp(0), wsp(nb), bsp(0), bsp(nb)],
        out_specs=[pl.BlockSpec((2, tt, tc), lambda j, i: (0, i, j)),
                   pl.BlockSpec((2, SUBLANES, tc), lambda j, i: (0, 0, j))],
        out_shape=[jax.ShapeDtypeStruct((2, T, F), _MM), jax.ShapeDtypeStruct((2, SUBLANES, F), _F32)],
        scratch_shapes=[pltpu.VMEM((tt + HB, tc), _F32)] * 2,
        compiler_params=_cparams(("parallel", "arbitrary")))(up, up, up, up, dact, w, w, b, b)


def dwconv_transpose(du, w, *, name):
    _, T, F = du.shape
    K = w.shape[0]
    tt = _tile(T, CONV_TILE)
    tc = _tile(F, 512)
    nb = F // tc
    per = tt // FFN_HALO
    HB = FFN_HALO
    nt = T // tt
    last_hb = T // HB - 1

    def body(d_ref, dn_ref, w_ref, o_ref, scr):
        ahead = (pl.program_id(1) < nt - 1).astype(_F32)
        scr[0:tt, :] = d_ref[...].astype(_F32)
        scr[tt:tt + HB, :] = dn_ref[...].astype(_F32) * ahead
        o_ref[...] = _anticausal_taps(w_ref, scr, tt, K).astype(o_ref.dtype)

    return pl.pallas_call(
        body, name=name, grid=(2, nt, nb),
        in_specs=[pl.BlockSpec((None, tt, tc), lambda s, i, j: (s, i, j)),
                  pl.BlockSpec((None, HB, tc), lambda s, i, j: (s, jnp.minimum((i + 1) * per, last_hb), j)),
                  pl.BlockSpec((K, tc), lambda s, i, j: (0, s * nb + j))],
        out_specs=pl.BlockSpec((tt, tc), lambda s, i, j: (i, s * nb + j)),
        out_shape=jax.ShapeDtypeStruct((T, 2 * F), _MM),
        scratch_shapes=[pltpu.VMEM((tt + HB, tc), _F32)],
        compiler_params=_cparams(("parallel", "parallel", "parallel")))(du, du, w)


def _adamw_math(w, g, m, v):
    m = ADAM_B1 * m + (1.0 - ADAM_B1) * g
    v = ADAM_B2 * v + (1.0 - ADAM_B2) * (g * g)
    m_hat = m / (1.0 - ADAM_B1 ** ADAM_STEP)
    v_hat = v / (1.0 - ADAM_B2 ** ADAM_STEP)
    delta = -ADAM_LR * (m_hat / (jnp.sqrt(v_hat) + ADAM_EPS) + ADAM_WD * w)
    return delta, m, v


def _as2d(a):
    return a.reshape(1, -1) if a.ndim == 1 else a.reshape(-1, a.shape[-1])


def adamw(w, g, m, v, *, name):
    shape = w.shape
    w2, g2, m2, v2 = _as2d(w), _as2d(g), _as2d(m), _as2d(v)
    R, C = w2.shape
    lanes = -(-C // LANES) * LANES
    tr = _tile(R, max(SUBLANES, (1 << 20) // (4 * lanes)), SUBLANES)

    def body(w_ref, g_ref, m_ref, v_ref, d_ref, mo_ref, vo_ref):
        d, mn, vn = _adamw_math(w_ref[...], g_ref[...], m_ref[...], v_ref[...])
        d_ref[...] = d
        mo_ref[...] = mn
        vo_ref[...] = vn

    blk = pl.BlockSpec((tr, C), lambda i: (i, 0))
    outs = pl.pallas_call(
        body, name=name, grid=(R // tr,), in_specs=[blk] * 4, out_specs=[blk] * 3,
        out_shape=[jax.ShapeDtypeStruct((R, C), _F32)] * 3,
        compiler_params=_cparams(("parallel",)))(w2, g2, m2, v2)
    return tuple(o.reshape(shape) for o in outs)


def ada_w_update(c_t, d_ada, w, m, v, *, name):
    L, D, N = w.shape
    B = c_t.shape[1]
    tr = _tile(D, 256, SUBLANES)
    tn = _tile(N, 1024)

    def body(c_ref, a_ref, w_ref, m_ref, v_ref, g_ref, d_ref, mo_ref, vo_ref):
        cv = c_ref[...]
        cv = cv * _sigmoid(cv)
        g = jnp.dot(cv, a_ref[...], preferred_element_type=_F32, precision=HIGHEST)
        g_ref[...] = g
        d, mn, vn = _adamw_math(w_ref[...], g, m_ref[...], v_ref[...])
        d_ref[...] = d
        mo_ref[...] = mn
        vo_ref[...] = vn

    blk = pl.BlockSpec((None, tr, tn), lambda l, i, j: (l, i, j))
    return pl.pallas_call(
        body, name=name, grid=(L, D // tr, N // tn),
        in_specs=[pl.BlockSpec((tr, B), lambda l, i, j: (i, 0)),
                  pl.BlockSpec((None, B, tn), lambda l, i, j: (l, 0, j)), blk, blk, blk],
        out_specs=[blk] * 4,
        out_shape=[jax.ShapeDtypeStruct((L, D, N), _F32)] * 4,
        compiler_params=_cparams(("parallel", "parallel", "parallel")))(c_t, d_ada, w, m, v)


def _coords():
    return lax.axis_index("x"), lax.axis_index("y"), lax.axis_index("c")


def allgather_small(x, *, with_sum, name):
    R, C = x.shape

    def body(x_ref, out_ref, *rest):
        if with_sum:
            sum_ref, send_sems, recv_sems, local_sem = rest
        else:
            send_sems, recv_sems, local_sem = rest
        px, py, pc = _coords()
        me, sibling = (px, py, pc), (px, py, 1 - pc)
        chips = [(1 - px, py), (px, 1 - py), (1 - px, 1 - py)]

        def rows(qx, qy, qc):
            return out_ref.at[4 * qx + 2 * qy + qc]

        def copy(k, block, to, src=None):
            return pltpu.make_async_remote_copy(
                src_ref=rows(*block) if src is None else src, dst_ref=rows(*block),
                send_sem=send_sems.at[k], recv_sem=recv_sems.at[k], device_id=to, device_id_type=MESH)

        mine = pltpu.make_async_copy(x_ref, rows(*me), local_sem)
        mine.start()
        first = [copy(0, me, sibling, src=x_ref)]
        first += [copy(1 + j, me, (*chip, pc), src=x_ref) for j, chip in enumerate(chips)]
        for cp in first:
            cp.start()
        passed = [copy(4 + j, (*chip, pc), sibling) for j, chip in enumerate(chips)]
        for j, chip in enumerate(chips):
            copy(1 + j, (*chip, pc), me).wait_recv()
            passed[j].start()
        copy(0, sibling, me).wait_recv()
        for j, chip in enumerate(chips):
            copy(4 + j, (*chip, 1 - pc), me).wait_recv()
        for cp in first + passed:
            cp.wait_send()
        mine.wait()
        if with_sum:
            acc = out_ref[0]
            for k in range(1, N_DEV):
                acc = acc + out_ref[k]
            sum_ref[...] = acc

    vm = pl.BlockSpec(memory_space=pltpu.VMEM)
    out_shape = [jax.ShapeDtypeStruct((N_DEV, R, C), x.dtype)]
    if with_sum:
        out_shape.append(jax.ShapeDtypeStruct((R, C), x.dtype))
    outs = pl.pallas_call(
        body, name=name, in_specs=[vm], out_specs=[vm] * len(out_shape), out_shape=out_shape,
        scratch_shapes=[pltpu.SemaphoreType.DMA((7,)), pltpu.SemaphoreType.DMA((7,)), pltpu.SemaphoreType.DMA],
        compiler_params=pltpu.CompilerParams(vmem_limit_bytes=VMEM_LIMIT_V7X))(x)
    return outs if with_sum else outs[0]


def _at(start, size, align):
    return pl.ds(pl.multiple_of(start, align) if align > 1 else start, size)


class _BigLayout:
    def __init__(self, D, INs, Ds, F2s, Fs):
        self.D, self.INs, self.Ds, self.F2s, self.Fs = D, INs, Ds, F2s, Fs
        self.Dh, self.Dsh, self.Fsh = D // 2, Ds // 2, Fs // 2
        self.piece_shapes = [(self.Dh, INs), (self.Dsh, D), (self.Dh, F2s), (self.Fsh, D)]

    def in_full(self, a, ref, k, h):
        if a == 0:
            return ref.at[k, _at(h * self.Dh, self.Dh, self.Dh), :]
        if a == 1:
            return ref.at[_at(k * self.Ds + h * self.Dsh, self.Dsh, self.Dsh), :]
        if a == 2:
            return ref.at[_at(h * self.Dh, self.Dh, self.Dh), _at(k * self.F2s, self.F2s, self.F2s)]
        return ref.at[_at(k * self.Fs + h * self.Fsh, self.Fsh, self.Fsh), :]

    def in_shard(self, a, ref, h):
        rows = self.piece_shapes[a][0]
        return ref.at[_at(h * rows, rows, rows), :]


def gather_weights(shards, *, name):
    D, INs = shards[0].shape
    lay = _BigLayout(D, INs, shards[1].shape[0], shards[2].shape[1], shards[3].shape[0])
    n_arr = len(shards)

    def body(*refs):
        ins, outs = refs[:n_arr], refs[n_arr:2 * n_arr]
        send_sems, recv_sems, local_sems = refs[2 * n_arr:]
        px, py, pc = _coords()
        my_chip = 2 * px + py
        sibling = (px, py, 1 - pc)
        chips = [(1 - px, py), (px, 1 - py), (1 - px, 1 - py)]

        def copy(a, kk, k, h, to, src=None):
            dst = lay.in_full(a, outs[a], k, h)
            return pltpu.make_async_remote_copy(
                src_ref=dst if src is None else src, dst_ref=dst,
                send_sem=send_sems.at[7 * a + kk], recv_sem=recv_sems.at[7 * a + kk],
                device_id=to, device_id_type=MESH)

        sends, mine = [], []
        for a in range(n_arr):
            own = lay.in_shard(a, ins[a], pc)
            mine.append(pltpu.make_async_copy(own, lay.in_full(a, outs[a], my_chip, pc), local_sems.at[a]))
            sends.append(copy(a, 0, my_chip, pc, sibling, src=own))
            sends += [copy(a, 1 + j, my_chip, pc, (*chip, pc), src=own) for j, chip in enumerate(chips)]
        for cp in mine + sends:
            cp.start()
        for j, (qx, qy) in enumerate(chips):
            for a in range(n_arr):
                copy(a, 1 + j, 2 * qx + qy, pc, sibling).wait_recv()
                fwd = copy(a, 4 + j, 2 * qx + qy, pc, sibling)
                fwd.start()
                sends.append(fwd)
        for a in range(n_arr):
            copy(a, 0, my_chip, 1 - pc, sibling).wait_recv()
            for j, (qx, qy) in enumerate(chips):
                copy(a, 4 + j, 2 * qx + qy, 1 - pc, sibling).wait_recv()
        for cp in sends:
            cp.wait_send()
        for cp in mine:
            cp.wait()

    hbm = pl.BlockSpec(memory_space=pl.ANY)
    dt = shards[0].dtype
    out_shape = [jax.ShapeDtypeStruct((N_CHIP, D, INs), dt), jax.ShapeDtypeStruct((D, D), dt),
                 jax.ShapeDtypeStruct((D, lay.F2s * N_CHIP), dt), jax.ShapeDtypeStruct((lay.Fs * N_CHIP, D), dt)]
    return pl.pallas_call(
        body, name=name, in_specs=[hbm] * n_arr, out_specs=[hbm] * n_arr, out_shape=out_shape,
        scratch_shapes=[pltpu.SemaphoreType.DMA((7 * n_arr,)), pltpu.SemaphoreType.DMA((7 * n_arr,)),
                        pltpu.SemaphoreType.DMA((n_arr,))],
    )(*shards)


def scatter_grads(partials, *, name):
    _, D, INs = partials[0].shape
    lay = _BigLayout(D, INs, partials[1].shape[0] // N_CHIP, partials[2].shape[1] // N_CHIP,
                     partials[3].shape[0] // N_CHIP)
    n_arr = len(partials)

    def body(*refs):
        ins, outs = refs[:n_arr], refs[n_arr:2 * n_arr]
        send_sems, recv_sems, local_sems = refs[2 * n_arr:]
        px, py, pc = _coords()
        me = 4 * px + 2 * py + pc
        copies, mine = [], []
        for a in range(n_arr):
            mine.append(pltpu.make_async_copy(lay.in_full(a, ins[a], 2 * px + py, pc), outs[a].at[me],
                                              local_sems.at[a]))
            for mask in range(1, N_DEV):
                qx = 1 - px if (mask >> 2) & 1 else px
                qy = 1 - py if (mask >> 1) & 1 else py
                qc = 1 - pc if mask & 1 else pc
                copies.append(pltpu.make_async_remote_copy(
                    src_ref=lay.in_full(a, ins[a], 2 * qx + qy, qc), dst_ref=outs[a].at[me],
                    send_sem=send_sems.at[7 * a + mask - 1], recv_sem=recv_sems.at[7 * a + mask - 1],
                    device_id=(qx, qy, qc), device_id_type=MESH))
        for cp in mine + copies:
            cp.start()
        for cp in copies:
            cp.wait_recv()
        for cp in copies:
            cp.wait_send()
        for cp in mine:
            cp.wait()

    hbm = pl.BlockSpec(memory_space=pl.ANY)
    dt = partials[0].dtype
    return pl.pallas_call(
        body, name=name, in_specs=[hbm] * n_arr, out_specs=[hbm] * n_arr,
        out_shape=[jax.ShapeDtypeStruct((N_DEV, *s), dt) for s in lay.piece_shapes],
        scratch_shapes=[pltpu.SemaphoreType.DMA((7 * n_arr,)), pltpu.SemaphoreType.DMA((7 * n_arr,)),
                        pltpu.SemaphoreType.DMA((n_arr,))],
    )(*partials)


SIBLING_CHUNKS = 8


def sibling_exchange(halves, *, name):
    n_arr = len(halves)
    n_ch = [max(n for n in (SIBLING_CHUNKS, 4, 2, 1) if x.shape[0] % (n * SUBLANES) == 0 or n == 1)
            for x in halves]
    offs = [sum(n_ch[:a]) for a in range(n_arr)]

    def body(*refs):
        ins, outs = refs[:n_arr], refs[n_arr:2 * n_arr]
        send_sems, recv_sems, local_sems = refs[2 * n_arr:]
        px, py, pc = _coords()
        copies, mine = [], []
        for a in range(n_arr):
            rows = halves[a].shape[0] // n_ch[a]
            for q in range(n_ch[a]):
                src = ins[a].at[pl.ds(q * rows, rows), :]
                dst = outs[a].at[pc, pl.ds(q * rows, rows), :]
                mine.append(pltpu.make_async_copy(src, dst, local_sems.at[offs[a] + q]))
                copies.append(pltpu.make_async_remote_copy(
                    src_ref=src, dst_ref=dst,
                    send_sem=send_sems.at[offs[a] + q], recv_sem=recv_sems.at[offs[a] + q],
                    device_id=(px, py, 1 - pc), device_id_type=MESH))
        for cp in mine + copies:
            cp.start()
        for cp in copies:
            cp.wait_recv()
        for cp in copies:
            cp.wait_send()
        for cp in mine:
            cp.wait()

    hbm = pl.BlockSpec(memory_space=pl.ANY)
    return pl.pallas_call(
        body, name=name, in_specs=[hbm] * n_arr, out_specs=[hbm] * n_arr,
        out_shape=[jax.ShapeDtypeStruct((2, *x.shape), x.dtype) for x in halves],
        scratch_shapes=[pltpu.SemaphoreType.DMA((sum(n_ch),))] * 3,
    )(*halves)


def sum_slots(x, *, name):
    n, R, C = x.shape
    lanes = -(-C // LANES) * LANES
    tr = _tile(R, max(BF16_ROWS, (4 << 20) // (n * 2 * lanes)), BF16_ROWS)

    def body(x_ref, o_ref):
        acc = x_ref[0].astype(_F32)
        for k in range(1, n):
            acc = acc + x_ref[k].astype(_F32)
        o_ref[...] = acc

    return pl.pallas_call(
        body, name=name, grid=(R // tr,),
        in_specs=[pl.BlockSpec((n, tr, C), lambda i: (0, i, 0))],
        out_specs=pl.BlockSpec((tr, C), lambda i: (i, 0)),
        out_shape=jax.ShapeDtypeStruct((R, C), _F32),
        compiler_params=_cparams(("parallel",)))(x)


def _pack_flat(arrays, quantum):
    flat = jnp.concatenate([a.reshape(-1) for a in arrays])
    pad = (-flat.shape[0]) % quantum
    return jnp.pad(flat, (0, pad)) if pad else flat


def _unpack_flat(flat, shapes):
    out, off = [], 0
    for s in shapes:
        n = math.prod(s)
        out.append(flat[off:off + n].reshape(s))
        off += n
    return out


def _small_pack(arrays):
    return _pack_flat([a.astype(_F32) for a in arrays], SUBLANES * LANES).reshape(-1, LANES)


def kernel(x, c, ada_w, ada_b, mix_norm_g, w_in, b_forget, conf_dw_w, conf_dw_b, conf_ln_g, conf_ln_b, sc_dw_w, w_out, ffn_norm_g, w_up, ffn_dw_w, ffn_dw_b, w_down, final_norm_g, loss_target, m_ada_w, m_ada_b, m_mix_norm_g, m_w_in, m_b_forget, m_conf_dw_w, m_conf_dw_b, m_conf_ln_g, m_conf_ln_b, m_sc_dw_w, m_w_out, m_ffn_norm_g, m_w_up, m_ffn_dw_w, m_ffn_dw_b, m_w_down, m_final_norm_g, v_ada_w, v_ada_b, v_mix_norm_g, v_w_in, v_b_forget, v_conf_dw_w, v_conf_dw_b, v_conf_ln_g, v_conf_ln_b, v_sc_dw_w, v_w_out, v_ffn_norm_g, v_w_up, v_ffn_dw_w, v_ffn_dw_b, v_w_down, v_final_norm_g):
    _, T, D = x.shape
    L = ada_w.shape[0]
    A = D // 2
    H = A // HEAD_DIM
    C = D // 4
    assert D - A - C == C
    IN = 3 * A + H + 5 * C
    NM = 3 * A + 5 * C
    NP = NM + LANES
    F2 = w_up.shape[2] * N_CHIP
    F = F2 // 2
    NA = ada_w.shape[2]
    assert NA * N_CHIP == 6 * D and w_in.shape[2] * N_CHIP == IN

    px, py, pc = _coords()
    chip = 2 * px + py
    me = 2 * chip + pc

    x0 = x[0]
    tgt = loss_target[0]

    c_all = allgather_small(c.reshape(-1, LANES), with_sum=False, name="gather_c").reshape(N_DEV, D)
    parts = [matmul(c_all, ada_w[l], out_dtype=_F32, name="ada_fwd", tm=N_DEV, tn=512, tk=D,
                    a_silu=True, precision=HIGHEST) for l in range(L)]
    parts = jnp.stack(parts)
    got = allgather_small(parts.reshape(-1, LANES), with_sum=False, name="gather_ada")
    got = got.reshape(N_DEV, L, N_DEV, NA)[0::2]
    mine = lax.dynamic_index_in_dim(got, me, axis=2, keepdims=False)
    ada = jnp.transpose(mine, (1, 0, 2)).reshape(L, 6 * D) + ada_b

    INs = IN // N_CHIP
    wp_l, wout_l, wup_l, wdown_l = [], [], [], []
    for l in range(L):
        wi4, wo, wu, wd = gather_weights(
            [w_in[l].astype(_MM), w_out[l].astype(_MM), w_up[l].astype(_MM), w_down[l].astype(_MM)],
            name="gather_weights")
        wi = jnp.transpose(wi4, (1, 0, 2)).reshape(D, IN)
        wp_l.append(jnp.concatenate(
            [wi[:, :3 * A], wi[:, 3 * A + H:], jnp.pad(wi[:, 3 * A:3 * A + H], ((0, 0), (0, LANES - H)))], axis=1))
        wout_l.append(wo)
        wup_l.append(wu)
        wdown_l.append(wd)

    small_w = _small_pack([conf_dw_w, sc_dw_w, ffn_dw_w])
    sw_all = allgather_small(small_w, with_sum=False, name="gather_small_w")[0::2].reshape(N_CHIP, -1)
    sw_shapes = [conf_dw_w.shape, sc_dw_w.shape, ffn_dw_w.shape]
    sw_parts = [_unpack_flat(sw_all[k], sw_shapes) for k in range(N_CHIP)]
    conf_w_full = jnp.concatenate([p[0] for p in sw_parts], axis=-1)
    sc_w_full = jnp.concatenate([p[1] for p in sw_parts], axis=-1)
    ffn_w_full = jnp.concatenate([p[2] for p in sw_parts], axis=-1)
    bf_pad = jnp.pad(b_forget, ((0, 0), (0, LANES - H)))

    row = lambda a: a.reshape(1, -1)

    saved = []
    x_cur, branch, gate = x0, None, None
    for l in range(L):
        sh_m, sc_m, g_m, sh_f, sc_f, g_f = [row(ada[l, k * D:(k + 1) * D]) for k in range(6)]
        a1 = row(mix_norm_g[l]) * (1.0 + sc_m)
        a2 = row(ffn_norm_g[l]) * (1.0 + sc_f)
        x_in, h1 = resid_norm_fwd(x_cur, a1, sh_m, branch, gate, name="norm_mix_fwd")
        wp = wp_l[l]
        proj = matmul(h1, wp, out_dtype=_MM, name="proj_fwd", tn=512, tk=D, b_cols=(0, NM))
        flog = matmul(h1, wp, out_dtype=_F32, name="fgate_logits", tn=LANES, tk=D, b_cols=(NM, LANES))
        bf = row(bf_pad[l])
        fcum = fgate_fwd(flog, bf, name="fgate_fwd")
        f_t = fcum[:, :H].T
        fq = jnp.broadcast_to(f_t[:, :, None], (H, T, LANES))
        fk = f_t[:, None, :]
        attn, lse = attn_fwd(proj, fq, fk, heads=H, name="attn_fwd")
        cw, cb = conf_w_full[l], row(conf_dw_b[l])
        lg, lb, sw = row(conf_ln_g[l]), row(conf_ln_b[l]), sc_w_full[l]
        cm, cc, zc = mixer_misc_fwd(proj, cw, cb, lg, lb, sw, width=C, base_col=3 * A, name="misc_fwd")
        cat = jnp.concatenate([attn, cm], axis=1)
        mixed = matmul(cat, wout_l[l], out_dtype=_F32, name="wout_fwd", tn=512, tk=D)
        x_mid, h2 = resid_norm_fwd(x_in, a2, sh_f, mixed, g_m, name="norm_ffn_fwd")
        up = matmul(h2, wup_l[l], out_dtype=_MM, name="wup_fwd", tn=512, tk=D)
        fw, fb = ffn_w_full[l], row(ffn_dw_b[l])
        act = ffn_act_fwd(up, fw, fb, name="ffn_act_fwd")
        dn = matmul(act, wdown_l[l], out_dtype=_F32, name="wdown_fwd", tk=1408)
        saved.append(dict(x_in=x_in, h1=h1, proj=proj, flog=flog, fq=fq, fk=fk, attn=attn, lse=lse, cc=cc, zc=zc,
                          cat=cat, mixed=mixed, x_mid=x_mid, h2=h2, up=up, act=act, dn=dn, a1=a1, a2=a2,
                          g_m=g_m, g_f=g_f, sc_m=sc_m, sc_f=sc_f, bf=bf, cw=cw, lg=lg, lb=lb, sw=sw, fw=fw, fb=fb))
        x_cur, branch, gate = x_mid, dn, g_f

    dx, loss_row, d_final_g = final_loss_bwd(x_cur, branch, gate, row(final_norm_g), tgt, name="loss_bwd")

    KF = ffn_dw_w.shape[1]
    g_big = [None] * L
    d_ada, d_g1, d_g2, d_bf, d_cw, d_cb, d_lg, d_lb, d_sw, d_fw, d_fb = ([None] * L for _ in range(11))
    for l in reversed(range(L)):
        s = saved[l]
        ddn, dg_f = gate_bwd(dx, s["dn"], s["g_f"], name="gate_ffn_bwd")
        dact = matmul(ddn, wdown_l[l], out_dtype=_MM, name="wdown_dgrad", trans_b=True, tn=512, tk=D)
        gw_down = matmul(s["act"], ddn, out_dtype=_MM, name="wdown_wgrad", trans_a=True, tm=512)
        du, dwb = ffn_bwd_du(s["up"], dact, s["fw"], s["fb"], name="ffn_bwd_du")
        dup = dwconv_transpose(du, s["fw"], name="ffn_bwd_dup")
        dh2 = matmul(dup, wup_l[l], out_dtype=_MM, name="wup_dgrad", trans_b=True)
        gw_up = matmul(s["h2"], dup, out_dtype=_MM, name="wup_wgrad", trans_a=True)
        dx_mid, dsh_f, da2 = norm_bwd(s["x_mid"], dh2, dx, s["a2"], name="norm_ffn_bwd")
        dmixed, dg_m = gate_bwd(dx_mid, s["mixed"], s["g_m"], name="gate_mix_bwd")
        dcat = matmul(dmixed, wout_l[l], out_dtype=_MM, name="wout_dgrad", trans_b=True, tn=512, tk=D)
        gw_out = matmul(s["cat"], dmixed, out_dtype=_MM, name="wout_wgrad", trans_a=True)
        dk, dv, dfk = attn_bwd_dkv(s["proj"], dcat, s["attn"], s["fq"], s["fk"], s["lse"], heads=H,
                                   name="attn_bwd_dkv")
        dq, dfq = attn_bwd_dq(s["proj"], dcat, s["attn"], s["fq"], s["fk"], s["lse"], heads=H,
                              name="attn_bwd_dq")
        dmisc, d_cw[l], d_cb[l], d_lg[l], d_lb[l], d_sw[l] = mixer_misc_bwd(
            s["proj"], dcat, s["cc"], s["zc"], s["cw"], s["lg"], s["lb"], s["sw"],
            width=C, base_col=3 * A, dbase_col=A, name="misc_bwd")
        dfk_pad = jnp.pad((dfk[:, 0, :] + dfq[:, :, 0]).T, ((0, 0), (0, LANES - H)))
        dflog, dbf = fgate_bwd(dfk_pad, s["flog"], s["bf"], name="fgate_bwd")
        dproj = jnp.concatenate([dq, dk, dv, dmisc, dflog], axis=1)
        dh1 = matmul(dproj, wp_l[l], out_dtype=_MM, name="proj_dgrad", trans_b=True, tk=640)
        gwp = matmul(s["h1"], dproj, out_dtype=_MM, name="proj_wgrad", trans_a=True, tn=640)
        dx, dsh_m, da1 = norm_bwd(s["x_in"], dh1, dx_mid, s["a1"], name="norm_mix_bwd")

        g1, g2 = row(mix_norm_g[l]), row(ffn_norm_g[l])
        d_ada[l] = jnp.concatenate([dsh_m, da1 * g1, dg_m, dsh_f, da2 * g2, dg_f], axis=1)[0]
        d_g1[l] = (da1 * (1.0 + s["sc_m"]))[0]
        d_g2[l] = (da2 * (1.0 + s["sc_f"]))[0]
        d_bf[l] = dbf[0, :H]
        d_fw[l] = jnp.concatenate([dwb[0, :KF], dwb[1, :KF]], axis=1)
        d_fb[l] = jnp.concatenate([dwb[0, KF], dwb[1, KF]])
        gw_in = jnp.concatenate([gwp[:, :3 * A], gwp[:, NM:NM + H], gwp[:, 3 * A:NM]], axis=1)
        g_big[l] = (gw_in, gw_out, gw_up, gw_down)

    small = [loss_row[0], jnp.stack(d_g1), jnp.stack(d_bf), jnp.stack(d_cw), jnp.stack(d_cb)[:, 0],
             jnp.stack(d_lg)[:, 0], jnp.stack(d_lb)[:, 0], jnp.stack(d_sw), jnp.stack(d_g2), jnp.stack(d_fw),
             jnp.stack(d_fb), d_final_g[0], jnp.stack(d_ada)]
    small_shapes = [a.shape for a in small]
    sm_all, sm_sum = allgather_small(_small_pack(small), with_sum=True, name="reduce_small")
    (loss_v, g_mix_norm, g_bf, g_cw_full, g_cb, g_lg, g_lb, g_sw_full, g_ffn_norm, g_fw_full, g_fb, g_final,
     g_ada_b) = _unpack_flat(sm_sum.reshape(-1), small_shapes)
    loss = loss_v[0]
    n_ada = L * 6 * D
    off_ada = sum(math.prod(sh) for sh in small_shapes[:-1])
    d_ada_all = sm_all.reshape(N_DEV, -1)[:, off_ada:off_ada + n_ada].reshape(N_DEV, L, 6 * D)
    d_ada_chip = lax.dynamic_slice_in_dim(d_ada_all, chip * NA, NA, axis=2)
    d_ada_chip = jnp.transpose(d_ada_chip, (1, 0, 2))
    cshard = lambda a: lax.dynamic_slice_in_dim(a, chip * (a.shape[-1] // N_CHIP), a.shape[-1] // N_CHIP,
                                                axis=a.ndim - 1)
    g_conf_dw_w, g_sc_dw_w, g_ffn_dw_w = cshard(g_cw_full), cshard(g_sw_full), cshard(g_fw_full)

    g_in_l, g_out_l, g_up_l, g_down_l = [], [], [], []
    for l in range(L):
        gw_in, gw_out, gw_up, gw_down = g_big[l]
        gw_in4 = jnp.transpose(gw_in.reshape(D, N_CHIP, INs), (1, 0, 2))
        recv = scatter_grads([gw_in4, gw_out, gw_up, gw_down], name="scatter_grads")
        red = [sum_slots(r, name="sum_grads") for r in recv]
        gi, go, gu, gd = sibling_exchange(red, name="sibling_grads")
        g_in_l.append(gi.reshape(w_in.shape[1:])); g_out_l.append(go.reshape(w_out.shape[1:]))
        g_up_l.append(gu.reshape(w_up.shape[1:])); g_down_l.append(gd.reshape(w_down.shape[1:]))
    g_w_in, g_w_out, g_w_up, g_w_down = (jnp.stack(t) for t in (g_in_l, g_out_l, g_up_l, g_down_l))

    g_ada_w, dl_ada_w, nm_ada_w, nv_ada_w = ada_w_update(c_all.T, d_ada_chip, ada_w, m_ada_w, v_ada_w,
                                                          name="ada_w_update")

    grads = dict(ada_b=g_ada_b, mix_norm_g=g_mix_norm, w_in=g_w_in, b_forget=g_bf, conf_dw_w=g_conf_dw_w,
                 conf_dw_b=g_cb, conf_ln_g=g_lg, conf_ln_b=g_lb, sc_dw_w=g_sc_dw_w, w_out=g_w_out,
                 ffn_norm_g=g_ffn_norm, w_up=g_w_up, ffn_dw_w=g_ffn_dw_w, ffn_dw_b=g_fb, w_down=g_w_down,
                 final_norm_g=g_final)
    weights = dict(ada_b=(ada_b, m_ada_b, v_ada_b), mix_norm_g=(mix_norm_g, m_mix_norm_g, v_mix_norm_g),
                   w_in=(w_in, m_w_in, v_w_in), b_forget=(b_forget, m_b_forget, v_b_forget),
                   conf_dw_w=(conf_dw_w, m_conf_dw_w, v_conf_dw_w), conf_dw_b=(conf_dw_b, m_conf_dw_b, v_conf_dw_b),
                   conf_ln_g=(conf_ln_g, m_conf_ln_g, v_conf_ln_g), conf_ln_b=(conf_ln_b, m_conf_ln_b, v_conf_ln_b),
                   sc_dw_w=(sc_dw_w, m_sc_dw_w, v_sc_dw_w), w_out=(w_out, m_w_out, v_w_out),
                   ffn_norm_g=(ffn_norm_g, m_ffn_norm_g, v_ffn_norm_g), w_up=(w_up, m_w_up, v_w_up),
                   ffn_dw_w=(ffn_dw_w, m_ffn_dw_w, v_ffn_dw_w), ffn_dw_b=(ffn_dw_b, m_ffn_dw_b, v_ffn_dw_b),
                   w_down=(w_down, m_w_down, v_w_down), final_norm_g=(final_norm_g, m_final_norm_g, v_final_norm_g))
    order = ["ada_w", "ada_b", "mix_norm_g", "w_in", "b_forget", "conf_dw_w", "conf_dw_b", "conf_ln_g", "conf_ln_b",
             "sc_dw_w", "w_out", "ffn_norm_g", "w_up", "ffn_dw_w", "ffn_dw_b", "w_down", "final_norm_g"]
    g_out, d_out, m_out, v_out = {}, {}, {}, {}
    g_out["ada_w"], d_out["ada_w"], m_out["ada_w"], v_out["ada_w"] = g_ada_w, dl_ada_w, nm_ada_w, nv_ada_w
    for n in order[1:]:
        w, m, v = weights[n]
        g = grads[n].reshape(w.shape)
        g_out[n] = g
        d_out[n], m_out[n], v_out[n] = adamw(w, g, m, v, name="adamw_" + n)

    return (loss, dx[None], *[g_out[n] for n in order], *[d_out[n] for n in order],
            *[m_out[n] for n in order], *[v_out[n] for n in order])
```

```python
import functools
import math

import jax
import jax.numpy as jnp
from jax import lax
from jax.experimental import pallas as pl
from jax.experimental.pallas import tpu as pltpu

_MM = jnp.bfloat16
_F32 = jnp.float32
VMEM_LIMIT_V7X = 48 * 1024 * 1024
LANES = 128
SUBLANES = 8
BF16_ROWS = 16
HEAD_DIM = 128
RMS_EPS = 1e-6
LN_EPS = 1e-5
NEG = -1e30
HIGHEST = lax.Precision.HIGHEST

ADAM_LR = 0.001
ADAM_B1 = 0.9
ADAM_B2 = 0.999
ADAM_EPS = 1e-08
ADAM_WD = 0.01
ADAM_STEP = 10

N_DEV = 8
N_CHIP = 4

ATT_BLOCK = 512
CONV_TILE = 512
CONV_HALO = 32
FFN_HALO = BF16_ROWS
NORM_TILE = 256
MM_TM = 1024
MM_TN = 1024
MM_TK = 1024

MESH = pl.DeviceIdType.MESH


def _tile(dim, pref, mult=LANES):
    t = (min(pref, dim) // mult) * mult
    while t >= mult:
        if dim % t == 0:
            return t
        t -= mult
    return dim


def _cparams(sem):
    return pltpu.CompilerParams(dimension_semantics=sem, vmem_limit_bytes=VMEM_LIMIT_V7X)


def _sigmoid(x):
    return 1.0 / (1.0 + jnp.exp(-x))


def _colsum(x):
    return jnp.sum(x, axis=0, keepdims=True)


def matmul(a, b, *, out_dtype, name, trans_a=False, trans_b=False, tm=MM_TM, tn=MM_TN, tk=MM_TK,
           a_silu=False, precision=None, b_cols=None):
    M, K = (a.shape[1], a.shape[0]) if trans_a else a.shape
    N = b.shape[0] if trans_b else b.shape[1]
    assert (b.shape[1] if trans_b else b.shape[0]) == K
    col0 = 0
    if b_cols is not None:
        assert not trans_b
        col0, N = b_cols
    tm = _tile(M, tm, SUBLANES if (M % LANES) else LANES)
    tn = _tile(math.gcd(N, col0) if col0 else N, tn)
    tk = _tile(K, tk)
    nk = K // tk
    jb = col0 // tn
    dims = (((0 if trans_a else 1,), (1 if trans_b else 0,)), ((), ()))

    def body(a_ref, b_ref, o_ref, *scratch):
        av = a_ref[...]
        if a_silu:
            av = av * _sigmoid(av)
        part = lax.dot_general(av, b_ref[...], dims, preferred_element_type=_F32, precision=precision)
        if nk == 1:
            o_ref[...] = part.astype(o_ref.dtype)
        else:
            acc_ref, = scratch
            k = pl.program_id(2)

            @pl.when(k == 0)
            def _():
                acc_ref[...] = part

            @pl.when(k > 0)
            def _():
                acc_ref[...] += part

            @pl.when(k == nk - 1)
            def _():
                o_ref[...] = acc_ref[...].astype(o_ref.dtype)

    a_spec = (pl.BlockSpec((tk, tm), lambda i, j, k: (k, i)) if trans_a
              else pl.BlockSpec((tm, tk), lambda i, j, k: (i, k)))
    b_spec = (pl.BlockSpec((tn, tk), lambda i, j, k: (j, k)) if trans_b
              else pl.BlockSpec((tk, tn), lambda i, j, k: (k, j + jb)))
    return pl.pallas_call(
        body, name=name, grid=(M // tm, N // tn, nk),
        in_specs=[a_spec, b_spec],
        out_specs=pl.BlockSpec((tm, tn), lambda i, j, k: (i, j)),
        out_shape=jax.ShapeDtypeStruct((M, N), out_dtype),
        scratch_shapes=[pltpu.VMEM((tm, tn), _F32)] if nk > 1 else [],
        compiler_params=_cparams(("parallel", "parallel", "arbitrary")),
    )(a, b)


def resid_norm_fwd(x, a, sh, branch=None, gate=None, *, name):
    T, D = x.shape
    tm = _tile(T, NORM_TILE, BF16_ROWS)
    has_res = branch is not None

    def body(*refs):
        if has_res:
            x_ref, br_ref, g_ref, a_ref, sh_ref, xo_ref, h_ref = refs
            xv = x_ref[...] + g_ref[...] * br_ref[...]
            xo_ref[...] = xv
        else:
            x_ref, a_ref, sh_ref, h_ref = refs
            xv = x_ref[...]
        r = lax.rsqrt(jnp.mean(xv * xv, axis=-1, keepdims=True) + RMS_EPS)
        h_ref[...] = (xv * r * a_ref[...] + sh_ref[...]).astype(h_ref.dtype)

    row = pl.BlockSpec((tm, D), lambda i: (i, 0))
    vec = pl.BlockSpec((1, D), lambda i: (0, 0))
    if has_res:
        xo, h = pl.pallas_call(
            body, name=name, grid=(T // tm,), in_specs=[row, row, vec, vec, vec], out_specs=[row, row],
            out_shape=[jax.ShapeDtypeStruct((T, D), _F32), jax.ShapeDtypeStruct((T, D), _MM)],
            compiler_params=_cparams(("parallel",)))(x, branch, gate, a, sh)
        return xo, h
    h = pl.pallas_call(
        body, name=name, grid=(T // tm,), in_specs=[row, vec, vec], out_specs=row,
        out_shape=jax.ShapeDtypeStruct((T, D), _MM),
        compiler_params=_cparams(("parallel",)))(x, a, sh)
    return x, h


def norm_bwd(x, dh, dx_in, a, *, name):
    T, D = x.shape
    tm = _tile(T, NORM_TILE, BF16_ROWS)

    def body(x_ref, dh_ref, dxi_ref, a_ref, dxo_ref, dsh_ref, da_ref):
        i = pl.program_id(0)
        xv = x_ref[...]
        r = lax.rsqrt(jnp.mean(xv * xv, axis=-1, keepdims=True) + RMS_EPS)
        n = xv * r
        dhv = dh_ref[...].astype(_F32)
        dn = dhv * a_ref[...]
        dxo_ref[...] = dxi_ref[...] + r * (dn - n * jnp.mean(dn * n, axis=-1, keepdims=True))

        @pl.when(i == 0)
        def _():
            dsh_ref[...] = jnp.zeros_like(dsh_ref)
            da_ref[...] = jnp.zeros_like(da_ref)

        dsh_ref[...] += _colsum(dhv)
        da_ref[...] += _colsum(dhv * n)

    row = pl.BlockSpec((tm, D), lambda i: (i, 0))
    vec = pl.BlockSpec((1, D), lambda i: (0, 0))
    return pl.pallas_call(
        body, name=name, grid=(T // tm,), in_specs=[row, row, row, vec], out_specs=[row, vec, vec],
        out_shape=[jax.ShapeDtypeStruct((T, D), _F32), jax.ShapeDtypeStruct((1, D), _F32),
                   jax.ShapeDtypeStruct((1, D), _F32)],
        compiler_params=_cparams(("arbitrary",)))(x, dh, dx_in, a)


def gate_bwd(dx, branch, gate, *, name):
    T, D = dx.shape
    tm = _tile(T, NORM_TILE, BF16_ROWS)

    def body(dx_ref, br_ref, g_ref, db_ref, dg_ref):
        i = pl.program_id(0)
        dxv = dx_ref[...]
        db_ref[...] = (dxv * g_ref[...]).astype(db_ref.dtype)

        @pl.when(i == 0)
        def _():
            dg_ref[...] = jnp.zeros_like(dg_ref)

        dg_ref[...] += _colsum(dxv * br_ref[...])

    row = pl.BlockSpec((tm, D), lambda i: (i, 0))
    vec = pl.BlockSpec((1, D), lambda i: (0, 0))
    return pl.pallas_call(
        body, name=name, grid=(T // tm,), in_specs=[row, row, vec], out_specs=[row, vec],
        out_shape=[jax.ShapeDtypeStruct((T, D), _MM), jax.ShapeDtypeStruct((1, D), _F32)],
        compiler_params=_cparams(("arbitrary",)))(dx, branch, gate)


def final_loss_bwd(x, branch, gate, gfin, tgt, *, name):
    T, D = x.shape
    tm = _tile(T, NORM_TILE, BF16_ROWS)

    def body(x_ref, br_ref, g_ref, gf_ref, t_ref, dx_ref, loss_ref, dgf_ref):
        i = pl.program_id(0)
        xv = x_ref[...] + g_ref[...] * br_ref[...]
        r = lax.rsqrt(jnp.mean(xv * xv, axis=-1, keepdims=True) + RMS_EPS)
        n = xv * r
        e = n * gf_ref[...] - t_ref[...]
        dy = e * (1.0 / D)
        dn = dy * gf_ref[...]
        dx_ref[...] = r * (dn - n * jnp.mean(dn * n, axis=-1, keepdims=True))

        @pl.when(i == 0)
        def _():
            loss_ref[...] = jnp.zeros_like(loss_ref)
            dgf_ref[...] = jnp.zeros_like(dgf_ref)

        per_row = jnp.mean(e * e, axis=-1, keepdims=True)
        loss_ref[...] += jnp.broadcast_to(0.5 * _colsum(per_row), loss_ref.shape)
        dgf_ref[...] += _colsum(dy * n)

    row = pl.BlockSpec((tm, D), lambda i: (i, 0))
    vec = pl.BlockSpec((1, D), lambda i: (0, 0))
    lvec = pl.BlockSpec((1, LANES), lambda i: (0, 0))
    return pl.pallas_call(
        body, name=name, grid=(T // tm,), in_specs=[row, row, vec, vec, row], out_specs=[row, lvec, vec],
        out_shape=[jax.ShapeDtypeStruct((T, D), _F32), jax.ShapeDtypeStruct((1, LANES), _F32),
                   jax.ShapeDtypeStruct((1, D), _F32)],
        compiler_params=_cparams(("arbitrary",)))(x, branch, gate, gfin, tgt)


def _log_sigmoid(x):
    return jnp.minimum(x, 0.0) - jnp.log(1.0 + jnp.exp(-jnp.abs(x)))


def fgate_fwd(flog, bf, *, name):
    T = flog.shape[0]
    tt = _tile(T, 256)

    def body(x_ref, b_ref, f_ref, carry):
        i = pl.program_id(0)

        @pl.when(i == 0)
        def _():
            carry[...] = jnp.zeros_like(carry)

        lf = _log_sigmoid(x_ref[...] + b_ref[...])
        rows = lax.broadcasted_iota(jnp.int32, (tt, tt), 0)
        cols = lax.broadcasted_iota(jnp.int32, (tt, tt), 1)
        tri = (cols <= rows).astype(_F32)
        f_ref[...] = jnp.dot(tri, lf, preferred_element_type=_F32, precision=HIGHEST) + carry[0:1, :]
        carry[0:1, :] = f_ref[tt - 1:tt, :]

    return pl.pallas_call(
        body, name=name, grid=(T // tt,),
        in_specs=[pl.BlockSpec((tt, LANES), lambda i: (i, 0)), pl.BlockSpec((1, LANES), lambda i: (0, 0))],
        out_specs=pl.BlockSpec((tt, LANES), lambda i: (i, 0)),
        out_shape=jax.ShapeDtypeStruct((T, LANES), _F32),
        scratch_shapes=[pltpu.VMEM((SUBLANES, LANES), _F32)],
        compiler_params=_cparams(("arbitrary",)))(flog, bf)


def fgate_bwd(dfk, flog, bf, *, name):
    T = flog.shape[0]
    tt = _tile(T, 256)
    nb = T // tt

    def body(d_ref, x_ref, b_ref, o_ref, db_ref, carry):
        i = pl.program_id(0)

        @pl.when(i == 0)
        def _():
            carry[...] = jnp.zeros_like(carry)
            db_ref[...] = jnp.zeros_like(db_ref)

        rows = lax.broadcasted_iota(jnp.int32, (tt, tt), 0)
        cols = lax.broadcasted_iota(jnp.int32, (tt, tt), 1)
        upper = (cols >= rows).astype(_F32)
        dlf = jnp.dot(upper, d_ref[...], preferred_element_type=_F32, precision=HIGHEST) + carry[0:1, :]
        carry[0:1, :] = dlf[0:1, :]
        dfl = dlf * _sigmoid(-(x_ref[...] + b_ref[...]))
        o_ref[...] = dfl.astype(o_ref.dtype)
        db_ref[...] += _colsum(dfl)

    rev = pl.BlockSpec((tt, LANES), lambda i: (nb - 1 - i, 0))
    vec = pl.BlockSpec((1, LANES), lambda i: (0, 0))
    return pl.pallas_call(
        body, name=name, grid=(nb,), in_specs=[rev, rev, vec], out_specs=[rev, vec],
        out_shape=[jax.ShapeDtypeStruct((T, LANES), _MM), jax.ShapeDtypeStruct((1, LANES), _F32)],
        scratch_shapes=[pltpu.VMEM((SUBLANES, LANES), _F32)],
        compiler_params=_cparams(("arbitrary",)))(dfk, flog, bf)


def _att_scores(q, k, fq, fk, rep, masked):
    s = lax.dot_general(q, k, (((1,), (1,)), ((), ())), preferred_element_type=_F32)
    s = s * (HEAD_DIM ** -0.5) + (jnp.tile(fq, (1, rep)) - fk)
    if masked:
        rows = lax.broadcasted_iota(jnp.int32, s.shape, 0)
        cols = lax.broadcasted_iota(jnp.int32, s.shape, 1)
        s = jnp.where(cols <= rows, s, NEG)
    return s


def _fold_q(r, t, nb):
    first = t <= r
    return jnp.where(first, r, nb - 1 - r), jnp.where(first, t, t - r - 1)


def _fold_k(r, t, nb):
    first = t < nb - r
    return jnp.where(first, r, nb - 1 - r), jnp.where(first, r + t, t - 1)


def attn_fwd(proj, fq, fk, *, heads, name):
    T = proj.shape[0]
    H = heads
    tb = _tile(T, ATT_BLOCK)
    nb = T // tb
    assert nb % 2 == 0
    rep = tb // LANES

    def body(q_ref, k_ref, v_ref, fq_ref, fk_ref, o_ref, lse_ref, m_s, l_s, acc_s):
        i, j = _fold_q(pl.program_id(1), pl.program_id(2), nb)

        @pl.when(j == 0)
        def _():
            m_s[...] = jnp.full_like(m_s, NEG)
            l_s[...] = jnp.zeros_like(l_s)
            acc_s[...] = jnp.zeros_like(acc_s)

        def step(masked):
            s = _att_scores(q_ref[...], k_ref[...], fq_ref[...], fk_ref[...], rep, masked)
            m_prev = m_s[...]
            m_new = jnp.maximum(m_prev, jnp.max(s, axis=-1, keepdims=True))
            alpha = jnp.exp(m_prev - m_new)
            p = jnp.exp(s - jnp.tile(m_new, (1, rep)))
            l_s[...] = alpha * l_s[...] + jnp.sum(p, axis=-1, keepdims=True)
            v = v_ref[...]
            acc_s[...] = alpha * acc_s[...] + jnp.dot(p.astype(v.dtype), v, preferred_element_type=_F32)
            m_s[...] = m_new

        @pl.when(j < i)
        def _():
            step(False)

        @pl.when(j == i)
        def _():
            step(True)
            o_ref[...] = (acc_s[...] / l_s[...]).astype(o_ref.dtype)
            lse_ref[...] = m_s[...] + jnp.log(l_s[...])

    qi = lambda r, t: _fold_q(r, t, nb)[0]
    kj = lambda r, t: _fold_q(r, t, nb)[1]
    qs = pl.BlockSpec((tb, HEAD_DIM), lambda h, r, t: (qi(r, t), h))
    ks = pl.BlockSpec((tb, HEAD_DIM), lambda h, r, t: (kj(r, t), H + h))
    vs = pl.BlockSpec((tb, HEAD_DIM), lambda h, r, t: (kj(r, t), 2 * H + h))
    fqs = pl.BlockSpec((None, tb, LANES), lambda h, r, t: (h, qi(r, t), 0))
    fks = pl.BlockSpec((None, 1, tb), lambda h, r, t: (h, 0, kj(r, t)))
    return pl.pallas_call(
        body, name=name, grid=(H, nb // 2, nb + 1),
        in_specs=[qs, ks, vs, fqs, fks],
        out_specs=[qs, fqs],
        out_shape=[jax.ShapeDtypeStruct((T, H * HEAD_DIM), _MM), jax.ShapeDtypeStruct((H, T, LANES), _F32)],
        scratch_shapes=[pltpu.VMEM((tb, LANES), _F32), pltpu.VMEM((tb, LANES), _F32),
                        pltpu.VMEM((tb, HEAD_DIM), _F32)],
        compiler_params=_cparams(("parallel", "parallel", "arbitrary")))(proj, proj, proj, fq, fk)


def _att_p_ds(q, k, v, do, o, fq, fk, lse, rep, masked):
    s = _att_scores(q, k, fq, fk, rep, masked)
    p = jnp.exp(s - jnp.tile(lse, (1, rep)))
    delta = jnp.sum(do.astype(_F32) * o.astype(_F32), axis=-1, keepdims=True)
    dp = lax.dot_general(do, v, (((1,), (1,)), ((), ())), preferred_element_type=_F32)
    ds = p * (dp - delta)
    return p, ds


def attn_bwd(proj, dcat, attn, fq, fk, lse, *, heads, name):
    T = proj.shape[0]
    H = heads
    tb = _tile(T, ATT_BLOCK)
    nb = T // tb
    assert nb % 2 == 0
    rep = tb // LANES
    scale = HEAD_DIM ** -0.5

    def body(q_ref, k_ref, v_ref, do_ref, o_ref, fq_ref, fk_ref, lse_ref,
             dq_ref, dk_ref, dv_ref, dfq_ref, dfk_ref, dk_s, dv_s, dfk_s, dq_s, dfq_s):
        r, t = pl.program_id(1), pl.program_id(2)
        j, i = _fold_k(r, t, nb)

        @pl.when((r == 0) & (t == 0))
        def _():
            dq_s[...] = jnp.zeros_like(dq_s)
            dfq_s[...] = jnp.zeros_like(dfq_s)

        @pl.when(i == j)
        def _():
            dk_s[...] = jnp.zeros_like(dk_s)
            dv_s[...] = jnp.zeros_like(dv_s)
            dfk_s[...] = jnp.zeros_like(dfk_s)

        def step(masked):
            q = q_ref[...]
            k = k_ref[...]
            do = do_ref[...]
            p, ds = _att_p_ds(q, k, v_ref[...], do, o_ref[...], fq_ref[...], fk_ref[...],
                              lse_ref[...], rep, masked)
            dsm = ds.astype(q.dtype)
            tn = (((0,), (0,)), ((), ()))
            dv_s[...] += lax.dot_general(p.astype(do.dtype), do, tn, preferred_element_type=_F32)
            dk_s[...] += lax.dot_general(dsm, q, tn, preferred_element_type=_F32)
            dfk_s[0:1, :] += -_colsum(ds)
            rows = pl.ds(pl.multiple_of(i * tb, tb), tb)
            dq_s[rows, :] += jnp.dot(dsm, k, preferred_element_type=_F32)
            dfq_s[rows, :] += jnp.sum(ds, axis=-1, keepdims=True)

        @pl.when(i > j)
        def _():
            step(False)

        @pl.when(i == j)
        def _():
            step(True)

        @pl.when(i == nb - 1)
        def _():
            dk_ref[...] = (dk_s[...] * scale).astype(dk_ref.dtype)
            dv_ref[...] = dv_s[...].astype(dv_ref.dtype)
            dfk_ref[...] = dfk_s[0:1, :]

        @pl.when((r == nb // 2 - 1) & (t == nb))
        def _():
            dq_ref[...] = (dq_s[...] * scale).astype(dq_ref.dtype)
            dfq_ref[...] = dfq_s[...]

    kj = lambda r, t: _fold_k(r, t, nb)[0]
    qi = lambda r, t: _fold_k(r, t, nb)[1]
    qs = pl.BlockSpec((tb, HEAD_DIM), lambda h, r, t: (qi(r, t), h))
    ks = pl.BlockSpec((tb, HEAD_DIM), lambda h, r, t: (kj(r, t), H + h))
    vs = pl.BlockSpec((tb, HEAD_DIM), lambda h, r, t: (kj(r, t), 2 * H + h))
    stat = pl.BlockSpec((None, tb, LANES), lambda h, r, t: (h, qi(r, t), 0))
    fks = pl.BlockSpec((None, 1, tb), lambda h, r, t: (h, 0, kj(r, t)))
    kout = pl.BlockSpec((tb, HEAD_DIM), lambda h, r, t: (kj(r, t), h))
    head_q = pl.BlockSpec((T, HEAD_DIM), lambda h, r, t: (0, h))
    head_stat = pl.BlockSpec((None, T, LANES), lambda h, r, t: (h, 0, 0))
    A = H * HEAD_DIM
    return pl.pallas_call(
        body, name=name, grid=(H, nb // 2, nb + 1),
        in_specs=[qs, ks, vs, qs, qs, stat, fks, stat],
        out_specs=[head_q, kout, kout, head_stat, fks],
        out_shape=[jax.ShapeDtypeStruct((T, A), _MM), jax.ShapeDtypeStruct((T, A), _MM),
                   jax.ShapeDtypeStruct((T, A), _MM), jax.ShapeDtypeStruct((H, T, LANES), _F32),
                   jax.ShapeDtypeStruct((H, 1, T), _F32)],
        scratch_shapes=[pltpu.VMEM((tb, HEAD_DIM), _F32), pltpu.VMEM((tb, HEAD_DIM), _F32),
                        pltpu.VMEM((SUBLANES, tb), _F32), pltpu.VMEM((T, HEAD_DIM), _F32),
                        pltpu.VMEM((T, LANES), _F32)],
        compiler_params=_cparams(("parallel", "arbitrary", "arbitrary")))(
            proj, proj, proj, dcat, attn, fq, fk, lse)


def _causal_taps(w_ref, scr, halo, tt, width):
    acc = w_ref[width - 1:width, :] * scr[halo:halo + tt, :]
    for j in range(1, width):
        acc = acc + w_ref[width - 1 - j:width - j, :] * scr[halo - j:halo - j + tt, :]
    return acc


def _anticausal_taps(w_ref, scr, tt, width):
    acc = w_ref[width - 1:width, :] * scr[0:tt, :]
    for j in range(1, width):
        acc = acc + w_ref[width - 1 - j:width - j, :] * scr[j:j + tt, :]
    return acc


def _ln_fwd(cc, g, b):
    mu = jnp.mean(cc, axis=-1, keepdims=True)
    xc = cc - mu
    rstd = lax.rsqrt(jnp.mean(xc * xc, axis=-1, keepdims=True) + LN_EPS)
    xhat = xc * rstd
    return xhat, rstd, xhat * g + b


def _ln_silu_bwd(cc, dconf, g, b):
    xhat, rstd, ln = _ln_fwd(cc, g, b)
    s = _sigmoid(ln)
    dln = dconf * (s * (1.0 + ln * (1.0 - s)))
    dxh = dln * g
    dcc = rstd * (dxh - jnp.mean(dxh, axis=-1, keepdims=True)
                  - xhat * jnp.mean(dxh * xhat, axis=-1, keepdims=True))
    return dcc, dln, xhat


def mixer_misc_fwd(proj, cw, cb, lg, lb, sw, *, width, base_col, name):
    T = proj.shape[0]
    C = width
    tt = _tile(T, CONV_TILE)
    HB = CONV_HALO
    per = tt // HB
    KC, KS = cw.shape[0], sw.shape[0]
    b0 = base_col // C

    def body(cv, cg, sx, sb, sc, cvh, cgh, sxh, sch, cw_ref, cb_ref, lg_ref, lb_ref, sw_ref,
             cm_ref, cc_ref, zc_ref, gscr, zscr):
        keep = (pl.program_id(0) > 0).astype(_F32)
        f = lambda r: r[...].astype(_F32)
        gscr[0:HB, :] = f(cvh) * _sigmoid(f(cgh)) * keep
        gscr[HB:HB + tt, :] = f(cv) * _sigmoid(f(cg))
        cc = _causal_taps(cw_ref, gscr, HB, tt, KC) + cb_ref[...]
        cc_ref[...] = cc
        _, _, ln = _ln_fwd(cc, lg_ref[...], lb_ref[...])
        cm_ref[:, 0:C] = (ln * _sigmoid(ln)).astype(cm_ref.dtype)
        zscr[0:HB, :] = f(sch) * f(sxh) * keep
        zscr[HB:HB + tt, :] = f(sc) * f(sx)
        zc = _causal_taps(sw_ref, zscr, HB, tt, KS)
        zc_ref[...] = zc
        cm_ref[:, C:2 * C] = (f(sb) * zc).astype(cm_ref.dtype)

    main = lambda k: pl.BlockSpec((tt, C), lambda i: (i, b0 + k))
    halo = lambda k: pl.BlockSpec((HB, C), lambda i: (jnp.maximum(i * per - 1, 0), b0 + k))
    full = lambda a: pl.BlockSpec(a.shape, lambda i: (0, 0))
    return pl.pallas_call(
        body, name=name, grid=(T // tt,),
        in_specs=[main(0), main(1), main(2), main(3), main(4), halo(0), halo(1), halo(2), halo(4),
                  full(cw), full(cb), full(lg), full(lb), full(sw)],
        out_specs=[pl.BlockSpec((tt, 2 * C), lambda i: (i, 0)), pl.BlockSpec((tt, C), lambda i: (i, 0)),
                   pl.BlockSpec((tt, C), lambda i: (i, 0))],
        out_shape=[jax.ShapeDtypeStruct((T, 2 * C), _MM), jax.ShapeDtypeStruct((T, C), _F32),
                   jax.ShapeDtypeStruct((T, C), _F32)],
        scratch_shapes=[pltpu.VMEM((tt + HB, C), _F32), pltpu.VMEM((tt + HB, C), _F32)],
        compiler_params=_cparams(("parallel",)))(
            proj, proj, proj, proj, proj, proj, proj, proj, proj, cw, cb, lg, lb, sw)


def mixer_misc_bwd(proj, dcat, cc, zc, cw, lg, lb, sw, *, width, base_col, dbase_col, name):
    T = proj.shape[0]
    C = width
    tt = _tile(T, CONV_TILE)
    nt = T // tt
    HB = CONV_HALO
    per = tt // HB
    KC, KS = cw.shape[0], sw.shape[0]
    b0 = base_col // C
    d0 = dbase_col // C
    last_hb = T // HB - 1

    def body(cv, cg, sx, sb, sc, cvh, cgh, sxh, sch, sbn, dcf, dsv, dcfn, dsvn, cc_ref, ccn_ref, zc_ref,
             cw_ref, lg_ref, lb_ref, sw_ref,
             dm_ref, dcw_ref, dcb_ref, dlg_ref, dlb_ref, dsw_ref, gscr, dscr, zscr, zdscr):
        i = pl.program_id(0)
        keep = (i > 0).astype(_F32)
        ahead = (i < nt - 1).astype(_F32)
        f = lambda r: r[...].astype(_F32)

        @pl.when(i == 0)
        def _():
            for r in (dcw_ref, dcb_ref, dlg_ref, dlb_ref, dsw_ref):
                r[...] = jnp.zeros_like(r)

        g, b = lg_ref[...], lb_ref[...]
        dcc, dln, xhat = _ln_silu_bwd(cc_ref[...], f(dcf), g, b)
        dcc_next, _, _ = _ln_silu_bwd(ccn_ref[...], f(dcfn), g, b)
        dlg_ref[...] += _colsum(dln * xhat)
        dlb_ref[...] += _colsum(dln)
        dcb_ref[...] += _colsum(dcc)
        dscr[0:tt, :] = dcc
        dscr[tt:tt + HB, :] = dcc_next * ahead
        dglu = _anticausal_taps(cw_ref, dscr, tt, KC)
        cvv = f(cv)
        sig = _sigmoid(f(cg))
        dm_ref[:, 0:C] = (dglu * sig).astype(dm_ref.dtype)
        dm_ref[:, C:2 * C] = (dglu * cvv * sig * (1.0 - sig)).astype(dm_ref.dtype)
        gscr[0:HB, :] = f(cvh) * _sigmoid(f(cgh)) * keep
        gscr[HB:HB + tt, :] = cvv * sig
        for j in range(KC):
            dcw_ref[KC - 1 - j:KC - j, :] += _colsum(dcc * gscr[HB - j:HB - j + tt, :])
        dsc_out = f(dsv)
        sbv = f(sb)
        dzc = dsc_out * sbv
        zdscr[0:tt, :] = dzc
        zdscr[tt:tt + HB, :] = f(dsvn) * f(sbn) * ahead
        dz = _anticausal_taps(sw_ref, zdscr, tt, KS)
        sxv, scv = f(sx), f(sc)
        dm_ref[:, 2 * C:3 * C] = (dz * scv).astype(dm_ref.dtype)
        dm_ref[:, 3 * C:4 * C] = (dsc_out * zc_ref[...]).astype(dm_ref.dtype)
        dm_ref[:, 4 * C:5 * C] = (dz * sxv).astype(dm_ref.dtype)
        zscr[0:HB, :] = f(sch) * f(sxh) * keep
        zscr[HB:HB + tt, :] = scv * sxv
        for j in range(KS):
            dsw_ref[KS - 1 - j:KS - j, :] += _colsum(dzc * zscr[HB - j:HB - j + tt, :])

    main = lambda col: pl.BlockSpec((tt, C), lambda i: (i, col))
    prev = lambda col: pl.BlockSpec((HB, C), lambda i: (jnp.maximum(i * per - 1, 0), col))
    nxt = lambda col: pl.BlockSpec((HB, C), lambda i: (jnp.minimum((i + 1) * per, last_hb), col))
    full = lambda a: pl.BlockSpec(a.shape, lambda i: (0, 0))
    vec = pl.BlockSpec((1, C), lambda i: (0, 0))
    return pl.pallas_call(
        body, name=name, grid=(nt,),
        in_specs=[main(b0), main(b0 + 1), main(b0 + 2), main(b0 + 3), main(b0 + 4),
                  prev(b0), prev(b0 + 1), prev(b0 + 2), prev(b0 + 4), nxt(b0 + 3),
                  main(d0), main(d0 + 1), nxt(d0), nxt(d0 + 1),
                  main(0), nxt(0), main(0),
                  full(cw), full(lg), full(lb), full(sw)],
        out_specs=[pl.BlockSpec((tt, 5 * C), lambda i: (i, 0)), full(cw), vec, vec, vec, full(sw)],
        out_shape=[jax.ShapeDtypeStruct((T, 5 * C), _MM), jax.ShapeDtypeStruct(cw.shape, _F32),
                   jax.ShapeDtypeStruct((1, C), _F32), jax.ShapeDtypeStruct((1, C), _F32),
                   jax.ShapeDtypeStruct((1, C), _F32), jax.ShapeDtypeStruct(sw.shape, _F32)],
        scratch_shapes=[pltpu.VMEM((tt + HB, C), _F32)] * 4,
        compiler_params=_cparams(("arbitrary",)))(
            proj, proj, proj, proj, proj, proj, proj, proj, proj, proj,
            dcat, dcat, dcat, dcat, cc, cc, zc, cw, lg, lb, sw)


def _ffn_u(main_ref, halo_ref, w_ref, b_ref, scr, keep, tt, width):
    HB = FFN_HALO
    scr[0:HB, :] = halo_ref[...].astype(_F32) * keep
    scr[HB:HB + tt, :] = main_ref[...].astype(_F32)
    return _causal_taps(w_ref, scr, HB, tt, width) + b_ref[...]


def ffn_act_fwd(up, w, b, *, name):
    T, F2 = up.shape
    F = F2 // 2
    K = w.shape[0]
    tt = _tile(T, CONV_TILE)
    tc = _tile(F, 512)
    nb = F // tc
    per = tt // FFN_HALO

    def body(g_ref, v_ref, gh_ref, vh_ref, wg_ref, wv_ref, bg_ref, bv_ref, o_ref, gscr, vscr):
        keep = (pl.program_id(0) > 0).astype(_F32)
        ug = _ffn_u(g_ref, gh_ref, wg_ref, bg_ref, gscr, keep, tt, K)
        uv = _ffn_u(v_ref, vh_ref, wv_ref, bv_ref, vscr, keep, tt, K)
        o_ref[...] = (ug * _sigmoid(ug) * uv).astype(o_ref.dtype)

    main = lambda off: pl.BlockSpec((tt, tc), lambda i, j: (i, j + off))
    halo = lambda off: pl.BlockSpec((FFN_HALO, tc), lambda i, j: (jnp.maximum(i * per - 1, 0), j + off))
    wsp = lambda off: pl.BlockSpec((K, tc), lambda i, j: (0, j + off))
    bsp = lambda off: pl.BlockSpec((1, tc), lambda i, j: (0, j + off))
    return pl.pallas_call(
        body, name=name, grid=(T // tt, nb),
        in_specs=[main(0), main(nb), halo(0), halo(nb), wsp(0), wsp(nb), bsp(0), bsp(nb)],
        out_specs=pl.BlockSpec((tt, tc), lambda i, j: (i, j)),
        out_shape=jax.ShapeDtypeStruct((T, F), _MM),
        scratch_shapes=[pltpu.VMEM((tt + FFN_HALO, tc), _F32)] * 2,
        compiler_params=_cparams(("parallel", "parallel")))(up, up, up, up, w, w, b, b)


def ffn_bwd_du(up, dact, w, b, *, name):
    T, F2 = up.shape
    F = F2 // 2
    K = w.shape[0]
    tt = _tile(T, CONV_TILE)
    tc = _tile(F, 512)
    nb = F // tc
    per = tt // FFN_HALO
    HB = FFN_HALO

    def body(g_ref, v_ref, gh_ref, vh_ref, da_ref, wg_ref, wv_ref, bg_ref, bv_ref, du_ref, dwb_ref,
             gscr, vscr):
        i = pl.program_id(1)
        keep = (i > 0).astype(_F32)
        ug = _ffn_u(g_ref, gh_ref, wg_ref, bg_ref, gscr, keep, tt, K)
        uv = _ffn_u(v_ref, vh_ref, wv_ref, bv_ref, vscr, keep, tt, K)
        s = _sigmoid(ug)
        da = da_ref[...].astype(_F32)
        du_g = da * uv * s * (1.0 + ug * (1.0 - s))
        du_v = da * ug * s
        du_ref[0] = du_g.astype(du_ref.dtype)
        du_ref[1] = du_v.astype(du_ref.dtype)

        @pl.when(i == 0)
        def _():
            dwb_ref[...] = jnp.zeros_like(dwb_ref)

        for half, (du, scr) in enumerate(((du_g, gscr), (du_v, vscr))):
            for j in range(K):
                dwb_ref[half, K - 1 - j:K - j, :] += _colsum(du * scr[HB - j:HB - j + tt, :])
            dwb_ref[half, K:K + 1, :] += _colsum(du)

    main = lambda off: pl.BlockSpec((tt, tc), lambda j, i: (i, j + off))
    halo = lambda off: pl.BlockSpec((HB, tc), lambda j, i: (jnp.maximum(i * per - 1, 0), j + off))
    wsp = lambda off: pl.BlockSpec((K, tc), lambda j, i: (0, j + off))
    bsp = lambda off: pl.BlockSpec((1, tc), lambda j, i: (0, j + off))
    return pl.pallas_call(
        body, name=name, grid=(nb, T // tt),
        in_specs=[main(0), main(nb), halo(0), halo(nb), main(0), wsp(0), wsp(nb), bsp(0), bsp(nb)],
        out_specs=[pl.BlockSpec((2, tt, tc), lambda j, i: (0, i, j)),
                   pl.BlockSpec((2, SUBLANES, tc), lambda j, i: (0, 0, j))],
        out_shape=[jax.ShapeDtypeStruct((2, T, F), _MM), jax.ShapeDtypeStruct((2, SUBLANES, F), _F32)],
        scratch_shapes=[pltpu.VMEM((tt + HB, tc), _F32)] * 2,
        compiler_params=_cparams(("parallel", "arbitrary")))(up, up, up, up, dact, w, w, b, b)


def dwconv_transpose(du, w, *, name):
    _, T, F = du.shape
    K = w.shape[0]
    tt = _tile(T, CONV_TILE)
    tc = _tile(F, 512)
    nb = F // tc
    per = tt // FFN_HALO
    HB = FFN_HALO
    nt = T // tt
    last_hb = T // HB - 1

    def body(d_ref, dn_ref, w_ref, o_ref, scr):
        ahead = (pl.program_id(1) < nt - 1).astype(_F32)
        scr[0:tt, :] = d_ref[...].astype(_F32)
        scr[tt:tt + HB, :] = dn_ref[...].astype(_F32) * ahead
        o_ref[...] = _anticausal_taps(w_ref, scr, tt, K).astype(o_ref.dtype)

    return pl.pallas_call(
        body, name=name, grid=(2, nt, nb),
        in_specs=[pl.BlockSpec((None, tt, tc), lambda s, i, j: (s, i, j)),
                  pl.BlockSpec((None, HB, tc), lambda s, i, j: (s, jnp.minimum((i + 1) * per, last_hb), j)),
                  pl.BlockSpec((K, tc), lambda s, i, j: (0, s * nb + j))],
        out_specs=pl.BlockSpec((tt, tc), lambda s, i, j: (i, s * nb + j)),
        out_shape=jax.ShapeDtypeStruct((T, 2 * F), _MM),
        scratch_shapes=[pltpu.VMEM((tt + HB, tc), _F32)],
        compiler_params=_cparams(("parallel", "parallel", "parallel")))(du, du, w)


def _adamw_math(w, g, m, v):
    m = ADAM_B1 * m + (1.0 - ADAM_B1) * g
    v = ADAM_B2 * v + (1.0 - ADAM_B2) * (g * g)
    m_hat = m / (1.0 - ADAM_B1 ** ADAM_STEP)
    v_hat = v / (1.0 - ADAM_B2 ** ADAM_STEP)
    delta = -ADAM_LR * (m_hat / (jnp.sqrt(v_hat) + ADAM_EPS) + ADAM_WD * w)
    return delta, m, v


def _as2d(a):
    return a.reshape(1, -1) if a.ndim == 1 else a.reshape(-1, a.shape[-1])


def adamw(w, g, m, v, *, name):
    shape = w.shape
    w2, g2, m2, v2 = _as2d(w), _as2d(g), _as2d(m), _as2d(v)
    R, C = w2.shape
    lanes = -(-C // LANES) * LANES
    tr = _tile(R, max(SUBLANES, (1 << 20) // (4 * lanes)), SUBLANES)

    def body(w_ref, g_ref, m_ref, v_ref, d_ref, mo_ref, vo_ref):
        d, mn, vn = _adamw_math(w_ref[...], g_ref[...], m_ref[...], v_ref[...])
        d_ref[...] = d
        mo_ref[...] = mn
        vo_ref[...] = vn

    blk = pl.BlockSpec((tr, C), lambda i: (i, 0))
    outs = pl.pallas_call(
        body, name=name, grid=(R // tr,), in_specs=[blk] * 4, out_specs=[blk] * 3,
        out_shape=[jax.ShapeDtypeStruct((R, C), _F32)] * 3,
        compiler_params=_cparams(("parallel",)))(w2, g2, m2, v2)
    return tuple(o.reshape(shape) for o in outs)


def ada_w_update(c_t, d_ada, w, m, v, *, name):
    L, D, N = w.shape
    B = c_t.shape[1]
    tr = _tile(D, 256, SUBLANES)
    tn = _tile(N, 1024)

    def body(c_ref, a_ref, w_ref, m_ref, v_ref, g_ref, d_ref, mo_ref, vo_ref):
        cv = c_ref[...]
        cv = cv * _sigmoid(cv)
        g = jnp.dot(cv, a_ref[...], preferred_element_type=_F32, precision=HIGHEST)
        g_ref[...] = g
        d, mn, vn = _adamw_math(w_ref[...], g, m_ref[...], v_ref[...])
        d_ref[...] = d
        mo_ref[...] = mn
        vo_ref[...] = vn

    blk = pl.BlockSpec((None, tr, tn), lambda l, i, j: (l, i, j))
    return pl.pallas_call(
        body, name=name, grid=(L, D // tr, N // tn),
        in_specs=[pl.BlockSpec((tr, B), lambda l, i, j: (i, 0)),
                  pl.BlockSpec((None, B, tn), lambda l, i, j: (l, 0, j)), blk, blk, blk],
        out_specs=[blk] * 4,
        out_shape=[jax.ShapeDtypeStruct((L, D, N), _F32)] * 4,
        compiler_params=_cparams(("parallel", "parallel", "parallel")))(c_t, d_ada, w, m, v)


def _coords():
    return lax.axis_index("x"), lax.axis_index("y"), lax.axis_index("c")


def allgather_small(x, *, with_sum, name):
    R, C = x.shape

    def body(x_ref, out_ref, *rest):
        if with_sum:
            sum_ref, send_sems, recv_sems, local_sem = rest
        else:
            send_sems, recv_sems, local_sem = rest
        px, py, pc = _coords()
        me, sibling = (px, py, pc), (px, py, 1 - pc)
        chips = [(1 - px, py), (px, 1 - py), (1 - px, 1 - py)]

        def rows(qx, qy, qc):
            return out_ref.at[4 * qx + 2 * qy + qc]

        def copy(k, block, to, src=None):
            return pltpu.make_async_remote_copy(
                src_ref=rows(*block) if src is None else src, dst_ref=rows(*block),
                send_sem=send_sems.at[k], recv_sem=recv_sems.at[k], device_id=to, device_id_type=MESH)

        mine = pltpu.make_async_copy(x_ref, rows(*me), local_sem)
        mine.start()
        first = [copy(0, me, sibling, src=x_ref)]
        first += [copy(1 + j, me, (*chip, pc), src=x_ref) for j, chip in enumerate(chips)]
        for cp in first:
            cp.start()
        passed = [copy(4 + j, (*chip, pc), sibling) for j, chip in enumerate(chips)]
        for j, chip in enumerate(chips):
            copy(1 + j, (*chip, pc), me).wait_recv()
            passed[j].start()
        copy(0, sibling, me).wait_recv()
        for j, chip in enumerate(chips):
            copy(4 + j, (*chip, 1 - pc), me).wait_recv()
        for cp in first + passed:
            cp.wait_send()
        mine.wait()
        if with_sum:
            acc = out_ref[0]
            for k in range(1, N_DEV):
                acc = acc + out_ref[k]
            sum_ref[...] = acc

    vm = pl.BlockSpec(memory_space=pltpu.VMEM)
    out_shape = [jax.ShapeDtypeStruct((N_DEV, R, C), x.dtype)]
    if with_sum:
        out_shape.append(jax.ShapeDtypeStruct((R, C), x.dtype))
    outs = pl.pallas_call(
        body, name=name, in_specs=[vm], out_specs=[vm] * len(out_shape), out_shape=out_shape,
        scratch_shapes=[pltpu.SemaphoreType.DMA((7,)), pltpu.SemaphoreType.DMA((7,)), pltpu.SemaphoreType.DMA],
        compiler_params=pltpu.CompilerParams(vmem_limit_bytes=VMEM_LIMIT_V7X))(x)
    return outs if with_sum else outs[0]


def _at(start, size, align):
    return pl.ds(pl.multiple_of(start, align) if align > 1 else start, size)


class _BigLayout:
    def __init__(self, D, INs, Ds, F2s, Fs):
        self.D, self.INs, self.Ds, self.F2s, self.Fs = D, INs, Ds, F2s, Fs
        self.Dh, self.Dsh, self.Fsh = D // 2, Ds // 2, Fs // 2
        self.piece_shapes = [(self.Dh, INs), (self.Dsh, D), (self.Dh, F2s), (self.Fsh, D)]

    def in_full(self, a, ref, k, h):
        if a == 0:
            return ref.at[k, _at(h * self.Dh, self.Dh, self.Dh), :]
        if a == 1:
            return ref.at[_at(k * self.Ds + h * self.Dsh, self.Dsh, self.Dsh), :]
        if a == 2:
            return ref.at[_at(h * self.Dh, self.Dh, self.Dh), _at(k * self.F2s, self.F2s, self.F2s)]
        return ref.at[_at(k * self.Fs + h * self.Fsh, self.Fsh, self.Fsh), :]

    def in_shard(self, a, ref, h):
        rows = self.piece_shapes[a][0]
        return ref.at[_at(h * rows, rows, rows), :]


def gather_weights(shards, *, name):
    D, INs = shards[0].shape
    lay = _BigLayout(D, INs, shards[1].shape[0], shards[2].shape[1], shards[3].shape[0])
    n_arr = len(shards)

    def body(*refs):
        ins, outs = refs[:n_arr], refs[n_arr:2 * n_arr]
        send_sems, recv_sems, local_sems = refs[2 * n_arr:]
        px, py, pc = _coords()
        my_chip = 2 * px + py
        sibling = (px, py, 1 - pc)
        chips = [(1 - px, py), (px, 1 - py), (1 - px, 1 - py)]

        def copy(a, kk, k, h, to, src=None):
            dst = lay.in_full(a, outs[a], k, h)
            return pltpu.make_async_remote_copy(
                src_ref=dst if src is None else src, dst_ref=dst,
                send_sem=send_sems.at[7 * a + kk], recv_sem=recv_sems.at[7 * a + kk],
                device_id=to, device_id_type=MESH)

        sends, mine = [], []
        for a in range(n_arr):
            own = lay.in_shard(a, ins[a], pc)
            mine.append(pltpu.make_async_copy(own, lay.in_full(a, outs[a], my_chip, pc), local_sems.at[a]))
            sends.append(copy(a, 0, my_chip, pc, sibling, src=own))
            sends += [copy(a, 1 + j, my_chip, pc, (*chip, pc), src=own) for j, chip in enumerate(chips)]
        for cp in mine + sends:
            cp.start()
        for j, (qx, qy) in enumerate(chips):
            for a in range(n_arr):
                copy(a, 1 + j, 2 * qx + qy, pc, sibling).wait_recv()
                fwd = copy(a, 4 + j, 2 * qx + qy, pc, sibling)
                fwd.start()
                sends.append(fwd)
        for a in range(n_arr):
            copy(a, 0, my_chip, 1 - pc, sibling).wait_recv()
            for j, (qx, qy) in enumerate(chips):
                copy(a, 4 + j, 2 * qx + qy, 1 - pc, sibling).wait_recv()
        for cp in sends:
            cp.wait_send()
        for cp in mine:
            cp.wait()

    hbm = pl.BlockSpec(memory_space=pl.ANY)
    dt = shards[0].dtype
    out_shape = [jax.ShapeDtypeStruct((N_CHIP, D, INs), dt), jax.ShapeDtypeStruct((D, D), dt),
                 jax.ShapeDtypeStruct((D, lay.F2s * N_CHIP), dt), jax.ShapeDtypeStruct((lay.Fs * N_CHIP, D), dt)]
    return pl.pallas_call(
        body, name=name, in_specs=[hbm] * n_arr, out_specs=[hbm] * n_arr, out_shape=out_shape,
        scratch_shapes=[pltpu.SemaphoreType.DMA((7 * n_arr,)), pltpu.SemaphoreType.DMA((7 * n_arr,)),
                        pltpu.SemaphoreType.DMA((n_arr,))],
    )(*shards)


def scatter_grads(partials, *, name):
    _, D, INs = partials[0].shape
    lay = _BigLayout(D, INs, partials[1].shape[0] // N_CHIP, partials[2].shape[1] // N_CHIP,
                     partials[3].shape[0] // N_CHIP)
    n_arr = len(partials)

    def body(*refs):
        ins, outs = refs[:n_arr], refs[n_arr:2 * n_arr]
        send_sems, recv_sems, local_sems = refs[2 * n_arr:]
        px, py, pc = _coords()
        me = 4 * px + 2 * py + pc
        copies, mine = [], []
        for a in range(n_arr):
            mine.append(pltpu.make_async_copy(lay.in_full(a, ins[a], 2 * px + py, pc), outs[a].at[me],
                                              local_sems.at[a]))
            for mask in range(1, N_DEV):
                qx = 1 - px if (mask >> 2) & 1 else px
                qy = 1 - py if (mask >> 1) & 1 else py
                qc = 1 - pc if mask & 1 else pc
                copies.append(pltpu.make_async_remote_copy(
                    src_ref=lay.in_full(a, ins[a], 2 * qx + qy, qc), dst_ref=outs[a].at[me],
                    send_sem=send_sems.at[7 * a + mask - 1], recv_sem=recv_sems.at[7 * a + mask - 1],
                    device_id=(qx, qy, qc), device_id_type=MESH))
        for cp in mine + copies:
            cp.start()
        for cp in copies:
            cp.wait_recv()
        for cp in copies:
            cp.wait_send()
        for cp in mine:
            cp.wait()

    hbm = pl.BlockSpec(memory_space=pl.ANY)
    dt = partials[0].dtype
    return pl.pallas_call(
        body, name=name, in_specs=[hbm] * n_arr, out_specs=[hbm] * n_arr,
        out_shape=[jax.ShapeDtypeStruct((N_DEV, *s), dt) for s in lay.piece_shapes],
        scratch_shapes=[pltpu.SemaphoreType.DMA((7 * n_arr,)), pltpu.SemaphoreType.DMA((7 * n_arr,)),
                        pltpu.SemaphoreType.DMA((n_arr,))],
    )(*partials)


SIBLING_CHUNKS = 4


def sibling_exchange(bufs, *, name):
    n_arr = len(bufs)
    n_ch = [max(n for n in (SIBLING_CHUNKS, 2, 1) if x.shape[1] % (n * SUBLANES) == 0 or n == 1) for x in bufs]
    offs = [sum(n_ch[:a]) for a in range(n_arr)]

    def body(*refs):
        outs = refs[n_arr:2 * n_arr]
        send_sems, recv_sems = refs[2 * n_arr:]
        px, py, pc = _coords()
        copies = []
        for a in range(n_arr):
            rows = bufs[a].shape[1] // n_ch[a]
            for q in range(n_ch[a]):
                mine = outs[a].at[pc, pl.ds(q * rows, rows), :]
                copies.append(pltpu.make_async_remote_copy(
                    src_ref=mine, dst_ref=mine,
                    send_sem=send_sems.at[offs[a] + q], recv_sem=recv_sems.at[offs[a] + q],
                    device_id=(px, py, 1 - pc), device_id_type=MESH))
        for cp in copies:
            cp.start()
        for cp in copies:
            cp.wait_recv()
        for cp in copies:
            cp.wait_send()

    hbm = pl.BlockSpec(memory_space=pl.ANY)
    return pl.pallas_call(
        body, name=name, in_specs=[hbm] * n_arr, out_specs=[hbm] * n_arr,
        out_shape=[jax.ShapeDtypeStruct(x.shape, x.dtype) for x in bufs],
        input_output_aliases={a: a for a in range(n_arr)},
        scratch_shapes=[pltpu.SemaphoreType.DMA((sum(n_ch),))] * 2,
    )(*bufs)


def sum_slots(x, core, *, name):
    n, R, C = x.shape
    lanes = -(-C // LANES) * LANES
    tr = _tile(R, max(BF16_ROWS, (4 << 20) // (n * 2 * lanes)), BF16_ROWS)

    def body(core_ref, x_ref, o_ref):
        acc = x_ref[0].astype(_F32)
        for k in range(1, n):
            acc = acc + x_ref[k].astype(_F32)
        o_ref[...] = acc

    return pl.pallas_call(
        body, name=name,
        grid_spec=pltpu.PrefetchScalarGridSpec(
            num_scalar_prefetch=1, grid=(R // tr,),
            in_specs=[pl.BlockSpec((n, tr, C), lambda i, core_ref: (0, i, 0))],
            out_specs=pl.BlockSpec((None, tr, C), lambda i, core_ref: (core_ref[0], i, 0))),
        out_shape=jax.ShapeDtypeStruct((2, R, C), _F32),
        compiler_params=_cparams(("parallel",)))(core, x)


def _pack_flat(arrays, quantum):
    flat = jnp.concatenate([a.reshape(-1) for a in arrays])
    pad = (-flat.shape[0]) % quantum
    return jnp.pad(flat, (0, pad)) if pad else flat


def _unpack_flat(flat, shapes):
    out, off = [], 0
    for s in shapes:
        n = math.prod(s)
        out.append(flat[off:off + n].reshape(s))
        off += n
    return out


def _small_pack(arrays):
    return _pack_flat([a.astype(_F32) for a in arrays], SUBLANES * LANES).reshape(-1, LANES)


def kernel(x, c, ada_w, ada_b, mix_norm_g, w_in, b_forget, conf_dw_w, conf_dw_b, conf_ln_g, conf_ln_b, sc_dw_w, w_out, ffn_norm_g, w_up, ffn_dw_w, ffn_dw_b, w_down, final_norm_g, loss_target, m_ada_w, m_ada_b, m_mix_norm_g, m_w_in, m_b_forget, m_conf_dw_w, m_conf_dw_b, m_conf_ln_g, m_conf_ln_b, m_sc_dw_w, m_w_out, m_ffn_norm_g, m_w_up, m_ffn_dw_w, m_ffn_dw_b, m_w_down, m_final_norm_g, v_ada_w, v_ada_b, v_mix_norm_g, v_w_in, v_b_forget, v_conf_dw_w, v_conf_dw_b, v_conf_ln_g, v_conf_ln_b, v_sc_dw_w, v_w_out, v_ffn_norm_g, v_w_up, v_ffn_dw_w, v_ffn_dw_b, v_w_down, v_final_norm_g):
    _, T, D = x.shape
    L = ada_w.shape[0]
    A = D // 2
    H = A // HEAD_DIM
    C = D // 4
    assert D - A - C == C
    IN = 3 * A + H + 5 * C
    NM = 3 * A + 5 * C
    NP = NM + LANES
    F2 = w_up.shape[2] * N_CHIP
    F = F2 // 2
    NA = ada_w.shape[2]
    assert NA * N_CHIP == 6 * D and w_in.shape[2] * N_CHIP == IN

    px, py, pc = _coords()
    chip = 2 * px + py
    me = 2 * chip + pc

    x0 = x[0]
    tgt = loss_target[0]

    c_all = allgather_small(c.reshape(-1, LANES), with_sum=False, name="gather_c").reshape(N_DEV, D)
    parts = [matmul(c_all, ada_w[l], out_dtype=_F32, name="ada_fwd", tm=N_DEV, tn=512, tk=D,
                    a_silu=True, precision=HIGHEST) for l in range(L)]
    parts = jnp.stack(parts)
    got = allgather_small(parts.reshape(-1, LANES), with_sum=False, name="gather_ada")
    got = got.reshape(N_DEV, L, N_DEV, NA)[0::2]
    mine = lax.dynamic_index_in_dim(got, me, axis=2, keepdims=False)
    ada = jnp.transpose(mine, (1, 0, 2)).reshape(L, 6 * D) + ada_b

    INs = IN // N_CHIP
    wp_l, wout_l, wup_l, wdown_l = [], [], [], []
    for l in range(L):
        wi4, wo, wu, wd = gather_weights(
            [w_in[l].astype(_MM), w_out[l].astype(_MM), w_up[l].astype(_MM), w_down[l].astype(_MM)],
            name="gather_weights")
        wi = jnp.transpose(wi4, (1, 0, 2)).reshape(D, IN)
        wp_l.append(jnp.concatenate(
            [wi[:, :3 * A], wi[:, 3 * A + H:], jnp.pad(wi[:, 3 * A:3 * A + H], ((0, 0), (0, LANES - H)))], axis=1))
        wout_l.append(wo)
        wup_l.append(wu)
        wdown_l.append(wd)

    small_w = _small_pack([conf_dw_w, sc_dw_w, ffn_dw_w])
    sw_all = allgather_small(small_w, with_sum=False, name="gather_small_w")[0::2].reshape(N_CHIP, -1)
    sw_shapes = [conf_dw_w.shape, sc_dw_w.shape, ffn_dw_w.shape]
    sw_parts = [_unpack_flat(sw_all[k], sw_shapes) for k in range(N_CHIP)]
    conf_w_full = jnp.concatenate([p[0] for p in sw_parts], axis=-1)
    sc_w_full = jnp.concatenate([p[1] for p in sw_parts], axis=-1)
    ffn_w_full = jnp.concatenate([p[2] for p in sw_parts], axis=-1)
    bf_pad = jnp.pad(b_forget, ((0, 0), (0, LANES - H)))

    row = lambda a: a.reshape(1, -1)

    saved = []
    x_cur, branch, gate = x0, None, None
    for l in range(L):
        sh_m, sc_m, g_m, sh_f, sc_f, g_f = [row(ada[l, k * D:(k + 1) * D]) for k in range(6)]
        a1 = row(mix_norm_g[l]) * (1.0 + sc_m)
        a2 = row(ffn_norm_g[l]) * (1.0 + sc_f)
        x_in, h1 = resid_norm_fwd(x_cur, a1, sh_m, branch, gate, name="norm_mix_fwd")
        wp = wp_l[l]
        proj = matmul(h1, wp, out_dtype=_MM, name="proj_fwd", tn=512, tk=D, b_cols=(0, NM))
        flog = matmul(h1, wp, out_dtype=_F32, name="fgate_logits", tn=LANES, tk=D, b_cols=(NM, LANES))
        bf = row(bf_pad[l])
        fcum = fgate_fwd(flog, bf, name="fgate_fwd")
        f_t = fcum[:, :H].T
        fq = jnp.broadcast_to(f_t[:, :, None], (H, T, LANES))
        fk = f_t[:, None, :]
        attn, lse = attn_fwd(proj, fq, fk, heads=H, name="attn_fwd")
        cw, cb = conf_w_full[l], row(conf_dw_b[l])
        lg, lb, sw = row(conf_ln_g[l]), row(conf_ln_b[l]), sc_w_full[l]
        cm, cc, zc = mixer_misc_fwd(proj, cw, cb, lg, lb, sw, width=C, base_col=3 * A, name="misc_fwd")
        cat = jnp.concatenate([attn, cm], axis=1)
        mixed = matmul(cat, wout_l[l], out_dtype=_F32, name="wout_fwd", tn=512, tk=D)
        x_mid, h2 = resid_norm_fwd(x_in, a2, sh_f, mixed, g_m, name="norm_ffn_fwd")
        up = matmul(h2, wup_l[l], out_dtype=_MM, name="wup_fwd", tn=512, tk=D)
        fw, fb = ffn_w_full[l], row(ffn_dw_b[l])
        act = ffn_act_fwd(up, fw, fb, name="ffn_act_fwd")
        dn = matmul(act, wdown_l[l], out_dtype=_F32, name="wdown_fwd", tk=1408)
        saved.append(dict(x_in=x_in, h1=h1, proj=proj, flog=flog, fq=fq, fk=fk, attn=attn, lse=lse, cc=cc, zc=zc,
                          cat=cat, mixed=mixed, x_mid=x_mid, h2=h2, up=up, act=act, dn=dn, a1=a1, a2=a2,
                          g_m=g_m, g_f=g_f, sc_m=sc_m, sc_f=sc_f, bf=bf, cw=cw, lg=lg, lb=lb, sw=sw, fw=fw, fb=fb))
        x_cur, branch, gate = x_mid, dn, g_f

    dx, loss_row, d_final_g = final_loss_bwd(x_cur, branch, gate, row(final_norm_g), tgt, name="loss_bwd")

    KF = ffn_dw_w.shape[1]
    g_big = [None] * L
    d_ada, d_g1, d_g2, d_bf, d_cw, d_cb, d_lg, d_lb, d_sw, d_fw, d_fb = ([None] * L for _ in range(11))
    for l in reversed(range(L)):
        s = saved[l]
        ddn, dg_f = gate_bwd(dx, s["dn"], s["g_f"], name="gate_ffn_bwd")
        dact = matmul(ddn, wdown_l[l], out_dtype=_MM, name="wdown_dgrad", trans_b=True, tn=512, tk=D)
        gw_down = matmul(s["act"], ddn, out_dtype=_MM, name="wdown_wgrad", trans_a=True, tm=512)
        du, dwb = ffn_bwd_du(s["up"], dact, s["fw"], s["fb"], name="ffn_bwd_du")
        dup = dwconv_transpose(du, s["fw"], name="ffn_bwd_dup")
        dh2 = matmul(dup, wup_l[l], out_dtype=_MM, name="wup_dgrad", trans_b=True)
        gw_up = matmul(s["h2"], dup, out_dtype=_MM, name="wup_wgrad", trans_a=True)
        dx_mid, dsh_f, da2 = norm_bwd(s["x_mid"], dh2, dx, s["a2"], name="norm_ffn_bwd")
        dmixed, dg_m = gate_bwd(dx_mid, s["mixed"], s["g_m"], name="gate_mix_bwd")
        dcat = matmul(dmixed, wout_l[l], out_dtype=_MM, name="wout_dgrad", trans_b=True, tn=512, tk=D)
        gw_out = matmul(s["cat"], dmixed, out_dtype=_MM, name="wout_wgrad", trans_a=True)
        dq, dk, dv, dfq, dfk = attn_bwd(s["proj"], dcat, s["attn"], s["fq"], s["fk"], s["lse"], heads=H,
                                        name="attn_bwd")
        dmisc, d_cw[l], d_cb[l], d_lg[l], d_lb[l], d_sw[l] = mixer_misc_bwd(
            s["proj"], dcat, s["cc"], s["zc"], s["cw"], s["lg"], s["lb"], s["sw"],
            width=C, base_col=3 * A, dbase_col=A, name="misc_bwd")
        dfk_pad = jnp.pad((dfk[:, 0, :] + dfq[:, :, 0]).T, ((0, 0), (0, LANES - H)))
        dflog, dbf = fgate_bwd(dfk_pad, s["flog"], s["bf"], name="fgate_bwd")
        dproj = jnp.concatenate([dq, dk, dv, dmisc, dflog], axis=1)
        dh1 = matmul(dproj, wp_l[l], out_dtype=_MM, name="proj_dgrad", trans_b=True, tk=640)
        gwp = matmul(s["h1"], dproj, out_dtype=_MM, name="proj_wgrad", trans_a=True, tn=640)
        dx, dsh_m, da1 = norm_bwd(s["x_in"], dh1, dx_mid, s["a1"], name="norm_mix_bwd")

        g1, g2 = row(mix_norm_g[l]), row(ffn_norm_g[l])
        d_ada[l] = jnp.concatenate([dsh_m, da1 * g1, dg_m, dsh_f, da2 * g2, dg_f], axis=1)[0]
        d_g1[l] = (da1 * (1.0 + s["sc_m"]))[0]
        d_g2[l] = (da2 * (1.0 + s["sc_f"]))[0]
        d_bf[l] = dbf[0, :H]
        d_fw[l] = jnp.concatenate([dwb[0, :KF], dwb[1, :KF]], axis=1)
        d_fb[l] = jnp.concatenate([dwb[0, KF], dwb[1, KF]])
        gw_in = jnp.concatenate([gwp[:, :3 * A], gwp[:, NM:NM + H], gwp[:, 3 * A:NM]], axis=1)
        g_big[l] = (gw_in, gw_out, gw_up, gw_down)

    small = [loss_row[0], jnp.stack(d_g1), jnp.stack(d_bf), jnp.stack(d_cw), jnp.stack(d_cb)[:, 0],
             jnp.stack(d_lg)[:, 0], jnp.stack(d_lb)[:, 0], jnp.stack(d_sw), jnp.stack(d_g2), jnp.stack(d_fw),
             jnp.stack(d_fb), d_final_g[0], jnp.stack(d_ada)]
    small_shapes = [a.shape for a in small]
    sm_all, sm_sum = allgather_small(_small_pack(small), with_sum=True, name="reduce_small")
    (loss_v, g_mix_norm, g_bf, g_cw_full, g_cb, g_lg, g_lb, g_sw_full, g_ffn_norm, g_fw_full, g_fb, g_final,
     g_ada_b) = _unpack_flat(sm_sum.reshape(-1), small_shapes)
    loss = loss_v[0]
    n_ada = L * 6 * D
    off_ada = sum(math.prod(sh) for sh in small_shapes[:-1])
    d_ada_all = sm_all.reshape(N_DEV, -1)[:, off_ada:off_ada + n_ada].reshape(N_DEV, L, 6 * D)
    d_ada_chip = lax.dynamic_slice_in_dim(d_ada_all, chip * NA, NA, axis=2)
    d_ada_chip = jnp.transpose(d_ada_chip, (1, 0, 2))
    cshard = lambda a: lax.dynamic_slice_in_dim(a, chip * (a.shape[-1] // N_CHIP), a.shape[-1] // N_CHIP,
                                                axis=a.ndim - 1)
    g_conf_dw_w, g_sc_dw_w, g_ffn_dw_w = cshard(g_cw_full), cshard(g_sw_full), cshard(g_fw_full)

    g_in_l, g_out_l, g_up_l, g_down_l = [], [], [], []
    core = jnp.reshape(pc, (1,)).astype(jnp.int32)
    for l in range(L):
        gw_in, gw_out, gw_up, gw_down = g_big[l]
        gw_in4 = jnp.transpose(gw_in.reshape(D, N_CHIP, INs), (1, 0, 2))
        recv = scatter_grads([gw_in4, gw_out, gw_up, gw_down], name="scatter_grads")
        red = [sum_slots(r, core, name="sum_grads") for r in recv]
        gi, go, gu, gd = sibling_exchange(red, name="sibling_grads")
        g_in_l.append(gi.reshape(w_in.shape[1:])); g_out_l.append(go.reshape(w_out.shape[1:]))
        g_up_l.append(gu.reshape(w_up.shape[1:])); g_down_l.append(gd.reshape(w_down.shape[1:]))
    g_w_in, g_w_out, g_w_up, g_w_down = (jnp.stack(t) for t in (g_in_l, g_out_l, g_up_l, g_down_l))

    g_ada_w, dl_ada_w, nm_ada_w, nv_ada_w = ada_w_update(c_all.T, d_ada_chip, ada_w, m_ada_w, v_ada_w,
                                                          name="ada_w_update")

    grads = dict(ada_b=g_ada_b, mix_norm_g=g_mix_norm, w_in=g_w_in, b_forget=g_bf, conf_dw_w=g_conf_dw_w,
                 conf_dw_b=g_cb, conf_ln_g=g_lg, conf_ln_b=g_lb, sc_dw_w=g_sc_dw_w, w_out=g_w_out,
                 ffn_norm_g=g_ffn_norm, w_up=g_w_up, ffn_dw_w=g_ffn_dw_w, ffn_dw_b=g_fb, w_down=g_w_down,
                 final_norm_g=g_final)
    weights = dict(ada_b=(ada_b, m_ada_b, v_ada_b), mix_norm_g=(mix_norm_g, m_mix_norm_g, v_mix_norm_g),
                   w_in=(w_in, m_w_in, v_w_in), b_forget=(b_forget, m_b_forget, v_b_forget),
                   conf_dw_w=(conf_dw_w, m_conf_dw_w, v_conf_dw_w), conf_dw_b=(conf_dw_b, m_conf_dw_b, v_conf_dw_b),
                   conf_ln_g=(conf_ln_g, m_conf_ln_g, v_conf_ln_g), conf_ln_b=(conf_ln_b, m_conf_ln_b, v_conf_ln_b),
                   sc_dw_w=(sc_dw_w, m_sc_dw_w, v_sc_dw_w), w_out=(w_out, m_w_out, v_w_out),
                   ffn_norm_g=(ffn_norm_g, m_ffn_norm_g, v_ffn_norm_g), w_up=(w_up, m_w_up, v_w_up),
                   ffn_dw_w=(ffn_dw_w, m_ffn_dw_w, v_ffn_dw_w), ffn_dw_b=(ffn_dw_b, m_ffn_dw_b, v_ffn_dw_b),
                   w_down=(w_down, m_w_down, v_w_down), final_norm_g=(final_norm_g, m_final_norm_g, v_final_norm_g))
    order = ["ada_w", "ada_b", "mix_norm_g", "w_in", "b_forget", "conf_dw_w", "conf_dw_b", "conf_ln_g", "conf_ln_b",
             "sc_dw_w", "w_out", "ffn_norm_g", "w_up", "ffn_dw_w", "ffn_dw_b", "w_down", "final_norm_g"]
    g_out, d_out, m_out, v_out = {}, {}, {}, {}
    g_out["ada_w"], d_out["ada_w"], m_out["ada_w"], v_out["ada_w"] = g_ada_w, dl_ada_w, nm_ada_w, nv_ada_w
    for n in order[1:]:
        w, m, v = weights[n]
        g = grads[n].reshape(w.shape)
        g_out[n] = g
        d_out[n], m_out[n], v_out[n] = adamw(w, g, m, v, name="adamw_" + n)

    return (loss, dx[None], *[g_out[n] for n in order], *[d_out[n] for n in order],
            *[m_out[n] for n in order], *[v_out[n] for n in order])
```

```python
import functools
import math

import jax
import jax.numpy as jnp
from jax import lax
from jax.experimental import pallas as pl
from jax.experimental.pallas import tpu as pltpu

_MM = jnp.bfloat16
_F32 = jnp.float32
VMEM_LIMIT_V7X = 48 * 1024 * 1024
LANES = 128
SUBLANES = 8
BF16_ROWS = 16
HEAD_DIM = 128
RMS_EPS = 1e-6
LN_EPS = 1e-5
NEG = -1e30
HIGHEST = lax.Precision.HIGHEST

ADAM_LR = 0.001
ADAM_B1 = 0.9
ADAM_B2 = 0.999
ADAM_EPS = 1e-08
ADAM_WD = 0.01
ADAM_STEP = 10

N_DEV = 8
N_CHIP = 4

ATT_BLOCK = 512
CONV_TILE = 512
CONV_HALO = 32
FFN_HALO = BF16_ROWS
NORM_TILE = 256
MM_TM = 1024
MM_TN = 1024
MM_TK = 1024

MESH = pl.DeviceIdType.MESH


def _tile(dim, pref, mult=LANES):
    t = (min(pref, dim) // mult) * mult
    while t >= mult:
        if dim % t == 0:
            return t
        t -= mult
    return dim


def _cparams(sem):
    return pltpu.CompilerParams(dimension_semantics=sem, vmem_limit_bytes=VMEM_LIMIT_V7X)


def _sigmoid(x):
    return 1.0 / (1.0 + jnp.exp(-x))


def _colsum(x):
    return jnp.sum(x, axis=0, keepdims=True)


def matmul(a, b, *, out_dtype, name, trans_a=False, trans_b=False, tm=MM_TM, tn=MM_TN, tk=MM_TK,
           a_silu=False, precision=None, b_cols=None):
    M, K = (a.shape[1], a.shape[0]) if trans_a else a.shape
    N = b.shape[0] if trans_b else b.shape[1]
    assert (b.shape[1] if trans_b else b.shape[0]) == K
    col0 = 0
    if b_cols is not None:
        assert not trans_b
        col0, N = b_cols
    tm = _tile(M, tm, SUBLANES if (M % LANES) else LANES)
    tn = _tile(math.gcd(N, col0) if col0 else N, tn)
    tk = _tile(K, tk)
    nk = K // tk
    jb = col0 // tn
    dims = (((0 if trans_a else 1,), (1 if trans_b else 0,)), ((), ()))

    def body(a_ref, b_ref, o_ref, *scratch):
        av = a_ref[...]
        if a_silu:
            av = av * _sigmoid(av)
        part = lax.dot_general(av, b_ref[...], dims, preferred_element_type=_F32, precision=precision)
        if nk == 1:
            o_ref[...] = part.astype(o_ref.dtype)
        else:
            acc_ref, = scratch
            k = pl.program_id(2)

            @pl.when(k == 0)
            def _():
                acc_ref[...] = part

            @pl.when(k > 0)
            def _():
                acc_ref[...] += part

            @pl.when(k == nk - 1)
            def _():
                o_ref[...] = acc_ref[...].astype(o_ref.dtype)

    a_spec = (pl.BlockSpec((tk, tm), lambda i, j, k: (k, i)) if trans_a
              else pl.BlockSpec((tm, tk), lambda i, j, k: (i, k)))
    b_spec = (pl.BlockSpec((tn, tk), lambda i, j, k: (j, k)) if trans_b
              else pl.BlockSpec((tk, tn), lambda i, j, k: (k, j + jb)))
    return pl.pallas_call(
        body, name=name, grid=(M // tm, N // tn, nk),
        in_specs=[a_spec, b_spec],
        out_specs=pl.BlockSpec((tm, tn), lambda i, j, k: (i, j)),
        out_shape=jax.ShapeDtypeStruct((M, N), out_dtype),
        scratch_shapes=[pltpu.VMEM((tm, tn), _F32)] if nk > 1 else [],
        compiler_params=_cparams(("parallel", "parallel", "arbitrary")),
    )(a, b)


def resid_norm_fwd(x, a, sh, branch=None, gate=None, *, name):
    T, D = x.shape
    tm = _tile(T, NORM_TILE, BF16_ROWS)
    has_res = branch is not None

    def body(*refs):
        if has_res:
            x_ref, br_ref, g_ref, a_ref, sh_ref, xo_ref, h_ref = refs
            xv = x_ref[...] + g_ref[...] * br_ref[...]
            xo_ref[...] = xv
        else:
            x_ref, a_ref, sh_ref, h_ref = refs
            xv = x_ref[...]
        r = lax.rsqrt(jnp.mean(xv * xv, axis=-1, keepdims=True) + RMS_EPS)
        h_ref[...] = (xv * r * a_ref[...] + sh_ref[...]).astype(h_ref.dtype)

    row = pl.BlockSpec((tm, D), lambda i: (i, 0))
    vec = pl.BlockSpec((1, D), lambda i: (0, 0))
    if has_res:
        xo, h = pl.pallas_call(
            body, name=name, grid=(T // tm,), in_specs=[row, row, vec, vec, vec], out_specs=[row, row],
            out_shape=[jax.ShapeDtypeStruct((T, D), _F32), jax.ShapeDtypeStruct((T, D), _MM)],
            compiler_params=_cparams(("parallel",)))(x, branch, gate, a, sh)
        return xo, h
    h = pl.pallas_call(
        body, name=name, grid=(T // tm,), in_specs=[row, vec, vec], out_specs=row,
        out_shape=jax.ShapeDtypeStruct((T, D), _MM),
        compiler_params=_cparams(("parallel",)))(x, a, sh)
    return x, h


def norm_bwd(x, dh, dx_in, a, *, name):
    T, D = x.shape
    tm = _tile(T, NORM_TILE, BF16_ROWS)

    def body(x_ref, dh_ref, dxi_ref, a_ref, dxo_ref, dsh_ref, da_ref):
        i = pl.program_id(0)
        xv = x_ref[...]
        r = lax.rsqrt(jnp.mean(xv * xv, axis=-1, keepdims=True) + RMS_EPS)
        n = xv * r
        dhv = dh_ref[...].astype(_F32)
        dn = dhv * a_ref[...]
        dxo_ref[...] = dxi_ref[...] + r * (dn - n * jnp.mean(dn * n, axis=-1, keepdims=True))

        @pl.when(i == 0)
        def _():
            dsh_ref[...] = jnp.zeros_like(dsh_ref)
            da_ref[...] = jnp.zeros_like(da_ref)

        dsh_ref[...] += _colsum(dhv)
        da_ref[...] += _colsum(dhv * n)

    row = pl.BlockSpec((tm, D), lambda i: (i, 0))
    vec = pl.BlockSpec((1, D), lambda i: (0, 0))
    return pl.pallas_call(
        body, name=name, grid=(T // tm,), in_specs=[row, row, row, vec], out_specs=[row, vec, vec],
        out_shape=[jax.ShapeDtypeStruct((T, D), _F32), jax.ShapeDtypeStruct((1, D), _F32),
                   jax.ShapeDtypeStruct((1, D), _F32)],
        compiler_params=_cparams(("arbitrary",)))(x, dh, dx_in, a)


def gate_bwd(dx, branch, gate, *, name):
    T, D = dx.shape
    tm = _tile(T, NORM_TILE, BF16_ROWS)

    def body(dx_ref, br_ref, g_ref, db_ref, dg_ref):
        i = pl.program_id(0)
        dxv = dx_ref[...]
        db_ref[...] = (dxv * g_ref[...]).astype(db_ref.dtype)

        @pl.when(i == 0)
        def _():
            dg_ref[...] = jnp.zeros_like(dg_ref)

        dg_ref[...] += _colsum(dxv * br_ref[...])

    row = pl.BlockSpec((tm, D), lambda i: (i, 0))
    vec = pl.BlockSpec((1, D), lambda i: (0, 0))
    return pl.pallas_call(
        body, name=name, grid=(T // tm,), in_specs=[row, row, vec], out_specs=[row, vec],
        out_shape=[jax.ShapeDtypeStruct((T, D), _MM), jax.ShapeDtypeStruct((1, D), _F32)],
        compiler_params=_cparams(("arbitrary",)))(dx, branch, gate)


def final_loss_bwd(x, branch, gate, gfin, tgt, *, name):
    T, D = x.shape
    tm = _tile(T, NORM_TILE, BF16_ROWS)

    def body(x_ref, br_ref, g_ref, gf_ref, t_ref, dx_ref, loss_ref, dgf_ref):
        i = pl.program_id(0)
        xv = x_ref[...] + g_ref[...] * br_ref[...]
        r = lax.rsqrt(jnp.mean(xv * xv, axis=-1, keepdims=True) + RMS_EPS)
        n = xv * r
        e = n * gf_ref[...] - t_ref[...]
        dy = e * (1.0 / D)
        dn = dy * gf_ref[...]
        dx_ref[...] = r * (dn - n * jnp.mean(dn * n, axis=-1, keepdims=True))

        @pl.when(i == 0)
        def _():
            loss_ref[...] = jnp.zeros_like(loss_ref)
            dgf_ref[...] = jnp.zeros_like(dgf_ref)

        per_row = jnp.mean(e * e, axis=-1, keepdims=True)
        loss_ref[...] += jnp.broadcast_to(0.5 * _colsum(per_row), loss_ref.shape)
        dgf_ref[...] += _colsum(dy * n)

    row = pl.BlockSpec((tm, D), lambda i: (i, 0))
    vec = pl.BlockSpec((1, D), lambda i: (0, 0))
    lvec = pl.BlockSpec((1, LANES), lambda i: (0, 0))
    return pl.pallas_call(
        body, name=name, grid=(T // tm,), in_specs=[row, row, vec, vec, row], out_specs=[row, lvec, vec],
        out_shape=[jax.ShapeDtypeStruct((T, D), _F32), jax.ShapeDtypeStruct((1, LANES), _F32),
                   jax.ShapeDtypeStruct((1, D), _F32)],
        compiler_params=_cparams(("arbitrary",)))(x, branch, gate, gfin, tgt)


def _log_sigmoid(x):
    return jnp.minimum(x, 0.0) - jnp.log(1.0 + jnp.exp(-jnp.abs(x)))


def fgate_fwd(flog, bf, *, name):
    T = flog.shape[0]
    tt = _tile(T, 256)

    def body(x_ref, b_ref, f_ref, carry):
        i = pl.program_id(0)

        @pl.when(i == 0)
        def _():
            carry[...] = jnp.zeros_like(carry)

        lf = _log_sigmoid(x_ref[...] + b_ref[...])
        rows = lax.broadcasted_iota(jnp.int32, (tt, tt), 0)
        cols = lax.broadcasted_iota(jnp.int32, (tt, tt), 1)
        tri = (cols <= rows).astype(_F32)
        f_ref[...] = jnp.dot(tri, lf, preferred_element_type=_F32, precision=HIGHEST) + carry[0:1, :]
        carry[0:1, :] = f_ref[tt - 1:tt, :]

    return pl.pallas_call(
        body, name=name, grid=(T // tt,),
        in_specs=[pl.BlockSpec((tt, LANES), lambda i: (i, 0)), pl.BlockSpec((1, LANES), lambda i: (0, 0))],
        out_specs=pl.BlockSpec((tt, LANES), lambda i: (i, 0)),
        out_shape=jax.ShapeDtypeStruct((T, LANES), _F32),
        scratch_shapes=[pltpu.VMEM((SUBLANES, LANES), _F32)],
        compiler_params=_cparams(("arbitrary",)))(flog, bf)


def fgate_bwd(dfk, flog, bf, *, name):
    T = flog.shape[0]
    tt = _tile(T, 256)
    nb = T // tt

    def body(d_ref, x_ref, b_ref, o_ref, db_ref, carry):
        i = pl.program_id(0)

        @pl.when(i == 0)
        def _():
            carry[...] = jnp.zeros_like(carry)
            db_ref[...] = jnp.zeros_like(db_ref)

        rows = lax.broadcasted_iota(jnp.int32, (tt, tt), 0)
        cols = lax.broadcasted_iota(jnp.int32, (tt, tt), 1)
        upper = (cols >= rows).astype(_F32)
        dlf = jnp.dot(upper, d_ref[...], preferred_element_type=_F32, precision=HIGHEST) + carry[0:1, :]
        carry[0:1, :] = dlf[0:1, :]
        dfl = dlf * _sigmoid(-(x_ref[...] + b_ref[...]))
        o_ref[...] = dfl.astype(o_ref.dtype)
        db_ref[...] += _colsum(dfl)

    rev = pl.BlockSpec((tt, LANES), lambda i: (nb - 1 - i, 0))
    vec = pl.BlockSpec((1, LANES), lambda i: (0, 0))
    return pl.pallas_call(
        body, name=name, grid=(nb,), in_specs=[rev, rev, vec], out_specs=[rev, vec],
        out_shape=[jax.ShapeDtypeStruct((T, LANES), _MM), jax.ShapeDtypeStruct((1, LANES), _F32)],
        scratch_shapes=[pltpu.VMEM((SUBLANES, LANES), _F32)],
        compiler_params=_cparams(("arbitrary",)))(dfk, flog, bf)


def _att_scores(q, k, fq, fk, rep, masked):
    s = lax.dot_general(q, k, (((1,), (1,)), ((), ())), preferred_element_type=_F32)
    s = s * (HEAD_DIM ** -0.5) + (jnp.tile(fq, (1, rep)) - fk)
    if masked:
        rows = lax.broadcasted_iota(jnp.int32, s.shape, 0)
        cols = lax.broadcasted_iota(jnp.int32, s.shape, 1)
        s = jnp.where(cols <= rows, s, NEG)
    return s


def _fold_q(r, t, nb):
    first = t <= r
    return jnp.where(first, r, nb - 1 - r), jnp.where(first, t, t - r - 1)


def _fold_k(r, t, nb):
    first = t < nb - r
    return jnp.where(first, r, nb - 1 - r), jnp.where(first, r + t, t - 1)


def _first_step():
    return (pl.program_id(0) == 0) & (pl.program_id(1) == 0) & (pl.program_id(2) == 0)


def _step_is(h, r, t):
    return (pl.program_id(0) == h) & (pl.program_id(1) == r) & (pl.program_id(2) == t)


def attn_fwd(proj, fq, fk, *, heads, name, gather=None):
    T = proj.shape[0]
    H = heads
    tb = _tile(T, ATT_BLOCK)
    nb = T // tb
    assert nb % 2 == 0
    rep = tb // LANES
    n_g = len(gather) if gather else 0
    lay = _gather_layout(gather) if gather else None

    def body(*refs):
        q_ref, k_ref, v_ref, fq_ref, fk_ref = refs[:5]
        o_ref, lse_ref = refs[5 + n_g:7 + n_g]
        m_s, l_s, acc_s = refs[7 + 2 * n_g:10 + 2 * n_g]
        i, j = _fold_q(pl.program_id(1), pl.program_id(2), nb)
        if n_g:
            ops = _GatherOps(lay, refs[5:5 + n_g], refs[7 + n_g:7 + 2 * n_g], *refs[10 + 2 * n_g:])
            pl.when(_first_step())(ops.start)
            pl.when(_step_is(H // 2, 0, 0))(ops.forward)

        @pl.when(j == 0)
        def _():
            m_s[...] = jnp.full_like(m_s, NEG)
            l_s[...] = jnp.zeros_like(l_s)
            acc_s[...] = jnp.zeros_like(acc_s)

        def step(masked):
            s = _att_scores(q_ref[...], k_ref[...], fq_ref[...], fk_ref[...], rep, masked)
            m_prev = m_s[...]
            m_new = jnp.maximum(m_prev, jnp.max(s, axis=-1, keepdims=True))
            alpha = jnp.exp(m_prev - m_new)
            p = jnp.exp(s - jnp.tile(m_new, (1, rep)))
            l_s[...] = alpha * l_s[...] + jnp.sum(p, axis=-1, keepdims=True)
            v = v_ref[...]
            acc_s[...] = alpha * acc_s[...] + jnp.dot(p.astype(v.dtype), v, preferred_element_type=_F32)
            m_s[...] = m_new

        @pl.when(j < i)
        def _():
            step(False)

        @pl.when(j == i)
        def _():
            step(True)
            o_ref[...] = (acc_s[...] / l_s[...]).astype(o_ref.dtype)
            lse_ref[...] = m_s[...] + jnp.log(l_s[...])

        if n_g:
            pl.when(_step_is(H - 1, nb // 2 - 1, nb))(ops.finish)

    qi = lambda r, t: _fold_q(r, t, nb)[0]
    kj = lambda r, t: _fold_q(r, t, nb)[1]
    qs = pl.BlockSpec((tb, HEAD_DIM), lambda h, r, t: (qi(r, t), h))
    ks = pl.BlockSpec((tb, HEAD_DIM), lambda h, r, t: (kj(r, t), H + h))
    vs = pl.BlockSpec((tb, HEAD_DIM), lambda h, r, t: (kj(r, t), 2 * H + h))
    fqs = pl.BlockSpec((None, tb, LANES), lambda h, r, t: (h, qi(r, t), 0))
    fks = pl.BlockSpec((None, 1, tb), lambda h, r, t: (h, 0, kj(r, t)))
    hbm = pl.BlockSpec(memory_space=pl.ANY)
    outs = pl.pallas_call(
        body, name=name, grid=(H, nb // 2, nb + 1),
        in_specs=[qs, ks, vs, fqs, fks] + [hbm] * n_g,
        out_specs=[qs, fqs] + [hbm] * n_g,
        out_shape=[jax.ShapeDtypeStruct((T, H * HEAD_DIM), _MM), jax.ShapeDtypeStruct((H, T, LANES), _F32)]
        + (_gather_out_shapes(lay, gather[0].dtype) if n_g else []),
        scratch_shapes=[pltpu.VMEM((tb, LANES), _F32), pltpu.VMEM((tb, LANES), _F32),
                        pltpu.VMEM((tb, HEAD_DIM), _F32)] + (_comm_sems(n_g) if n_g else []),
        compiler_params=_cparams(("arbitrary",) * 3 if n_g else ("parallel", "parallel", "arbitrary")))(
            proj, proj, proj, fq, fk, *(gather or []))
    return outs[0], outs[1], outs[2:]


def _att_p_ds(q, k, v, do, o, fq, fk, lse, rep, masked):
    s = _att_scores(q, k, fq, fk, rep, masked)
    p = jnp.exp(s - jnp.tile(lse, (1, rep)))
    delta = jnp.sum(do.astype(_F32) * o.astype(_F32), axis=-1, keepdims=True)
    dp = lax.dot_general(do, v, (((1,), (1,)), ((), ())), preferred_element_type=_F32)
    ds = p * (dp - delta)
    return p, ds


def attn_bwd(proj, dcat, attn, fq, fk, lse, *, heads, name, scatter=None):
    T = proj.shape[0]
    H = heads
    tb = _tile(T, ATT_BLOCK)
    nb = T // tb
    assert nb % 2 == 0
    rep = tb // LANES
    scale = HEAD_DIM ** -0.5
    n_s = len(scatter) if scatter else 0
    lay = _scatter_layout(scatter) if scatter else None

    def body(*refs):
        q_ref, k_ref, v_ref, do_ref, o_ref, fq_ref, fk_ref, lse_ref = refs[:8]
        dq_ref, dk_ref, dv_ref, dfq_ref, dfk_ref = refs[8 + n_s:13 + n_s]
        dk_s, dv_s, dfk_s, dq_s, dfq_s = refs[13 + 2 * n_s:18 + 2 * n_s]
        r, t = pl.program_id(1), pl.program_id(2)
        j, i = _fold_k(r, t, nb)
        if n_s:
            ops = _ScatterOps(lay, refs[8:8 + n_s], refs[13 + n_s:13 + 2 * n_s], *refs[18 + 2 * n_s:])
            pl.when(_first_step())(ops.start)

        @pl.when((r == 0) & (t == 0))
        def _():
            dq_s[...] = jnp.zeros_like(dq_s)
            dfq_s[...] = jnp.zeros_like(dfq_s)

        @pl.when(i == j)
        def _():
            dk_s[...] = jnp.zeros_like(dk_s)
            dv_s[...] = jnp.zeros_like(dv_s)
            dfk_s[...] = jnp.zeros_like(dfk_s)

        def step(masked):
            q = q_ref[...]
            k = k_ref[...]
            do = do_ref[...]
            p, ds = _att_p_ds(q, k, v_ref[...], do, o_ref[...], fq_ref[...], fk_ref[...],
                              lse_ref[...], rep, masked)
            dsm = ds.astype(q.dtype)
            tn = (((0,), (0,)), ((), ()))
            dv_s[...] += lax.dot_general(p.astype(do.dtype), do, tn, preferred_element_type=_F32)
            dk_s[...] += lax.dot_general(dsm, q, tn, preferred_element_type=_F32)
            dfk_s[0:1, :] += -_colsum(ds)
            rows = pl.ds(pl.multiple_of(i * tb, tb), tb)
            dq_s[rows, :] += jnp.dot(dsm, k, preferred_element_type=_F32)
            dfq_s[rows, :] += jnp.sum(ds, axis=-1, keepdims=True)

        @pl.when(i > j)
        def _():
            step(False)

        @pl.when(i == j)
        def _():
            step(True)

        @pl.when(i == nb - 1)
        def _():
            dk_ref[...] = (dk_s[...] * scale).astype(dk_ref.dtype)
            dv_ref[...] = dv_s[...].astype(dv_ref.dtype)
            dfk_ref[...] = dfk_s[0:1, :]

        @pl.when((r == nb // 2 - 1) & (t == nb))
        def _():
            dq_ref[...] = (dq_s[...] * scale).astype(dq_ref.dtype)
            dfq_ref[...] = dfq_s[...]

        if n_s:
            pl.when(_step_is(H - 1, nb // 2 - 1, nb))(ops.finish)

    kj = lambda r, t: _fold_k(r, t, nb)[0]
    qi = lambda r, t: _fold_k(r, t, nb)[1]
    qs = pl.BlockSpec((tb, HEAD_DIM), lambda h, r, t: (qi(r, t), h))
    ks = pl.BlockSpec((tb, HEAD_DIM), lambda h, r, t: (kj(r, t), H + h))
    vs = pl.BlockSpec((tb, HEAD_DIM), lambda h, r, t: (kj(r, t), 2 * H + h))
    stat = pl.BlockSpec((None, tb, LANES), lambda h, r, t: (h, qi(r, t), 0))
    fks = pl.BlockSpec((None, 1, tb), lambda h, r, t: (h, 0, kj(r, t)))
    kout = pl.BlockSpec((tb, HEAD_DIM), lambda h, r, t: (kj(r, t), h))
    head_q = pl.BlockSpec((T, HEAD_DIM), lambda h, r, t: (0, h))
    head_stat = pl.BlockSpec((None, T, LANES), lambda h, r, t: (h, 0, 0))
    hbm = pl.BlockSpec(memory_space=pl.ANY)
    A = H * HEAD_DIM
    outs = pl.pallas_call(
        body, name=name, grid=(H, nb // 2, nb + 1),
        in_specs=[qs, ks, vs, qs, qs, stat, fks, stat] + [hbm] * n_s,
        out_specs=[head_q, kout, kout, head_stat, fks] + [hbm] * n_s,
        out_shape=[jax.ShapeDtypeStruct((T, A), _MM), jax.ShapeDtypeStruct((T, A), _MM),
                   jax.ShapeDtypeStruct((T, A), _MM), jax.ShapeDtypeStruct((H, T, LANES), _F32),
                   jax.ShapeDtypeStruct((H, 1, T), _F32)]
        + (_scatter_out_shapes(lay, scatter[0].dtype) if n_s else []),
        scratch_shapes=[pltpu.VMEM((tb, HEAD_DIM), _F32), pltpu.VMEM((tb, HEAD_DIM), _F32),
                        pltpu.VMEM((SUBLANES, tb), _F32), pltpu.VMEM((T, HEAD_DIM), _F32),
                        pltpu.VMEM((T, LANES), _F32)] + (_comm_sems(n_s) if n_s else []),
        compiler_params=_cparams(("arbitrary",) * 3 if n_s else ("parallel", "arbitrary", "arbitrary")))(
            proj, proj, proj, dcat, attn, fq, fk, lse, *(scatter or []))
    return (*outs[:5], outs[5:])


def _causal_taps(w_ref, scr, halo, tt, width):
    acc = w_ref[width - 1:width, :] * scr[halo:halo + tt, :]
    for j in range(1, width):
        acc = acc + w_ref[width - 1 - j:width - j, :] * scr[halo - j:halo - j + tt, :]
    return acc


def _anticausal_taps(w_ref, scr, tt, width):
    acc = w_ref[width - 1:width, :] * scr[0:tt, :]
    for j in range(1, width):
        acc = acc + w_ref[width - 1 - j:width - j, :] * scr[j:j + tt, :]
    return acc


def _ln_fwd(cc, g, b):
    mu = jnp.mean(cc, axis=-1, keepdims=True)
    xc = cc - mu
    rstd = lax.rsqrt(jnp.mean(xc * xc, axis=-1, keepdims=True) + LN_EPS)
    xhat = xc * rstd
    return xhat, rstd, xhat * g + b


def _ln_silu_bwd(cc, dconf, g, b):
    xhat, rstd, ln = _ln_fwd(cc, g, b)
    s = _sigmoid(ln)
    dln = dconf * (s * (1.0 + ln * (1.0 - s)))
    dxh = dln * g
    dcc = rstd * (dxh - jnp.mean(dxh, axis=-1, keepdims=True)
                  - xhat * jnp.mean(dxh * xhat, axis=-1, keepdims=True))
    return dcc, dln, xhat


def mixer_misc_fwd(proj, cw, cb, lg, lb, sw, *, width, base_col, name):
    T = proj.shape[0]
    C = width
    tt = _tile(T, CONV_TILE)
    HB = CONV_HALO
    per = tt // HB
    KC, KS = cw.shape[0], sw.shape[0]
    b0 = base_col // C

    def body(cv, cg, sx, sb, sc, cvh, cgh, sxh, sch, cw_ref, cb_ref, lg_ref, lb_ref, sw_ref,
             cm_ref, cc_ref, zc_ref, gscr, zscr):
        keep = (pl.program_id(0) > 0).astype(_F32)
        f = lambda r: r[...].astype(_F32)
        gscr[0:HB, :] = f(cvh) * _sigmoid(f(cgh)) * keep
        gscr[HB:HB + tt, :] = f(cv) * _sigmoid(f(cg))
        cc = _causal_taps(cw_ref, gscr, HB, tt, KC) + cb_ref[...]
        cc_ref[...] = cc
        _, _, ln = _ln_fwd(cc, lg_ref[...], lb_ref[...])
        cm_ref[:, 0:C] = (ln * _sigmoid(ln)).astype(cm_ref.dtype)
        zscr[0:HB, :] = f(sch) * f(sxh) * keep
        zscr[HB:HB + tt, :] = f(sc) * f(sx)
        zc = _causal_taps(sw_ref, zscr, HB, tt, KS)
        zc_ref[...] = zc
        cm_ref[:, C:2 * C] = (f(sb) * zc).astype(cm_ref.dtype)

    main = lambda k: pl.BlockSpec((tt, C), lambda i: (i, b0 + k))
    halo = lambda k: pl.BlockSpec((HB, C), lambda i: (jnp.maximum(i * per - 1, 0), b0 + k))
    full = lambda a: pl.BlockSpec(a.shape, lambda i: (0, 0))
    return pl.pallas_call(
        body, name=name, grid=(T // tt,),
        in_specs=[main(0), main(1), main(2), main(3), main(4), halo(0), halo(1), halo(2), halo(4),
                  full(cw), full(cb), full(lg), full(lb), full(sw)],
        out_specs=[pl.BlockSpec((tt, 2 * C), lambda i: (i, 0)), pl.BlockSpec((tt, C), lambda i: (i, 0)),
                   pl.BlockSpec((tt, C), lambda i: (i, 0))],
        out_shape=[jax.ShapeDtypeStruct((T, 2 * C), _MM), jax.ShapeDtypeStruct((T, C), _F32),
                   jax.ShapeDtypeStruct((T, C), _F32)],
        scratch_shapes=[pltpu.VMEM((tt + HB, C), _F32), pltpu.VMEM((tt + HB, C), _F32)],
        compiler_params=_cparams(("parallel",)))(
            proj, proj, proj, proj, proj, proj, proj, proj, proj, cw, cb, lg, lb, sw)


def mixer_misc_bwd(proj, dcat, cc, zc, cw, lg, lb, sw, *, width, base_col, dbase_col, name):
    T = proj.shape[0]
    C = width
    tt = _tile(T, CONV_TILE)
    nt = T // tt
    HB = CONV_HALO
    per = tt // HB
    KC, KS = cw.shape[0], sw.shape[0]
    b0 = base_col // C
    d0 = dbase_col // C
    last_hb = T // HB - 1

    def body(cv, cg, sx, sb, sc, cvh, cgh, sxh, sch, sbn, dcf, dsv, dcfn, dsvn, cc_ref, ccn_ref, zc_ref,
             cw_ref, lg_ref, lb_ref, sw_ref,
             dm_ref, dcw_ref, dcb_ref, dlg_ref, dlb_ref, dsw_ref, gscr, dscr, zscr, zdscr):
        i = pl.program_id(0)
        keep = (i > 0).astype(_F32)
        ahead = (i < nt - 1).astype(_F32)
        f = lambda r: r[...].astype(_F32)

        @pl.when(i == 0)
        def _():
            for r in (dcw_ref, dcb_ref, dlg_ref, dlb_ref, dsw_ref):
                r[...] = jnp.zeros_like(r)

        g, b = lg_ref[...], lb_ref[...]
        dcc, dln, xhat = _ln_silu_bwd(cc_ref[...], f(dcf), g, b)
        dcc_next, _, _ = _ln_silu_bwd(ccn_ref[...], f(dcfn), g, b)
        dlg_ref[...] += _colsum(dln * xhat)
        dlb_ref[...] += _colsum(dln)
        dcb_ref[...] += _colsum(dcc)
        dscr[0:tt, :] = dcc
        dscr[tt:tt + HB, :] = dcc_next * ahead
        dglu = _anticausal_taps(cw_ref, dscr, tt, KC)
        cvv = f(cv)
        sig = _sigmoid(f(cg))
        dm_ref[:, 0:C] = (dglu * sig).astype(dm_ref.dtype)
        dm_ref[:, C:2 * C] = (dglu * cvv * sig * (1.0 - sig)).astype(dm_ref.dtype)
        gscr[0:HB, :] = f(cvh) * _sigmoid(f(cgh)) * keep
        gscr[HB:HB + tt, :] = cvv * sig
        for j in range(KC):
            dcw_ref[KC - 1 - j:KC - j, :] += _colsum(dcc * gscr[HB - j:HB - j + tt, :])
        dsc_out = f(dsv)
        sbv = f(sb)
        dzc = dsc_out * sbv
        zdscr[0:tt, :] = dzc
        zdscr[tt:tt + HB, :] = f(dsvn) * f(sbn) * ahead
        dz = _anticausal_taps(sw_ref, zdscr, tt, KS)
        sxv, scv = f(sx), f(sc)
        dm_ref[:, 2 * C:3 * C] = (dz * scv).astype(dm_ref.dtype)
        dm_ref[:, 3 * C:4 * C] = (dsc_out * zc_ref[...]).astype(dm_ref.dtype)
        dm_ref[:, 4 * C:5 * C] = (dz * sxv).astype(dm_ref.dtype)
        zscr[0:HB, :] = f(sch) * f(sxh) * keep
        zscr[HB:HB + tt, :] = scv * sxv
        for j in range(KS):
            dsw_ref[KS - 1 - j:KS - j, :] += _colsum(dzc * zscr[HB - j:HB - j + tt, :])

    main = lambda col: pl.BlockSpec((tt, C), lambda i: (i, col))
    prev = lambda col: pl.BlockSpec((HB, C), lambda i: (jnp.maximum(i * per - 1, 0), col))
    nxt = lambda col: pl.BlockSpec((HB, C), lambda i: (jnp.minimum((i + 1) * per, last_hb), col))
    full = lambda a: pl.BlockSpec(a.shape, lambda i: (0, 0))
    vec = pl.BlockSpec((1, C), lambda i: (0, 0))
    return pl.pallas_call(
        body, name=name, grid=(nt,),
        in_specs=[main(b0), main(b0 + 1), main(b0 + 2), main(b0 + 3), main(b0 + 4),
                  prev(b0), prev(b0 + 1), prev(b0 + 2), prev(b0 + 4), nxt(b0 + 3),
                  main(d0), main(d0 + 1), nxt(d0), nxt(d0 + 1),
                  main(0), nxt(0), main(0),
                  full(cw), full(lg), full(lb), full(sw)],
        out_specs=[pl.BlockSpec((tt, 5 * C), lambda i: (i, 0)), full(cw), vec, vec, vec, full(sw)],
        out_shape=[jax.ShapeDtypeStruct((T, 5 * C), _MM), jax.ShapeDtypeStruct(cw.shape, _F32),
                   jax.ShapeDtypeStruct((1, C), _F32), jax.ShapeDtypeStruct((1, C), _F32),
                   jax.ShapeDtypeStruct((1, C), _F32), jax.ShapeDtypeStruct(sw.shape, _F32)],
        scratch_shapes=[pltpu.VMEM((tt + HB, C), _F32)] * 4,
        compiler_params=_cparams(("arbitrary",)))(
            proj, proj, proj, proj, proj, proj, proj, proj, proj, proj,
            dcat, dcat, dcat, dcat, cc, cc, zc, cw, lg, lb, sw)


def _ffn_u(main_ref, halo_ref, w_ref, b_ref, scr, keep, tt, width):
    HB = FFN_HALO
    scr[0:HB, :] = halo_ref[...].astype(_F32) * keep
    scr[HB:HB + tt, :] = main_ref[...].astype(_F32)
    return _causal_taps(w_ref, scr, HB, tt, width) + b_ref[...]


def ffn_act_fwd(up, w, b, *, name):
    T, F2 = up.shape
    F = F2 // 2
    K = w.shape[0]
    tt = _tile(T, CONV_TILE)
    tc = _tile(F, 512)
    nb = F // tc
    per = tt // FFN_HALO

    def body(g_ref, v_ref, gh_ref, vh_ref, wg_ref, wv_ref, bg_ref, bv_ref, o_ref, gscr, vscr):
        keep = (pl.program_id(0) > 0).astype(_F32)
        ug = _ffn_u(g_ref, gh_ref, wg_ref, bg_ref, gscr, keep, tt, K)
        uv = _ffn_u(v_ref, vh_ref, wv_ref, bv_ref, vscr, keep, tt, K)
        o_ref[...] = (ug * _sigmoid(ug) * uv).astype(o_ref.dtype)

    main = lambda off: pl.BlockSpec((tt, tc), lambda i, j: (i, j + off))
    halo = lambda off: pl.BlockSpec((FFN_HALO, tc), lambda i, j: (jnp.maximum(i * per - 1, 0), j + off))
    wsp = lambda off: pl.BlockSpec((K, tc), lambda i, j: (0, j + off))
    bsp = lambda off: pl.BlockSpec((1, tc), lambda i, j: (0, j + off))
    return pl.pallas_call(
        body, name=name, grid=(T // tt, nb),
        in_specs=[main(0), main(nb), halo(0), halo(nb), wsp(0), wsp(nb), bsp(0), bsp(nb)],
        out_specs=pl.BlockSpec((tt, tc), lambda i, j: (i, j)),
        out_shape=jax.ShapeDtypeStruct((T, F), _MM),
        scratch_shapes=[pltpu.VMEM((tt + FFN_HALO, tc), _F32)] * 2,
        compiler_params=_cparams(("parallel", "parallel")))(up, up, up, up, w, w, b, b)


def ffn_bwd_du(up, dact, w, b, *, name):
    T, F2 = up.shape
    F = F2 // 2
    K = w.shape[0]
    tt = _tile(T, CONV_TILE)
    tc = _tile(F, 512)
    nb = F // tc
    per = tt // FFN_HALO
    HB = FFN_HALO

    def body(g_ref, v_ref, gh_ref, vh_ref, da_ref, wg_ref, wv_ref, bg_ref, bv_ref, du_ref, dwb_ref,
             gscr, vscr):
        i = pl.program_id(1)
        keep = (i > 0).astype(_F32)
        ug = _ffn_u(g_ref, gh_ref, wg_ref, bg_ref, gscr, keep, tt, K)
        uv = _ffn_u(v_ref, vh_ref, wv_ref, bv_ref, vscr, keep, tt, K)
        s = _sigmoid(ug)
        da = da_ref[...].astype(_F32)
        du_g = da * uv * s * (1.0 + ug * (1.0 - s))
        du_v = da * ug * s
        du_ref[0] = du_g.astype(du_ref.dtype)
        du_ref[1] = du_v.astype(du_ref.dtype)

        @pl.when(i == 0)
        def _():
            dwb_ref[...] = jnp.zeros_like(dwb_ref)

        for half, (du, scr) in enumerate(((du_g, gscr), (du_v, vscr))):
            for j in range(K):
                dwb_ref[half, K - 1 - j:K - j, :] += _colsum(du * scr[HB - j:HB - j + tt, :])
            dwb_ref[half, K:K + 1, :] += _colsum(du)

    main = lambda off: pl.BlockSpec((tt, tc), lambda j, i: (i, j + off))
    halo = lambda off: pl.BlockSpec((HB, tc), lambda j, i: (jnp.maximum(i * per - 1, 0), j + off))
    wsp = lambda off: pl.BlockSpec((K, tc), lambda j, i: (0, j + off))
    bsp = lambda off: pl.BlockSpec((1, tc), lambda j, i: (0, j + off))
    return pl.pallas_call(
        body, name=name, grid=(nb, T // tt),
        in_specs=[main(0), main(nb), halo(0), halo(nb), main(0), wsp(0), wsp(nb), bsp(0), bsp(nb)],
        out_specs=[pl.BlockSpec((2, tt, tc), lambda j, i: (0, i, j)),
                   pl.BlockSpec((2, SUBLANES, tc), lambda j, i: (0, 0, j))],
        out_shape=[jax.ShapeDtypeStruct((2, T, F), _MM), jax.ShapeDtypeStruct((2, SUBLANES, F), _F32)],
        scratch_shapes=[pltpu.VMEM((tt + HB, tc), _F32)] * 2,
        compiler_params=_cparams(("parallel", "arbitrary")))(up, up, up, up, dact, w, w, b, b)


def dwconv_transpose(du, w, *, name):
    _, T, F = du.shape
    K = w.shape[0]
    tt = _tile(T, CONV_TILE)
    tc = _tile(F, 512)
    nb = F // tc
    per = tt // FFN_HALO
    HB = FFN_HALO
    nt = T // tt
    last_hb = T // HB - 1

    def body(d_ref, dn_ref, w_ref, o_ref, scr):
        ahead = (pl.program_id(1) < nt - 1).astype(_F32)
        scr[0:tt, :] = d_ref[...].astype(_F32)
        scr[tt:tt + HB, :] = dn_ref[...].astype(_F32) * ahead
        o_ref[...] = _anticausal_taps(w_ref, scr, tt, K).astype(o_ref.dtype)

    return pl.pallas_call(
        body, name=name, grid=(2, nt, nb),
        in_specs=[pl.BlockSpec((None, tt, tc), lambda s, i, j: (s, i, j)),
                  pl.BlockSpec((None, HB, tc), lambda s, i, j: (s, jnp.minimum((i + 1) * per, last_hb), j)),
                  pl.BlockSpec((K, tc), lambda s, i, j: (0, s * nb + j))],
        out_specs=pl.BlockSpec((tt, tc), lambda s, i, j: (i, s * nb + j)),
        out_shape=jax.ShapeDtypeStruct((T, 2 * F), _MM),
        scratch_shapes=[pltpu.VMEM((tt + HB, tc), _F32)],
        compiler_params=_cparams(("parallel", "parallel", "parallel")))(du, du, w)


def _adamw_math(w, g, m, v):
    m = ADAM_B1 * m + (1.0 - ADAM_B1) * g
    v = ADAM_B2 * v + (1.0 - ADAM_B2) * (g * g)
    m_hat = m / (1.0 - ADAM_B1 ** ADAM_STEP)
    v_hat = v / (1.0 - ADAM_B2 ** ADAM_STEP)
    delta = -ADAM_LR * (m_hat / (jnp.sqrt(v_hat) + ADAM_EPS) + ADAM_WD * w)
    return delta, m, v


def _as2d(a):
    return a.reshape(1, -1) if a.ndim == 1 else a.reshape(-1, a.shape[-1])


def adamw(w, g, m, v, *, name):
    shape = w.shape
    w2, g2, m2, v2 = _as2d(w), _as2d(g), _as2d(m), _as2d(v)
    R, C = w2.shape
    lanes = -(-C // LANES) * LANES
    tr = _tile(R, max(SUBLANES, (1 << 20) // (4 * lanes)), SUBLANES)

    def body(w_ref, g_ref, m_ref, v_ref, d_ref, mo_ref, vo_ref):
        d, mn, vn = _adamw_math(w_ref[...], g_ref[...], m_ref[...], v_ref[...])
        d_ref[...] = d
        mo_ref[...] = mn
        vo_ref[...] = vn

    blk = pl.BlockSpec((tr, C), lambda i: (i, 0))
    outs = pl.pallas_call(
        body, name=name, grid=(R // tr,), in_specs=[blk] * 4, out_specs=[blk] * 3,
        out_shape=[jax.ShapeDtypeStruct((R, C), _F32)] * 3,
        compiler_params=_cparams(("parallel",)))(w2, g2, m2, v2)
    return tuple(o.reshape(shape) for o in outs)


def ada_w_update(c_t, d_ada, w, m, v, *, name):
    L, D, N = w.shape
    B = c_t.shape[1]
    tr = _tile(D, 256, SUBLANES)
    tn = _tile(N, 1024)

    def body(c_ref, a_ref, w_ref, m_ref, v_ref, g_ref, d_ref, mo_ref, vo_ref):
        cv = c_ref[...]
        cv = cv * _sigmoid(cv)
        g = jnp.dot(cv, a_ref[...], preferred_element_type=_F32, precision=HIGHEST)
        g_ref[...] = g
        d, mn, vn = _adamw_math(w_ref[...], g, m_ref[...], v_ref[...])
        d_ref[...] = d
        mo_ref[...] = mn
        vo_ref[...] = vn

    blk = pl.BlockSpec((None, tr, tn), lambda l, i, j: (l, i, j))
    return pl.pallas_call(
        body, name=name, grid=(L, D // tr, N // tn),
        in_specs=[pl.BlockSpec((tr, B), lambda l, i, j: (i, 0)),
                  pl.BlockSpec((None, B, tn), lambda l, i, j: (l, 0, j)), blk, blk, blk],
        out_specs=[blk] * 4,
        out_shape=[jax.ShapeDtypeStruct((L, D, N), _F32)] * 4,
        compiler_params=_cparams(("parallel", "parallel", "parallel")))(c_t, d_ada, w, m, v)


def _coords():
    return lax.axis_index("x"), lax.axis_index("y"), lax.axis_index("c")


def allgather_small(x, *, with_sum, name):
    R, C = x.shape

    def body(x_ref, out_ref, *rest):
        if with_sum:
            sum_ref, send_sems, recv_sems, local_sem = rest
        else:
            send_sems, recv_sems, local_sem = rest
        px, py, pc = _coords()
        me, sibling = (px, py, pc), (px, py, 1 - pc)
        chips = [(1 - px, py), (px, 1 - py), (1 - px, 1 - py)]

        def rows(qx, qy, qc):
            return out_ref.at[4 * qx + 2 * qy + qc]

        def copy(k, block, to, src=None):
            return pltpu.make_async_remote_copy(
                src_ref=rows(*block) if src is None else src, dst_ref=rows(*block),
                send_sem=send_sems.at[k], recv_sem=recv_sems.at[k], device_id=to, device_id_type=MESH)

        mine = pltpu.make_async_copy(x_ref, rows(*me), local_sem)
        mine.start()
        first = [copy(0, me, sibling, src=x_ref)]
        first += [copy(1 + j, me, (*chip, pc), src=x_ref) for j, chip in enumerate(chips)]
        for cp in first:
            cp.start()
        passed = [copy(4 + j, (*chip, pc), sibling) for j, chip in enumerate(chips)]
        for j, chip in enumerate(chips):
            copy(1 + j, (*chip, pc), me).wait_recv()
            passed[j].start()
        copy(0, sibling, me).wait_recv()
        for j, chip in enumerate(chips):
            copy(4 + j, (*chip, 1 - pc), me).wait_recv()
        for cp in first + passed:
            cp.wait_send()
        mine.wait()
        if with_sum:
            acc = out_ref[0]
            for k in range(1, N_DEV):
                acc = acc + out_ref[k]
            sum_ref[...] = acc

    vm = pl.BlockSpec(memory_space=pltpu.VMEM)
    out_shape = [jax.ShapeDtypeStruct((N_DEV, R, C), x.dtype)]
    if with_sum:
        out_shape.append(jax.ShapeDtypeStruct((R, C), x.dtype))
    outs = pl.pallas_call(
        body, name=name, in_specs=[vm], out_specs=[vm] * len(out_shape), out_shape=out_shape,
        scratch_shapes=[pltpu.SemaphoreType.DMA((7,)), pltpu.SemaphoreType.DMA((7,)), pltpu.SemaphoreType.DMA],
        compiler_params=pltpu.CompilerParams(vmem_limit_bytes=VMEM_LIMIT_V7X))(x)
    return outs if with_sum else outs[0]


def _at(start, size, align):
    return pl.ds(pl.multiple_of(start, align) if align > 1 else start, size)


class _BigLayout:
    def __init__(self, D, INs, Ds, F2s, Fs):
        self.D, self.INs, self.Ds, self.F2s, self.Fs = D, INs, Ds, F2s, Fs
        self.Dh, self.Dsh, self.Fsh = D // 2, Ds // 2, Fs // 2
        self.piece_shapes = [(self.Dh, INs), (self.Dsh, D), (self.Dh, F2s), (self.Fsh, D)]

    def in_full(self, a, ref, k, h):
        if a == 0:
            return ref.at[k, _at(h * self.Dh, self.Dh, self.Dh), :]
        if a == 1:
            return ref.at[_at(k * self.Ds + h * self.Dsh, self.Dsh, self.Dsh), :]
        if a == 2:
            return ref.at[_at(h * self.Dh, self.Dh, self.Dh), _at(k * self.F2s, self.F2s, self.F2s)]
        return ref.at[_at(k * self.Fs + h * self.Fsh, self.Fsh, self.Fsh), :]

    def in_shard(self, a, ref, h):
        rows = self.piece_shapes[a][0]
        return ref.at[_at(h * rows, rows, rows), :]


def gather_weights(shards, *, name):
    lay = _gather_layout(shards)
    n_arr = len(shards)

    def body(*refs):
        ops = _GatherOps(lay, refs[:n_arr], refs[n_arr:2 * n_arr], *refs[2 * n_arr:])
        ops.start()
        ops.forward()
        ops.finish()

    hbm = pl.BlockSpec(memory_space=pl.ANY)
    return pl.pallas_call(
        body, name=name, in_specs=[hbm] * n_arr, out_specs=[hbm] * n_arr,
        out_shape=_gather_out_shapes(lay, shards[0].dtype), scratch_shapes=_comm_sems(n_arr),
    )(*shards)


def _gather_layout(shards):
    D, INs = shards[0].shape
    return _BigLayout(D, INs, shards[1].shape[0], shards[2].shape[1], shards[3].shape[0])


def _gather_out_shapes(lay, dt):
    return [jax.ShapeDtypeStruct((N_CHIP, lay.D, lay.INs), dt), jax.ShapeDtypeStruct((lay.D, lay.D), dt),
            jax.ShapeDtypeStruct((lay.D, lay.F2s * N_CHIP), dt), jax.ShapeDtypeStruct((lay.Fs * N_CHIP, lay.D), dt)]


def _comm_sems(n_arr):
    return [pltpu.SemaphoreType.DMA((7 * n_arr,)), pltpu.SemaphoreType.DMA((7 * n_arr,)),
            pltpu.SemaphoreType.DMA((n_arr,))]


class _GatherOps:
    def __init__(self, lay, ins, outs, send_sems, recv_sems, local_sems):
        self.lay, self.ins, self.outs = lay, ins, outs
        self.send_sems, self.recv_sems, self.local_sems = send_sems, recv_sems, local_sems
        px, py, pc = _coords()
        self.pc, self.my_chip, self.sibling = pc, 2 * px + py, (px, py, 1 - pc)
        self.chips = [(1 - px, py), (px, 1 - py), (1 - px, 1 - py)]

    def _copy(self, a, kk, k, h, to, src=None):
        dst = self.lay.in_full(a, self.outs[a], k, h)
        return pltpu.make_async_remote_copy(
            src_ref=dst if src is None else src, dst_ref=dst,
            send_sem=self.send_sems.at[7 * a + kk], recv_sem=self.recv_sems.at[7 * a + kk],
            device_id=to, device_id_type=MESH)

    def _own(self, a):
        own = self.lay.in_shard(a, self.ins[a], self.pc)
        local = pltpu.make_async_copy(own, self.lay.in_full(a, self.outs[a], self.my_chip, self.pc),
                                      self.local_sems.at[a])
        sends = [self._copy(a, 0, self.my_chip, self.pc, self.sibling, src=own)]
        sends += [self._copy(a, 1 + j, self.my_chip, self.pc, (*chip, self.pc), src=own)
                  for j, chip in enumerate(self.chips)]
        return local, sends

    def _passed_on(self, a, j):
        qx, qy = self.chips[j]
        return self._copy(a, 4 + j, 2 * qx + qy, self.pc, self.sibling)

    def start(self):
        for a in range(len(self.ins)):
            local, sends = self._own(a)
            for cp in [local] + sends:
                cp.start()

    def forward(self):
        for j, (qx, qy) in enumerate(self.chips):
            for a in range(len(self.ins)):
                self._copy(a, 1 + j, 2 * qx + qy, self.pc, self.sibling).wait_recv()
                self._passed_on(a, j).start()

    def finish(self):
        for a in range(len(self.ins)):
            self._copy(a, 0, self.my_chip, 1 - self.pc, self.sibling).wait_recv()
            for j, (qx, qy) in enumerate(self.chips):
                self._copy(a, 4 + j, 2 * qx + qy, 1 - self.pc, self.sibling).wait_recv()
        for a in range(len(self.ins)):
            local, sends = self._own(a)
            for cp in sends + [self._passed_on(a, j) for j in range(len(self.chips))]:
                cp.wait_send()
            local.wait()


def scatter_grads(partials, *, name):
    lay = _scatter_layout(partials)
    n_arr = len(partials)

    def body(*refs):
        ops = _ScatterOps(lay, refs[:n_arr], refs[n_arr:2 * n_arr], *refs[2 * n_arr:])
        ops.start()
        ops.finish()

    hbm = pl.BlockSpec(memory_space=pl.ANY)
    return pl.pallas_call(
        body, name=name, in_specs=[hbm] * n_arr, out_specs=[hbm] * n_arr,
        out_shape=_scatter_out_shapes(lay, partials[0].dtype), scratch_shapes=_comm_sems(n_arr),
    )(*partials)


def _scatter_layout(partials):
    _, D, INs = partials[0].shape
    return _BigLayout(D, INs, partials[1].shape[0] // N_CHIP, partials[2].shape[1] // N_CHIP,
                      partials[3].shape[0] // N_CHIP)


def _scatter_out_shapes(lay, dt):
    return [jax.ShapeDtypeStruct((N_DEV, *s), dt) for s in lay.piece_shapes]


class _ScatterOps:
    def __init__(self, lay, ins, outs, send_sems, recv_sems, local_sems):
        self.lay, self.ins, self.outs = lay, ins, outs
        self.send_sems, self.recv_sems, self.local_sems = send_sems, recv_sems, local_sems

    def _copies(self):
        px, py, pc = _coords()
        me = 4 * px + 2 * py + pc
        copies, mine = [], []
        for a in range(len(self.ins)):
            mine.append(pltpu.make_async_copy(self.lay.in_full(a, self.ins[a], 2 * px + py, pc),
                                              self.outs[a].at[me], self.local_sems.at[a]))
            for mask in range(1, N_DEV):
                qx = 1 - px if (mask >> 2) & 1 else px
                qy = 1 - py if (mask >> 1) & 1 else py
                qc = 1 - pc if mask & 1 else pc
                copies.append(pltpu.make_async_remote_copy(
                    src_ref=self.lay.in_full(a, self.ins[a], 2 * qx + qy, qc), dst_ref=self.outs[a].at[me],
                    send_sem=self.send_sems.at[7 * a + mask - 1], recv_sem=self.recv_sems.at[7 * a + mask - 1],
                    device_id=(qx, qy, qc), device_id_type=MESH))
        return mine, copies

    def start(self):
        mine, copies = self._copies()
        for cp in mine + copies:
            cp.start()

    def finish(self):
        mine, copies = self._copies()
        for cp in copies:
            cp.wait_recv()
        for cp in copies:
            cp.wait_send()
        for cp in mine:
            cp.wait()


SIBLING_CHUNKS = 4


def sibling_exchange(bufs, *, name):
    n_arr = len(bufs)
    n_ch = [max(n for n in (SIBLING_CHUNKS, 2, 1) if x.shape[1] % (n * SUBLANES) == 0 or n == 1) for x in bufs]
    offs = [sum(n_ch[:a]) for a in range(n_arr)]

    def body(*refs):
        outs = refs[n_arr:2 * n_arr]
        send_sems, recv_sems = refs[2 * n_arr:]
        px, py, pc = _coords()
        copies = []
        for a in range(n_arr):
            rows = bufs[a].shape[1] // n_ch[a]
            for q in range(n_ch[a]):
                mine = outs[a].at[pc, pl.ds(q * rows, rows), :]
                copies.append(pltpu.make_async_remote_copy(
                    src_ref=mine, dst_ref=mine,
                    send_sem=send_sems.at[offs[a] + q], recv_sem=recv_sems.at[offs[a] + q],
                    device_id=(px, py, 1 - pc), device_id_type=MESH))
        for cp in copies:
            cp.start()
        for cp in copies:
            cp.wait_recv()
        for cp in copies:
            cp.wait_send()

    hbm = pl.BlockSpec(memory_space=pl.ANY)
    return pl.pallas_call(
        body, name=name, in_specs=[hbm] * n_arr, out_specs=[hbm] * n_arr,
        out_shape=[jax.ShapeDtypeStruct(x.shape, x.dtype) for x in bufs],
        input_output_aliases={a: a for a in range(n_arr)},
        scratch_shapes=[pltpu.SemaphoreType.DMA((sum(n_ch),))] * 2,
    )(*bufs)


def sum_slots(x, core, *, name):
    n, R, C = x.shape
    lanes = -(-C // LANES) * LANES
    tr = _tile(R, max(BF16_ROWS, (4 << 20) // (n * 2 * lanes)), BF16_ROWS)

    def body(core_ref, x_ref, o_ref):
        acc = x_ref[0].astype(_F32)
        for k in range(1, n):
            acc = acc + x_ref[k].astype(_F32)
        o_ref[...] = acc

    return pl.pallas_call(
        body, name=name,
        grid_spec=pltpu.PrefetchScalarGridSpec(
            num_scalar_prefetch=1, grid=(R // tr,),
            in_specs=[pl.BlockSpec((n, tr, C), lambda i, core_ref: (0, i, 0))],
            out_specs=pl.BlockSpec((None, tr, C), lambda i, core_ref: (core_ref[0], i, 0))),
        out_shape=jax.ShapeDtypeStruct((2, R, C), _F32),
        compiler_params=_cparams(("parallel",)))(core, x)


def _pack_flat(arrays, quantum):
    flat = jnp.concatenate([a.reshape(-1) for a in arrays])
    pad = (-flat.shape[0]) % quantum
    return jnp.pad(flat, (0, pad)) if pad else flat


def _unpack_flat(flat, shapes):
    out, off = [], 0
    for s in shapes:
        n = math.prod(s)
        out.append(flat[off:off + n].reshape(s))
        off += n
    return out


def _small_pack(arrays):
    return _pack_flat([a.astype(_F32) for a in arrays], SUBLANES * LANES).reshape(-1, LANES)


def kernel(x, c, ada_w, ada_b, mix_norm_g, w_in, b_forget, conf_dw_w, conf_dw_b, conf_ln_g, conf_ln_b, sc_dw_w, w_out, ffn_norm_g, w_up, ffn_dw_w, ffn_dw_b, w_down, final_norm_g, loss_target, m_ada_w, m_ada_b, m_mix_norm_g, m_w_in, m_b_forget, m_conf_dw_w, m_conf_dw_b, m_conf_ln_g, m_conf_ln_b, m_sc_dw_w, m_w_out, m_ffn_norm_g, m_w_up, m_ffn_dw_w, m_ffn_dw_b, m_w_down, m_final_norm_g, v_ada_w, v_ada_b, v_mix_norm_g, v_w_in, v_b_forget, v_conf_dw_w, v_conf_dw_b, v_conf_ln_g, v_conf_ln_b, v_sc_dw_w, v_w_out, v_ffn_norm_g, v_w_up, v_ffn_dw_w, v_ffn_dw_b, v_w_down, v_final_norm_g):
    _, T, D = x.shape
    L = ada_w.shape[0]
    A = D // 2
    H = A // HEAD_DIM
    C = D // 4
    assert D - A - C == C
    IN = 3 * A + H + 5 * C
    NM = 3 * A + 5 * C
    NP = NM + LANES
    F2 = w_up.shape[2] * N_CHIP
    F = F2 // 2
    NA = ada_w.shape[2]
    assert NA * N_CHIP == 6 * D and w_in.shape[2] * N_CHIP == IN

    px, py, pc = _coords()
    chip = 2 * px + py
    me = 2 * chip + pc

    x0 = x[0]
    tgt = loss_target[0]

    c_all = allgather_small(c.reshape(-1, LANES), with_sum=False, name="gather_c").reshape(N_DEV, D)
    parts = [matmul(c_all, ada_w[l], out_dtype=_F32, name="ada_fwd", tm=N_DEV, tn=512, tk=D,
                    a_silu=True, precision=HIGHEST) for l in range(L)]
    parts = jnp.stack(parts)
    got = allgather_small(parts.reshape(-1, LANES), with_sum=False, name="gather_ada")
    got = got.reshape(N_DEV, L, N_DEV, NA)[0::2]
    mine = lax.dynamic_index_in_dim(got, me, axis=2, keepdims=False)
    ada = jnp.transpose(mine, (1, 0, 2)).reshape(L, 6 * D) + ada_b

    INs = IN // N_CHIP
    wp_l, wout_l, wup_l, wdown_l = [None] * L, [None] * L, [None] * L, [None] * L
    shards_of = lambda l: [w_in[l].astype(_MM), w_out[l].astype(_MM), w_up[l].astype(_MM), w_down[l].astype(_MM)]

    def take_gathered(l, gathered):
        wi4, wout_l[l], wup_l[l], wdown_l[l] = gathered
        wi = jnp.transpose(wi4, (1, 0, 2)).reshape(D, IN)
        wp_l[l] = jnp.concatenate(
            [wi[:, :3 * A], wi[:, 3 * A + H:], jnp.pad(wi[:, 3 * A:3 * A + H], ((0, 0), (0, LANES - H)))], axis=1)

    take_gathered(0, gather_weights(shards_of(0), name="gather_weights"))

    small_w = _small_pack([conf_dw_w, sc_dw_w, ffn_dw_w])
    sw_all = allgather_small(small_w, with_sum=False, name="gather_small_w")[0::2].reshape(N_CHIP, -1)
    sw_shapes = [conf_dw_w.shape, sc_dw_w.shape, ffn_dw_w.shape]
    sw_parts = [_unpack_flat(sw_all[k], sw_shapes) for k in range(N_CHIP)]
    conf_w_full = jnp.concatenate([p[0] for p in sw_parts], axis=-1)
    sc_w_full = jnp.concatenate([p[1] for p in sw_parts], axis=-1)
    ffn_w_full = jnp.concatenate([p[2] for p in sw_parts], axis=-1)
    bf_pad = jnp.pad(b_forget, ((0, 0), (0, LANES - H)))

    row = lambda a: a.reshape(1, -1)

    saved = []
    x_cur, branch, gate = x0, None, None
    for l in range(L):
        sh_m, sc_m, g_m, sh_f, sc_f, g_f = [row(ada[l, k * D:(k + 1) * D]) for k in range(6)]
        a1 = row(mix_norm_g[l]) * (1.0 + sc_m)
        a2 = row(ffn_norm_g[l]) * (1.0 + sc_f)
        x_in, h1 = resid_norm_fwd(x_cur, a1, sh_m, branch, gate, name="norm_mix_fwd")
        wp = wp_l[l]
        proj = matmul(h1, wp, out_dtype=_MM, name="proj_fwd", tn=512, tk=D, b_cols=(0, NM))
        flog = matmul(h1, wp, out_dtype=_F32, name="fgate_logits", tn=LANES, tk=D, b_cols=(NM, LANES))
        bf = row(bf_pad[l])
        fcum = fgate_fwd(flog, bf, name="fgate_fwd")
        f_t = fcum[:, :H].T
        fq = jnp.broadcast_to(f_t[:, :, None], (H, T, LANES))
        fk = f_t[:, None, :]
        if l + 1 < L:
            attn, lse, gathered = attn_fwd(proj, fq, fk, heads=H, name="attn_fwd_gather", gather=shards_of(l + 1))
            take_gathered(l + 1, gathered)
        else:
            attn, lse, _ = attn_fwd(proj, fq, fk, heads=H, name="attn_fwd")
        cw, cb = conf_w_full[l], row(conf_dw_b[l])
        lg, lb, sw = row(conf_ln_g[l]), row(conf_ln_b[l]), sc_w_full[l]
        cm, cc, zc = mixer_misc_fwd(proj, cw, cb, lg, lb, sw, width=C, base_col=3 * A, name="misc_fwd")
        cat = jnp.concatenate([attn, cm], axis=1)
        mixed = matmul(cat, wout_l[l], out_dtype=_F32, name="wout_fwd", tn=512, tk=D)
        x_mid, h2 = resid_norm_fwd(x_in, a2, sh_f, mixed, g_m, name="norm_ffn_fwd")
        up = matmul(h2, wup_l[l], out_dtype=_MM, name="wup_fwd", tn=512, tk=D)
        fw, fb = ffn_w_full[l], row(ffn_dw_b[l])
        act = ffn_act_fwd(up, fw, fb, name="ffn_act_fwd")
        dn = matmul(act, wdown_l[l], out_dtype=_F32, name="wdown_fwd", tk=1408)
        saved.append(dict(x_in=x_in, h1=h1, proj=proj, flog=flog, fq=fq, fk=fk, attn=attn, lse=lse, cc=cc, zc=zc,
                          cat=cat, mixed=mixed, x_mid=x_mid, h2=h2, up=up, act=act, dn=dn, a1=a1, a2=a2,
                          g_m=g_m, g_f=g_f, sc_m=sc_m, sc_f=sc_f, bf=bf, cw=cw, lg=lg, lb=lb, sw=sw, fw=fw, fb=fb))
        x_cur, branch, gate = x_mid, dn, g_f

    dx, loss_row, d_final_g = final_loss_bwd(x_cur, branch, gate, row(final_norm_g), tgt, name="loss_bwd")

    KF = ffn_dw_w.shape[1]
    g_big, recv_l = [None] * L, [None] * L
    d_ada, d_g1, d_g2, d_bf, d_cw, d_cb, d_lg, d_lb, d_sw, d_fw, d_fb = ([None] * L for _ in range(11))
    for l in reversed(range(L)):
        s = saved[l]
        ddn, dg_f = gate_bwd(dx, s["dn"], s["g_f"], name="gate_ffn_bwd")
        dact = matmul(ddn, wdown_l[l], out_dtype=_MM, name="wdown_dgrad", trans_b=True, tn=512, tk=D)
        gw_down = matmul(s["act"], ddn, out_dtype=_MM, name="wdown_wgrad", trans_a=True, tm=512)
        du, dwb = ffn_bwd_du(s["up"], dact, s["fw"], s["fb"], name="ffn_bwd_du")
        dup = dwconv_transpose(du, s["fw"], name="ffn_bwd_dup")
        dh2 = matmul(dup, wup_l[l], out_dtype=_MM, name="wup_dgrad", trans_b=True)
        gw_up = matmul(s["h2"], dup, out_dtype=_MM, name="wup_wgrad", trans_a=True)
        dx_mid, dsh_f, da2 = norm_bwd(s["x_mid"], dh2, dx, s["a2"], name="norm_ffn_bwd")
        dmixed, dg_m = gate_bwd(dx_mid, s["mixed"], s["g_m"], name="gate_mix_bwd")
        dcat = matmul(dmixed, wout_l[l], out_dtype=_MM, name="wout_dgrad", trans_b=True, tn=512, tk=D)
        gw_out = matmul(s["cat"], dmixed, out_dtype=_MM, name="wout_wgrad", trans_a=True)
        if l + 1 < L:
            dq, dk, dv, dfq, dfk, recv_l[l + 1] = attn_bwd(
                s["proj"], dcat, s["attn"], s["fq"], s["fk"], s["lse"], heads=H, name="attn_bwd_scatter",
                scatter=g_big[l + 1])
        else:
            dq, dk, dv, dfq, dfk, _ = attn_bwd(s["proj"], dcat, s["attn"], s["fq"], s["fk"], s["lse"], heads=H,
                                               name="attn_bwd")
        dmisc, d_cw[l], d_cb[l], d_lg[l], d_lb[l], d_sw[l] = mixer_misc_bwd(
            s["proj"], dcat, s["cc"], s["zc"], s["cw"], s["lg"], s["lb"], s["sw"],
            width=C, base_col=3 * A, dbase_col=A, name="misc_bwd")
        dfk_pad = jnp.pad((dfk[:, 0, :] + dfq[:, :, 0]).T, ((0, 0), (0, LANES - H)))
        dflog, dbf = fgate_bwd(dfk_pad, s["flog"], s["bf"], name="fgate_bwd")
        dproj = jnp.concatenate([dq, dk, dv, dmisc, dflog], axis=1)
        dh1 = matmul(dproj, wp_l[l], out_dtype=_MM, name="proj_dgrad", trans_b=True, tk=640)
        gwp = matmul(s["h1"], dproj, out_dtype=_MM, name="proj_wgrad", trans_a=True, tn=640)
        dx, dsh_m, da1 = norm_bwd(s["x_in"], dh1, dx_mid, s["a1"], name="norm_mix_bwd")

        g1, g2 = row(mix_norm_g[l]), row(ffn_norm_g[l])
        d_ada[l] = jnp.concatenate([dsh_m, da1 * g1, dg_m, dsh_f, da2 * g2, dg_f], axis=1)[0]
        d_g1[l] = (da1 * (1.0 + s["sc_m"]))[0]
        d_g2[l] = (da2 * (1.0 + s["sc_f"]))[0]
        d_bf[l] = dbf[0, :H]
        d_fw[l] = jnp.concatenate([dwb[0, :KF], dwb[1, :KF]], axis=1)
        d_fb[l] = jnp.concatenate([dwb[0, KF], dwb[1, KF]])
        gw_in = jnp.concatenate([gwp[:, :3 * A], gwp[:, NM:NM + H], gwp[:, 3 * A:NM]], axis=1)
        gw_in4 = jnp.transpose(gw_in.reshape(D, N_CHIP, INs), (1, 0, 2))
        g_big[l] = [gw_in4, gw_out, gw_up, gw_down]

    small = [loss_row[0], jnp.stack(d_g1), jnp.stack(d_bf), jnp.stack(d_cw), jnp.stack(d_cb)[:, 0],
             jnp.stack(d_lg)[:, 0], jnp.stack(d_lb)[:, 0], jnp.stack(d_sw), jnp.stack(d_g2), jnp.stack(d_fw),
             jnp.stack(d_fb), d_final_g[0], jnp.stack(d_ada)]
    small_shapes = [a.shape for a in small]
    sm_all, sm_sum = allgather_small(_small_pack(small), with_sum=True, name="reduce_small")
    (loss_v, g_mix_norm, g_bf, g_cw_full, g_cb, g_lg, g_lb, g_sw_full, g_ffn_norm, g_fw_full, g_fb, g_final,
     g_ada_b) = _unpack_flat(sm_sum.reshape(-1), small_shapes)
    loss = loss_v[0]
    n_ada = L * 6 * D
    off_ada = sum(math.prod(sh) for sh in small_shapes[:-1])
    d_ada_all = sm_all.reshape(N_DEV, -1)[:, off_ada:off_ada + n_ada].reshape(N_DEV, L, 6 * D)
    d_ada_chip = lax.dynamic_slice_in_dim(d_ada_all, chip * NA, NA, axis=2)
    d_ada_chip = jnp.transpose(d_ada_chip, (1, 0, 2))
    cshard = lambda a: lax.dynamic_slice_in_dim(a, chip * (a.shape[-1] // N_CHIP), a.shape[-1] // N_CHIP,
                                                axis=a.ndim - 1)
    g_conf_dw_w, g_sc_dw_w, g_ffn_dw_w = cshard(g_cw_full), cshard(g_sw_full), cshard(g_fw_full)

    g_in_l, g_out_l, g_up_l, g_down_l = [], [], [], []
    core = jnp.reshape(pc, (1,)).astype(jnp.int32)
    recv_l[0] = scatter_grads(g_big[0], name="scatter_grads")
    for l in range(L):
        red = [sum_slots(r, core, name="sum_grads") for r in recv_l[l]]
        gi, go, gu, gd = sibling_exchange(red, name="sibling_grads")
        g_in_l.append(gi.reshape(w_in.shape[1:])); g_out_l.append(go.reshape(w_out.shape[1:]))
        g_up_l.append(gu.reshape(w_up.shape[1:])); g_down_l.append(gd.reshape(w_down.shape[1:]))
    g_w_in, g_w_out, g_w_up, g_w_down = (jnp.stack(t) for t in (g_in_l, g_out_l, g_up_l, g_down_l))

    g_ada_w, dl_ada_w, nm_ada_w, nv_ada_w = ada_w_update(c_all.T, d_ada_chip, ada_w, m_ada_w, v_ada_w,
                                                          name="ada_w_update")

    grads = dict(ada_b=g_ada_b, mix_norm_g=g_mix_norm, w_in=g_w_in, b_forget=g_bf, conf_dw_w=g_conf_dw_w,
                 conf_dw_b=g_cb, conf_ln_g=g_lg, conf_ln_b=g_lb, sc_dw_w=g_sc_dw_w, w_out=g_w_out,
                 ffn_norm_g=g_ffn_norm, w_up=g_w_up, ffn_dw_w=g_ffn_dw_w, ffn_dw_b=g_fb, w_down=g_w_down,
                 final_norm_g=g_final)
    weights = dict(ada_b=(ada_b, m_ada_b, v_ada_b), mix_norm_g=(mix_norm_g, m_mix_norm_g, v_mix_norm_g),
                   w_in=(w_in, m_w_in, v_w_in), b_forget=(b_forget, m_b_forget, v_b_forget),
                   conf_dw_w=(conf_dw_w, m_conf_dw_w, v_conf_dw_w), conf_dw_b=(conf_dw_b, m_conf_dw_b, v_conf_dw_b),
                   conf_ln_g=(conf_ln_g, m_conf_ln_g, v_conf_ln_g), conf_ln_b=(conf_ln_b, m_conf_ln_b, v_conf_ln_b),
                   sc_dw_w=(sc_dw_w, m_sc_dw_w, v_sc_dw_w), w_out=(w_out, m_w_out, v_w_out),
                   ffn_norm_g=(ffn_norm_g, m_ffn_norm_g, v_ffn_norm_g), w_up=(w_up, m_w_up, v_w_up),
                   ffn_dw_w=(ffn_dw_w, m_ffn_dw_w, v_ffn_dw_w), ffn_dw_b=(ffn_dw_b, m_ffn_dw_b, v_ffn_dw_b),
                   w_down=(w_down, m_w_down, v_w_down), final_norm_g=(final_norm_g, m_final_norm_g, v_final_norm_g))
    order = ["ada_w", "ada_b", "mix_norm_g", "w_in", "b_forget", "conf_dw_w", "conf_dw_b", "conf_ln_g", "conf_ln_b",
             "sc_dw_w", "w_out", "ffn_norm_g", "w_up", "ffn_dw_w", "ffn_dw_b", "w_down", "final_norm_g"]
    g_out, d_out, m_out, v_out = {}, {}, {}, {}
    g_out["ada_w"], d_out["ada_w"], m_out["ada_w"], v_out["ada_w"] = g_ada_w, dl_ada_w, nm_ada_w, nv_ada_w
    for n in order[1:]:
        w, m, v = weights[n]
        g = grads[n].reshape(w.shape)
        g_out[n] = g
        d_out[n], m_out[n], v_out[n] = adamw(w, g, m, v, name="adamw_" + n)

    return (loss, dx[None], *[g_out[n] for n in order], *[d_out[n] for n in order],
            *[m_out[n] for n in order], *[v_out[n] for n in order])
```

```python
import functools
import math

import jax
import jax.numpy as jnp
from jax import lax
from jax.experimental import pallas as pl
from jax.experimental.pallas import tpu as pltpu

_MM = jnp.bfloat16
_F32 = jnp.float32
VMEM_LIMIT_V7X = 48 * 1024 * 1024
LANES = 128
SUBLANES = 8
BF16_ROWS = 16
HEAD_DIM = 128
RMS_EPS = 1e-6
LN_EPS = 1e-5
NEG = -1e30
HIGHEST = lax.Precision.HIGHEST

ADAM_LR = 0.001
ADAM_B1 = 0.9
ADAM_B2 = 0.999
ADAM_EPS = 1e-08
ADAM_WD = 0.01
ADAM_STEP = 10

N_DEV = 8
N_CHIP = 4

ATT_BLOCK = 512
CONV_TILE = 512
CONV_HALO = 32
FFN_HALO = BF16_ROWS
NORM_TILE = 256
MM_TM = 1024
MM_TN = 1024
MM_TK = 1024

MESH = pl.DeviceIdType.MESH


def _tile(dim, pref, mult=LANES):
    t = (min(pref, dim) // mult) * mult
    while t >= mult:
        if dim % t == 0:
            return t
        t -= mult
    return dim


def _cparams(sem):
    return pltpu.CompilerParams(dimension_semantics=sem, vmem_limit_bytes=VMEM_LIMIT_V7X)


def _sigmoid(x):
    return 1.0 / (1.0 + jnp.exp(-x))


def _colsum(x):
    return jnp.sum(x, axis=0, keepdims=True)


def matmul(a, b, *, out_dtype, name, trans_a=False, trans_b=False, tm=MM_TM, tn=MM_TN, tk=MM_TK,
           a_silu=False, precision=None, b_cols=None):
    M, K = (a.shape[1], a.shape[0]) if trans_a else a.shape
    N = b.shape[0] if trans_b else b.shape[1]
    assert (b.shape[1] if trans_b else b.shape[0]) == K
    col0 = 0
    if b_cols is not None:
        assert not trans_b
        col0, N = b_cols
    tm = _tile(M, tm, SUBLANES if (M % LANES) else LANES)
    tn = _tile(math.gcd(N, col0) if col0 else N, tn)
    tk = _tile(K, tk)
    nk = K // tk
    jb = col0 // tn
    dims = (((0 if trans_a else 1,), (1 if trans_b else 0,)), ((), ()))

    def body(a_ref, b_ref, o_ref, *scratch):
        av = a_ref[...]
        if a_silu:
            av = av * _sigmoid(av)
        part = lax.dot_general(av, b_ref[...], dims, preferred_element_type=_F32, precision=precision)
        if nk == 1:
            o_ref[...] = part.astype(o_ref.dtype)
        else:
            acc_ref, = scratch
            k = pl.program_id(2)

            @pl.when(k == 0)
            def _():
                acc_ref[...] = part

            @pl.when(k > 0)
            def _():
                acc_ref[...] += part

            @pl.when(k == nk - 1)
            def _():
                o_ref[...] = acc_ref[...].astype(o_ref.dtype)

    a_spec = (pl.BlockSpec((tk, tm), lambda i, j, k: (k, i)) if trans_a
              else pl.BlockSpec((tm, tk), lambda i, j, k: (i, k)))
    b_spec = (pl.BlockSpec((tn, tk), lambda i, j, k: (j, k)) if trans_b
              else pl.BlockSpec((tk, tn), lambda i, j, k: (k, j + jb)))
    return pl.pallas_call(
        body, name=name, grid=(M // tm, N // tn, nk),
        in_specs=[a_spec, b_spec],
        out_specs=pl.BlockSpec((tm, tn), lambda i, j, k: (i, j)),
        out_shape=jax.ShapeDtypeStruct((M, N), out_dtype),
        scratch_shapes=[pltpu.VMEM((tm, tn), _F32)] if nk > 1 else [],
        compiler_params=_cparams(("parallel", "parallel", "arbitrary")),
    )(a, b)


def resid_norm_fwd(x, a, sh, branch=None, gate=None, *, name):
    T, D = x.shape
    tm = _tile(T, NORM_TILE, BF16_ROWS)
    has_res = branch is not None

    def body(*refs):
        if has_res:
            x_ref, br_ref, g_ref, a_ref, sh_ref, xo_ref, h_ref = refs
            xv = x_ref[...] + g_ref[...] * br_ref[...]
            xo_ref[...] = xv
        else:
            x_ref, a_ref, sh_ref, h_ref = refs
            xv = x_ref[...]
        r = lax.rsqrt(jnp.mean(xv * xv, axis=-1, keepdims=True) + RMS_EPS)
        h_ref[...] = (xv * r * a_ref[...] + sh_ref[...]).astype(h_ref.dtype)

    row = pl.BlockSpec((tm, D), lambda i: (i, 0))
    vec = pl.BlockSpec((1, D), lambda i: (0, 0))
    if has_res:
        xo, h = pl.pallas_call(
            body, name=name, grid=(T // tm,), in_specs=[row, row, vec, vec, vec], out_specs=[row, row],
            out_shape=[jax.ShapeDtypeStruct((T, D), _F32), jax.ShapeDtypeStruct((T, D), _MM)],
            compiler_params=_cparams(("parallel",)))(x, branch, gate, a, sh)
        return xo, h
    h = pl.pallas_call(
        body, name=name, grid=(T // tm,), in_specs=[row, vec, vec], out_specs=row,
        out_shape=jax.ShapeDtypeStruct((T, D), _MM),
        compiler_params=_cparams(("parallel",)))(x, a, sh)
    return x, h


def norm_bwd(x, dh, dx_in, a, *, name):
    T, D = x.shape
    tm = _tile(T, NORM_TILE, BF16_ROWS)

    def body(x_ref, dh_ref, dxi_ref, a_ref, dxo_ref, dsh_ref, da_ref):
        i = pl.program_id(0)
        xv = x_ref[...]
        r = lax.rsqrt(jnp.mean(xv * xv, axis=-1, keepdims=True) + RMS_EPS)
        n = xv * r
        dhv = dh_ref[...].astype(_F32)
        dn = dhv * a_ref[...]
        dxo_ref[...] = dxi_ref[...] + r * (dn - n * jnp.mean(dn * n, axis=-1, keepdims=True))

        @pl.when(i == 0)
        def _():
            dsh_ref[...] = jnp.zeros_like(dsh_ref)
            da_ref[...] = jnp.zeros_like(da_ref)

        dsh_ref[...] += _colsum(dhv)
        da_ref[...] += _colsum(dhv * n)

    row = pl.BlockSpec((tm, D), lambda i: (i, 0))
    vec = pl.BlockSpec((1, D), lambda i: (0, 0))
    return pl.pallas_call(
        body, name=name, grid=(T // tm,), in_specs=[row, row, row, vec], out_specs=[row, vec, vec],
        out_shape=[jax.ShapeDtypeStruct((T, D), _F32), jax.ShapeDtypeStruct((1, D), _F32),
                   jax.ShapeDtypeStruct((1, D), _F32)],
        compiler_params=_cparams(("arbitrary",)))(x, dh, dx_in, a)


def gate_bwd(dx, branch, gate, *, name):
    T, D = dx.shape
    tm = _tile(T, NORM_TILE, BF16_ROWS)

    def body(dx_ref, br_ref, g_ref, db_ref, dg_ref):
        i = pl.program_id(0)
        dxv = dx_ref[...]
        db_ref[...] = (dxv * g_ref[...]).astype(db_ref.dtype)

        @pl.when(i == 0)
        def _():
            dg_ref[...] = jnp.zeros_like(dg_ref)

        dg_ref[...] += _colsum(dxv * br_ref[...])

    row = pl.BlockSpec((tm, D), lambda i: (i, 0))
    vec = pl.BlockSpec((1, D), lambda i: (0, 0))
    return pl.pallas_call(
        body, name=name, grid=(T // tm,), in_specs=[row, row, vec], out_specs=[row, vec],
        out_shape=[jax.ShapeDtypeStruct((T, D), _MM), jax.ShapeDtypeStruct((1, D), _F32)],
        compiler_params=_cparams(("arbitrary",)))(dx, branch, gate)


def final_loss_bwd(x, branch, gate, gfin, tgt, *, name):
    T, D = x.shape
    tm = _tile(T, NORM_TILE, BF16_ROWS)

    def body(x_ref, br_ref, g_ref, gf_ref, t_ref, dx_ref, loss_ref, dgf_ref):
        i = pl.program_id(0)
        xv = x_ref[...] + g_ref[...] * br_ref[...]
        r = lax.rsqrt(jnp.mean(xv * xv, axis=-1, keepdims=True) + RMS_EPS)
        n = xv * r
        e = n * gf_ref[...] - t_ref[...]
        dy = e * (1.0 / D)
        dn = dy * gf_ref[...]
        dx_ref[...] = r * (dn - n * jnp.mean(dn * n, axis=-1, keepdims=True))

        @pl.when(i == 0)
        def _():
            loss_ref[...] = jnp.zeros_like(loss_ref)
            dgf_ref[...] = jnp.zeros_like(dgf_ref)

        per_row = jnp.mean(e * e, axis=-1, keepdims=True)
        loss_ref[...] += jnp.broadcast_to(0.5 * _colsum(per_row), loss_ref.shape)
        dgf_ref[...] += _colsum(dy * n)

    row = pl.BlockSpec((tm, D), lambda i: (i, 0))
    vec = pl.BlockSpec((1, D), lambda i: (0, 0))
    lvec = pl.BlockSpec((1, LANES), lambda i: (0, 0))
    return pl.pallas_call(
        body, name=name, grid=(T // tm,), in_specs=[row, row, vec, vec, row], out_specs=[row, lvec, vec],
        out_shape=[jax.ShapeDtypeStruct((T, D), _F32), jax.ShapeDtypeStruct((1, LANES), _F32),
                   jax.ShapeDtypeStruct((1, D), _F32)],
        compiler_params=_cparams(("arbitrary",)))(x, branch, gate, gfin, tgt)


def _log_sigmoid(x):
    return jnp.minimum(x, 0.0) - jnp.log(1.0 + jnp.exp(-jnp.abs(x)))


def fgate_fwd(flog, bf, *, name):
    T = flog.shape[0]
    tt = _tile(T, 256)

    def body(x_ref, b_ref, f_ref, carry):
        i = pl.program_id(0)

        @pl.when(i == 0)
        def _():
            carry[...] = jnp.zeros_like(carry)

        lf = _log_sigmoid(x_ref[...] + b_ref[...])
        rows = lax.broadcasted_iota(jnp.int32, (tt, tt), 0)
        cols = lax.broadcasted_iota(jnp.int32, (tt, tt), 1)
        tri = (cols <= rows).astype(_F32)
        f_ref[...] = jnp.dot(tri, lf, preferred_element_type=_F32, precision=HIGHEST) + carry[0:1, :]
        carry[0:1, :] = f_ref[tt - 1:tt, :]

    return pl.pallas_call(
        body, name=name, grid=(T // tt,),
        in_specs=[pl.BlockSpec((tt, LANES), lambda i: (i, 0)), pl.BlockSpec((1, LANES), lambda i: (0, 0))],
        out_specs=pl.BlockSpec((tt, LANES), lambda i: (i, 0)),
        out_shape=jax.ShapeDtypeStruct((T, LANES), _F32),
        scratch_shapes=[pltpu.VMEM((SUBLANES, LANES), _F32)],
        compiler_params=_cparams(("arbitrary",)))(flog, bf)


def fgate_bwd(dfk, flog, bf, *, name):
    T = flog.shape[0]
    tt = _tile(T, 256)
    nb = T // tt

    def body(d_ref, x_ref, b_ref, o_ref, db_ref, carry):
        i = pl.program_id(0)

        @pl.when(i == 0)
        def _():
            carry[...] = jnp.zeros_like(carry)
            db_ref[...] = jnp.zeros_like(db_ref)

        rows = lax.broadcasted_iota(jnp.int32, (tt, tt), 0)
        cols = lax.broadcasted_iota(jnp.int32, (tt, tt), 1)
        upper = (cols >= rows).astype(_F32)
        dlf = jnp.dot(upper, d_ref[...], preferred_element_type=_F32, precision=HIGHEST) + carry[0:1, :]
        carry[0:1, :] = dlf[0:1, :]
        dfl = dlf * _sigmoid(-(x_ref[...] + b_ref[...]))
        o_ref[...] = dfl.astype(o_ref.dtype)
        db_ref[...] += _colsum(dfl)

    rev = pl.BlockSpec((tt, LANES), lambda i: (nb - 1 - i, 0))
    vec = pl.BlockSpec((1, LANES), lambda i: (0, 0))
    return pl.pallas_call(
        body, name=name, grid=(nb,), in_specs=[rev, rev, vec], out_specs=[rev, vec],
        out_shape=[jax.ShapeDtypeStruct((T, LANES), _MM), jax.ShapeDtypeStruct((1, LANES), _F32)],
        scratch_shapes=[pltpu.VMEM((SUBLANES, LANES), _F32)],
        compiler_params=_cparams(("arbitrary",)))(dfk, flog, bf)


def _att_scores(q, k, fq, fk, rep, masked):
    s = lax.dot_general(q, k, (((1,), (1,)), ((), ())), preferred_element_type=_F32)
    s = s * (HEAD_DIM ** -0.5) + (jnp.tile(fq, (1, rep)) - fk)
    if masked:
        rows = lax.broadcasted_iota(jnp.int32, s.shape, 0)
        cols = lax.broadcasted_iota(jnp.int32, s.shape, 1)
        s = jnp.where(cols <= rows, s, NEG)
    return s


def _fold_q(r, t, nb):
    first = t <= r
    return jnp.where(first, r, nb - 1 - r), jnp.where(first, t, t - r - 1)


def _fold_k(r, t, nb):
    first = t < nb - r
    return jnp.where(first, r, nb - 1 - r), jnp.where(first, r + t, t - 1)


def _first_step():
    return (pl.program_id(0) == 0) & (pl.program_id(1) == 0) & (pl.program_id(2) == 0)


def _step_is(h, r, t):
    return (pl.program_id(0) == h) & (pl.program_id(1) == r) & (pl.program_id(2) == t)


def attn_fwd(proj, fq, fk, *, heads, name, gather=None):
    T = proj.shape[0]
    H = heads
    tb = _tile(T, ATT_BLOCK)
    nb = T // tb
    assert nb % 2 == 0
    rep = tb // LANES
    n_g = len(gather) if gather else 0
    lay = _gather_layout(gather) if gather else None

    def body(*refs):
        q_ref, k_ref, v_ref, fq_ref, fk_ref = refs[:5]
        o_ref, lse_ref = refs[5 + n_g:7 + n_g]
        m_s, l_s, acc_s = refs[7 + 2 * n_g:10 + 2 * n_g]
        i, j = _fold_q(pl.program_id(1), pl.program_id(2), nb)
        if n_g:
            ops = _GatherOps(lay, refs[5:5 + n_g], refs[7 + n_g:7 + 2 * n_g], *refs[10 + 2 * n_g:])
            pl.when(_first_step())(ops.start)
            pl.when(_step_is(H // 2, 0, 0))(ops.forward)

        @pl.when(j == 0)
        def _():
            m_s[...] = jnp.full_like(m_s, NEG)
            l_s[...] = jnp.zeros_like(l_s)
            acc_s[...] = jnp.zeros_like(acc_s)

        def step(masked):
            s = _att_scores(q_ref[...], k_ref[...], fq_ref[...], fk_ref[...], rep, masked)
            m_prev = m_s[...]
            m_new = jnp.maximum(m_prev, jnp.max(s, axis=-1, keepdims=True))
            alpha = jnp.exp(m_prev - m_new)
            p = jnp.exp(s - jnp.tile(m_new, (1, rep)))
            l_s[...] = alpha * l_s[...] + jnp.sum(p, axis=-1, keepdims=True)
            v = v_ref[...]
            acc_s[...] = alpha * acc_s[...] + jnp.dot(p.astype(v.dtype), v, preferred_element_type=_F32)
            m_s[...] = m_new

        @pl.when(j < i)
        def _():
            step(False)

        @pl.when(j == i)
        def _():
            step(True)
            o_ref[...] = (acc_s[...] / l_s[...]).astype(o_ref.dtype)
            lse_ref[...] = m_s[...] + jnp.log(l_s[...])

        if n_g:
            pl.when(_step_is(H - 1, nb // 2 - 1, nb))(ops.finish)

    qi = lambda r, t: _fold_q(r, t, nb)[0]
    kj = lambda r, t: _fold_q(r, t, nb)[1]
    qs = pl.BlockSpec((tb, HEAD_DIM), lambda h, r, t: (qi(r, t), h))
    ks = pl.BlockSpec((tb, HEAD_DIM), lambda h, r, t: (kj(r, t), H + h))
    vs = pl.BlockSpec((tb, HEAD_DIM), lambda h, r, t: (kj(r, t), 2 * H + h))
    fqs = pl.BlockSpec((None, tb, LANES), lambda h, r, t: (h, qi(r, t), 0))
    fks = pl.BlockSpec((None, 1, tb), lambda h, r, t: (h, 0, kj(r, t)))
    hbm = pl.BlockSpec(memory_space=pl.ANY)
    outs = pl.pallas_call(
        body, name=name, grid=(H, nb // 2, nb + 1),
        in_specs=[qs, ks, vs, fqs, fks] + [hbm] * n_g,
        out_specs=[qs, fqs] + [hbm] * n_g,
        out_shape=[jax.ShapeDtypeStruct((T, H * HEAD_DIM), _MM), jax.ShapeDtypeStruct((H, T, LANES), _F32)]
        + (_gather_out_shapes(lay, gather[0].dtype) if n_g else []),
        scratch_shapes=[pltpu.VMEM((tb, LANES), _F32), pltpu.VMEM((tb, LANES), _F32),
                        pltpu.VMEM((tb, HEAD_DIM), _F32)] + (_comm_sems(n_g) if n_g else []),
        compiler_params=_cparams(("arbitrary",) * 3 if n_g else ("parallel", "parallel", "arbitrary")))(
            proj, proj, proj, fq, fk, *(gather or []))
    return outs[0], outs[1], outs[2:]


def _att_p_ds(q, k, v, do, o, fq, fk, lse, rep, masked):
    s = _att_scores(q, k, fq, fk, rep, masked)
    p = jnp.exp(s - jnp.tile(lse, (1, rep)))
    delta = jnp.sum(do.astype(_F32) * o.astype(_F32), axis=-1, keepdims=True)
    dp = lax.dot_general(do, v, (((1,), (1,)), ((), ())), preferred_element_type=_F32)
    ds = p * (dp - delta)
    return p, ds


def attn_bwd(proj, dcat, attn, fq, fk, lse, *, heads, name, scatter=None):
    T = proj.shape[0]
    H = heads
    tb = _tile(T, ATT_BLOCK)
    nb = T // tb
    assert nb % 2 == 0
    rep = tb // LANES
    scale = HEAD_DIM ** -0.5
    n_s = len(scatter) if scatter else 0
    lay = _scatter_layout(scatter) if scatter else None

    def body(*refs):
        q_ref, k_ref, v_ref, do_ref, o_ref, fq_ref, fk_ref, lse_ref = refs[:8]
        dq_ref, dk_ref, dv_ref, dfq_ref, dfk_ref = refs[8 + n_s:13 + n_s]
        dk_s, dv_s, dfk_s, dq_s, dfq_s = refs[13 + 2 * n_s:18 + 2 * n_s]
        r, t = pl.program_id(1), pl.program_id(2)
        j, i = _fold_k(r, t, nb)
        if n_s:
            ops = _ScatterOps(lay, refs[8:8 + n_s], refs[13 + n_s:13 + 2 * n_s], *refs[18 + 2 * n_s:])
            pl.when(_first_step())(ops.start)

        @pl.when((r == 0) & (t == 0))
        def _():
            dq_s[...] = jnp.zeros_like(dq_s)
            dfq_s[...] = jnp.zeros_like(dfq_s)

        @pl.when(i == j)
        def _():
            dk_s[...] = jnp.zeros_like(dk_s)
            dv_s[...] = jnp.zeros_like(dv_s)
            dfk_s[...] = jnp.zeros_like(dfk_s)

        def step(masked):
            q = q_ref[...]
            k = k_ref[...]
            do = do_ref[...]
            p, ds = _att_p_ds(q, k, v_ref[...], do, o_ref[...], fq_ref[...], fk_ref[...],
                              lse_ref[...], rep, masked)
            dsm = ds.astype(q.dtype)
            tn = (((0,), (0,)), ((), ()))
            dv_s[...] += lax.dot_general(p.astype(do.dtype), do, tn, preferred_element_type=_F32)
            dk_s[...] += lax.dot_general(dsm, q, tn, preferred_element_type=_F32)
            dfk_s[0:1, :] += -_colsum(ds)
            rows = pl.ds(pl.multiple_of(i * tb, tb), tb)
            dq_s[rows, :] += jnp.dot(dsm, k, preferred_element_type=_F32)
            dfq_s[rows, :] += jnp.sum(ds, axis=-1, keepdims=True)

        @pl.when(i > j)
        def _():
            step(False)

        @pl.when(i == j)
        def _():
            step(True)

        @pl.when(i == nb - 1)
        def _():
            dk_ref[...] = (dk_s[...] * scale).astype(dk_ref.dtype)
            dv_ref[...] = dv_s[...].astype(dv_ref.dtype)
            dfk_ref[...] = dfk_s[0:1, :]

        @pl.when((r == nb // 2 - 1) & (t == nb))
        def _():
            dq_ref[...] = (dq_s[...] * scale).astype(dq_ref.dtype)
            dfq_ref[...] = dfq_s[...]

        if n_s:
            pl.when(_step_is(H - 1, nb // 2 - 1, nb))(ops.finish)

    kj = lambda r, t: _fold_k(r, t, nb)[0]
    qi = lambda r, t: _fold_k(r, t, nb)[1]
    qs = pl.BlockSpec((tb, HEAD_DIM), lambda h, r, t: (qi(r, t), h))
    ks = pl.BlockSpec((tb, HEAD_DIM), lambda h, r, t: (kj(r, t), H + h))
    vs = pl.BlockSpec((tb, HEAD_DIM), lambda h, r, t: (kj(r, t), 2 * H + h))
    stat = pl.BlockSpec((None, tb, LANES), lambda h, r, t: (h, qi(r, t), 0))
    fks = pl.BlockSpec((None, 1, tb), lambda h, r, t: (h, 0, kj(r, t)))
    kout = pl.BlockSpec((tb, HEAD_DIM), lambda h, r, t: (kj(r, t), h))
    head_q = pl.BlockSpec((T, HEAD_DIM), lambda h, r, t: (0, h))
    head_stat = pl.BlockSpec((None, T, LANES), lambda h, r, t: (h, 0, 0))
    hbm = pl.BlockSpec(memory_space=pl.ANY)
    A = H * HEAD_DIM
    outs = pl.pallas_call(
        body, name=name, grid=(H, nb // 2, nb + 1),
        in_specs=[qs, ks, vs, qs, qs, stat, fks, stat] + [hbm] * n_s,
        out_specs=[head_q, kout, kout, head_stat, fks] + [hbm] * n_s,
        out_shape=[jax.ShapeDtypeStruct((T, A), _MM), jax.ShapeDtypeStruct((T, A), _MM),
                   jax.ShapeDtypeStruct((T, A), _MM), jax.ShapeDtypeStruct((H, T, LANES), _F32),
                   jax.ShapeDtypeStruct((H, 1, T), _F32)]
        + (_scatter_out_shapes(lay, scatter[0].dtype) if n_s else []),
        scratch_shapes=[pltpu.VMEM((tb, HEAD_DIM), _F32), pltpu.VMEM((tb, HEAD_DIM), _F32),
                        pltpu.VMEM((SUBLANES, tb), _F32), pltpu.VMEM((T, HEAD_DIM), _F32),
                        pltpu.VMEM((T, LANES), _F32)] + (_comm_sems(n_s) if n_s else []),
        compiler_params=_cparams(("arbitrary",) * 3 if n_s else ("parallel", "arbitrary", "arbitrary")))(
            proj, proj, proj, dcat, attn, fq, fk, lse, *(scatter or []))
    return (*outs[:5], outs[5:])


def _causal_taps(w_ref, scr, halo, tt, width):
    acc = w_ref[width - 1:width, :] * scr[halo:halo + tt, :]
    for j in range(1, width):
        acc = acc + w_ref[width - 1 - j:width - j, :] * scr[halo - j:halo - j + tt, :]
    return acc


def _anticausal_taps(w_ref, scr, tt, width):
    acc = w_ref[width - 1:width, :] * scr[0:tt, :]
    for j in range(1, width):
        acc = acc + w_ref[width - 1 - j:width - j, :] * scr[j:j + tt, :]
    return acc


def _ln_fwd(cc, g, b):
    mu = jnp.mean(cc, axis=-1, keepdims=True)
    xc = cc - mu
    rstd = lax.rsqrt(jnp.mean(xc * xc, axis=-1, keepdims=True) + LN_EPS)
    xhat = xc * rstd
    return xhat, rstd, xhat * g + b


def _ln_silu_bwd(cc, dconf, g, b):
    xhat, rstd, ln = _ln_fwd(cc, g, b)
    s = _sigmoid(ln)
    dln = dconf * (s * (1.0 + ln * (1.0 - s)))
    dxh = dln * g
    dcc = rstd * (dxh - jnp.mean(dxh, axis=-1, keepdims=True)
                  - xhat * jnp.mean(dxh * xhat, axis=-1, keepdims=True))
    return dcc, dln, xhat


def mixer_misc_fwd(proj, cw, cb, lg, lb, sw, *, width, base_col, name):
    T = proj.shape[0]
    C = width
    tt = _tile(T, CONV_TILE)
    HB = CONV_HALO
    per = tt // HB
    KC, KS = cw.shape[0], sw.shape[0]
    b0 = base_col // C

    def body(cv, cg, sx, sb, sc, cvh, cgh, sxh, sch, cw_ref, cb_ref, lg_ref, lb_ref, sw_ref,
             cm_ref, cc_ref, zc_ref, gscr, zscr):
        keep = (pl.program_id(0) > 0).astype(_F32)
        f = lambda r: r[...].astype(_F32)
        gscr[0:HB, :] = f(cvh) * _sigmoid(f(cgh)) * keep
        gscr[HB:HB + tt, :] = f(cv) * _sigmoid(f(cg))
        cc = _causal_taps(cw_ref, gscr, HB, tt, KC) + cb_ref[...]
        cc_ref[...] = cc
        _, _, ln = _ln_fwd(cc, lg_ref[...], lb_ref[...])
        cm_ref[:, 0:C] = (ln * _sigmoid(ln)).astype(cm_ref.dtype)
        zscr[0:HB, :] = f(sch) * f(sxh) * keep
        zscr[HB:HB + tt, :] = f(sc) * f(sx)
        zc = _causal_taps(sw_ref, zscr, HB, tt, KS)
        zc_ref[...] = zc
        cm_ref[:, C:2 * C] = (f(sb) * zc).astype(cm_ref.dtype)

    main = lambda k: pl.BlockSpec((tt, C), lambda i: (i, b0 + k))
    halo = lambda k: pl.BlockSpec((HB, C), lambda i: (jnp.maximum(i * per - 1, 0), b0 + k))
    full = lambda a: pl.BlockSpec(a.shape, lambda i: (0, 0))
    return pl.pallas_call(
        body, name=name, grid=(T // tt,),
        in_specs=[main(0), main(1), main(2), main(3), main(4), halo(0), halo(1), halo(2), halo(4),
                  full(cw), full(cb), full(lg), full(lb), full(sw)],
        out_specs=[pl.BlockSpec((tt, 2 * C), lambda i: (i, 0)), pl.BlockSpec((tt, C), lambda i: (i, 0)),
                   pl.BlockSpec((tt, C), lambda i: (i, 0))],
        out_shape=[jax.ShapeDtypeStruct((T, 2 * C), _MM), jax.ShapeDtypeStruct((T, C), _F32),
                   jax.ShapeDtypeStruct((T, C), _F32)],
        scratch_shapes=[pltpu.VMEM((tt + HB, C), _F32), pltpu.VMEM((tt + HB, C), _F32)],
        compiler_params=_cparams(("parallel",)))(
            proj, proj, proj, proj, proj, proj, proj, proj, proj, cw, cb, lg, lb, sw)


def mixer_misc_bwd(proj, dcat, cc, zc, cw, lg, lb, sw, *, width, base_col, dbase_col, name):
    T = proj.shape[0]
    C = width
    tt = _tile(T, CONV_TILE)
    nt = T // tt
    HB = CONV_HALO
    per = tt // HB
    KC, KS = cw.shape[0], sw.shape[0]
    b0 = base_col // C
    d0 = dbase_col // C
    last_hb = T // HB - 1

    def body(cv, cg, sx, sb, sc, cvh, cgh, sxh, sch, sbn, dcf, dsv, dcfn, dsvn, cc_ref, ccn_ref, zc_ref,
             cw_ref, lg_ref, lb_ref, sw_ref,
             dm_ref, dcw_ref, dcb_ref, dlg_ref, dlb_ref, dsw_ref, gscr, dscr, zscr, zdscr):
        i = pl.program_id(0)
        keep = (i > 0).astype(_F32)
        ahead = (i < nt - 1).astype(_F32)
        f = lambda r: r[...].astype(_F32)

        @pl.when(i == 0)
        def _():
            for r in (dcw_ref, dcb_ref, dlg_ref, dlb_ref, dsw_ref):
                r[...] = jnp.zeros_like(r)

        g, b = lg_ref[...], lb_ref[...]
        dcc, dln, xhat = _ln_silu_bwd(cc_ref[...], f(dcf), g, b)
        dcc_next, _, _ = _ln_silu_bwd(ccn_ref[...], f(dcfn), g, b)
        dlg_ref[...] += _colsum(dln * xhat)
        dlb_ref[...] += _colsum(dln)
        dcb_ref[...] += _colsum(dcc)
        dscr[0:tt, :] = dcc
        dscr[tt:tt + HB, :] = dcc_next * ahead
        dglu = _anticausal_taps(cw_ref, dscr, tt, KC)
        cvv = f(cv)
        sig = _sigmoid(f(cg))
        dm_ref[:, 0:C] = (dglu * sig).astype(dm_ref.dtype)
        dm_ref[:, C:2 * C] = (dglu * cvv * sig * (1.0 - sig)).astype(dm_ref.dtype)
        gscr[0:HB, :] = f(cvh) * _sigmoid(f(cgh)) * keep
        gscr[HB:HB + tt, :] = cvv * sig
        for j in range(KC):
            dcw_ref[KC - 1 - j:KC - j, :] += _colsum(dcc * gscr[HB - j:HB - j + tt, :])
        dsc_out = f(dsv)
        sbv = f(sb)
        dzc = dsc_out * sbv
        zdscr[0:tt, :] = dzc
        zdscr[tt:tt + HB, :] = f(dsvn) * f(sbn) * ahead
        dz = _anticausal_taps(sw_ref, zdscr, tt, KS)
        sxv, scv = f(sx), f(sc)
        dm_ref[:, 2 * C:3 * C] = (dz * scv).astype(dm_ref.dtype)
        dm_ref[:, 3 * C:4 * C] = (dsc_out * zc_ref[...]).astype(dm_ref.dtype)
        dm_ref[:, 4 * C:5 * C] = (dz * sxv).astype(dm_ref.dtype)
        zscr[0:HB, :] = f(sch) * f(sxh) * keep
        zscr[HB:HB + tt, :] = scv * sxv
        for j in range(KS):
            dsw_ref[KS - 1 - j:KS - j, :] += _colsum(dzc * zscr[HB - j:HB - j + tt, :])

    main = lambda col: pl.BlockSpec((tt, C), lambda i: (i, col))
    prev = lambda col: pl.BlockSpec((HB, C), lambda i: (jnp.maximum(i * per - 1, 0), col))
    nxt = lambda col: pl.BlockSpec((HB, C), lambda i: (jnp.minimum((i + 1) * per, last_hb), col))
    full = lambda a: pl.BlockSpec(a.shape, lambda i: (0, 0))
    vec = pl.BlockSpec((1, C), lambda i: (0, 0))
    return pl.pallas_call(
        body, name=name, grid=(nt,),
        in_specs=[main(b0), main(b0 + 1), main(b0 + 2), main(b0 + 3), main(b0 + 4),
                  prev(b0), prev(b0 + 1), prev(b0 + 2), prev(b0 + 4), nxt(b0 + 3),
                  main(d0), main(d0 + 1), nxt(d0), nxt(d0 + 1),
                  main(0), nxt(0), main(0),
                  full(cw), full(lg), full(lb), full(sw)],
        out_specs=[pl.BlockSpec((tt, 5 * C), lambda i: (i, 0)), full(cw), vec, vec, vec, full(sw)],
        out_shape=[jax.ShapeDtypeStruct((T, 5 * C), _MM), jax.ShapeDtypeStruct(cw.shape, _F32),
                   jax.ShapeDtypeStruct((1, C), _F32), jax.ShapeDtypeStruct((1, C), _F32),
                   jax.ShapeDtypeStruct((1, C), _F32), jax.ShapeDtypeStruct(sw.shape, _F32)],
        scratch_shapes=[pltpu.VMEM((tt + HB, C), _F32)] * 4,
        compiler_params=_cparams(("arbitrary",)))(
            proj, proj, proj, proj, proj, proj, proj, proj, proj, proj,
            dcat, dcat, dcat, dcat, cc, cc, zc, cw, lg, lb, sw)


def _ffn_u(main_ref, halo_ref, w_ref, b_ref, scr, keep, tt, width):
    HB = FFN_HALO
    scr[0:HB, :] = halo_ref[...].astype(_F32) * keep
    scr[HB:HB + tt, :] = main_ref[...].astype(_F32)
    return _causal_taps(w_ref, scr, HB, tt, width) + b_ref[...]


def ffn_act_fwd(up, w, b, *, name):
    T, F2 = up.shape
    F = F2 // 2
    K = w.shape[0]
    tt = _tile(T, CONV_TILE)
    tc = _tile(F, 512)
    nb = F // tc
    per = tt // FFN_HALO

    def body(g_ref, v_ref, gh_ref, vh_ref, wg_ref, wv_ref, bg_ref, bv_ref, o_ref, gscr, vscr):
        keep = (pl.program_id(0) > 0).astype(_F32)
        ug = _ffn_u(g_ref, gh_ref, wg_ref, bg_ref, gscr, keep, tt, K)
        uv = _ffn_u(v_ref, vh_ref, wv_ref, bv_ref, vscr, keep, tt, K)
        o_ref[...] = (ug * _sigmoid(ug) * uv).astype(o_ref.dtype)

    main = lambda off: pl.BlockSpec((tt, tc), lambda i, j: (i, j + off))
    halo = lambda off: pl.BlockSpec((FFN_HALO, tc), lambda i, j: (jnp.maximum(i * per - 1, 0), j + off))
    wsp = lambda off: pl.BlockSpec((K, tc), lambda i, j: (0, j + off))
    bsp = lambda off: pl.BlockSpec((1, tc), lambda i, j: (0, j + off))
    return pl.pallas_call(
        body, name=name, grid=(T // tt, nb),
        in_specs=[main(0), main(nb), halo(0), halo(nb), wsp(0), wsp(nb), bsp(0), bsp(nb)],
        out_specs=pl.BlockSpec((tt, tc), lambda i, j: (i, j)),
        out_shape=jax.ShapeDtypeStruct((T, F), _MM),
        scratch_shapes=[pltpu.VMEM((tt + FFN_HALO, tc), _F32)] * 2,
        compiler_params=_cparams(("parallel", "parallel")))(up, up, up, up, w, w, b, b)


def ffn_bwd_du(up, dact, w, b, *, name):
    T, F2 = up.shape
    F = F2 // 2
    K = w.shape[0]
    tt = _tile(T, CONV_TILE)
    tc = _tile(F, 512)
    nb = F // tc
    per = tt // FFN_HALO
    HB = FFN_HALO

    def body(g_ref, v_ref, gh_ref, vh_ref, da_ref, wg_ref, wv_ref, bg_ref, bv_ref, du_ref, dwb_ref,
             gscr, vscr):
        i = pl.program_id(1)
        keep = (i > 0).astype(_F32)
        ug = _ffn_u(g_ref, gh_ref, wg_ref, bg_ref, gscr, keep, tt, K)
        uv = _ffn_u(v_ref, vh_ref, wv_ref, bv_ref, vscr, keep, tt, K)
        s = _sigmoid(ug)
        da = da_ref[...].astype(_F32)
        du_g = da * uv * s * (1.0 + ug * (1.0 - s))
        du_v = da * ug * s
        du_ref[0] = du_g.astype(du_ref.dtype)
        du_ref[1] = du_v.astype(du_ref.dtype)

        @pl.when(i == 0)
        def _():
            dwb_ref[...] = jnp.zeros_like(dwb_ref)

        for half, (du, scr) in enumerate(((du_g, gscr), (du_v, vscr))):
            for j in range(K):
                dwb_ref[half, K - 1 - j:K - j, :] += _colsum(du * scr[HB - j:HB - j + tt, :])
            dwb_ref[half, K:K + 1, :] += _colsum(du)

    main = lambda off: pl.BlockSpec((tt, tc), lambda j, i: (i, j + off))
    halo = lambda off: pl.BlockSpec((HB, tc), lambda j, i: (jnp.maximum(i * per - 1, 0), j + off))
    wsp = lambda off: pl.BlockSpec((K, tc), lambda j, i: (0, j + off))
    bsp = lambda off: pl.BlockSpec((1, tc), lambda j, i: (0, j + off))
    return pl.pallas_call(
        body, name=name, grid=(nb, T // tt),
        in_specs=[main(0), main(nb), halo(0), halo(nb), main(0), wsp(0), wsp(nb), bsp(0), bsp(nb)],
        out_specs=[pl.BlockSpec((2, tt, tc), lambda j, i: (0, i, j)),
                   pl.BlockSpec((2, SUBLANES, tc), lambda j, i: (0, 0, j))],
        out_shape=[jax.ShapeDtypeStruct((2, T, F), _MM), jax.ShapeDtypeStruct((2, SUBLANES, F), _F32)],
        scratch_shapes=[pltpu.VMEM((tt + HB, tc), _F32)] * 2,
        compiler_params=_cparams(("parallel", "arbitrary")))(up, up, up, up, dact, w, w, b, b)


def dwconv_transpose(du, w, *, name):
    _, T, F = du.shape
    K = w.shape[0]
    tt = _tile(T, CONV_TILE)
    tc = _tile(F, 1408)
    nb = F // tc
    per = tt // FFN_HALO
    HB = FFN_HALO
    nt = T // tt
    last_hb = T // HB - 1

    def body(d_ref, dn_ref, w_ref, o_ref, scr):
        ahead = (pl.program_id(1) < nt - 1).astype(_F32)
        scr[0:tt, :] = d_ref[...].astype(_F32)
        scr[tt:tt + HB, :] = dn_ref[...].astype(_F32) * ahead
        o_ref[...] = _anticausal_taps(w_ref, scr, tt, K).astype(o_ref.dtype)

    return pl.pallas_call(
        body, name=name, grid=(2, nt, nb),
        in_specs=[pl.BlockSpec((None, tt, tc), lambda s, i, j: (s, i, j)),
                  pl.BlockSpec((None, HB, tc), lambda s, i, j: (s, jnp.minimum((i + 1) * per, last_hb), j)),
                  pl.BlockSpec((K, tc), lambda s, i, j: (0, s * nb + j))],
        out_specs=pl.BlockSpec((tt, tc), lambda s, i, j: (i, s * nb + j)),
        out_shape=jax.ShapeDtypeStruct((T, 2 * F), _MM),
        scratch_shapes=[pltpu.VMEM((tt + HB, tc), _F32)],
        compiler_params=_cparams(("parallel", "parallel", "parallel")))(du, du, w)


def _adamw_math(w, g, m, v):
    m = ADAM_B1 * m + (1.0 - ADAM_B1) * g
    v = ADAM_B2 * v + (1.0 - ADAM_B2) * (g * g)
    m_hat = m / (1.0 - ADAM_B1 ** ADAM_STEP)
    v_hat = v / (1.0 - ADAM_B2 ** ADAM_STEP)
    delta = -ADAM_LR * (m_hat / (jnp.sqrt(v_hat) + ADAM_EPS) + ADAM_WD * w)
    return delta, m, v


def _as2d(a):
    return a.reshape(1, -1) if a.ndim == 1 else a.reshape(-1, a.shape[-1])


def adamw(w, g, m, v, *, name):
    shape = w.shape
    w2, g2, m2, v2 = _as2d(w), _as2d(g), _as2d(m), _as2d(v)
    R, C = w2.shape
    lanes = -(-C // LANES) * LANES
    tr = _tile(R, max(SUBLANES, (1 << 20) // (4 * lanes)), SUBLANES)

    def body(w_ref, g_ref, m_ref, v_ref, d_ref, mo_ref, vo_ref):
        d, mn, vn = _adamw_math(w_ref[...], g_ref[...], m_ref[...], v_ref[...])
        d_ref[...] = d
        mo_ref[...] = mn
        vo_ref[...] = vn

    blk = pl.BlockSpec((tr, C), lambda i: (i, 0))
    outs = pl.pallas_call(
        body, name=name, grid=(R // tr,), in_specs=[blk] * 4, out_specs=[blk] * 3,
        out_shape=[jax.ShapeDtypeStruct((R, C), _F32)] * 3,
        compiler_params=_cparams(("parallel",)))(w2, g2, m2, v2)
    return tuple(o.reshape(shape) for o in outs)


def ada_w_update(c_t, d_ada, w, m, v, *, name):
    L, D, N = w.shape
    B = c_t.shape[1]
    tr = _tile(D, 256, SUBLANES)
    tn = _tile(N, 1024)

    def body(c_ref, a_ref, w_ref, m_ref, v_ref, g_ref, d_ref, mo_ref, vo_ref):
        cv = c_ref[...]
        cv = cv * _sigmoid(cv)
        g = jnp.dot(cv, a_ref[...], preferred_element_type=_F32, precision=HIGHEST)
        g_ref[...] = g
        d, mn, vn = _adamw_math(w_ref[...], g, m_ref[...], v_ref[...])
        d_ref[...] = d
        mo_ref[...] = mn
        vo_ref[...] = vn

    blk = pl.BlockSpec((None, tr, tn), lambda l, i, j: (l, i, j))
    return pl.pallas_call(
        body, name=name, grid=(L, D // tr, N // tn),
        in_specs=[pl.BlockSpec((tr, B), lambda l, i, j: (i, 0)),
                  pl.BlockSpec((None, B, tn), lambda l, i, j: (l, 0, j)), blk, blk, blk],
        out_specs=[blk] * 4,
        out_shape=[jax.ShapeDtypeStruct((L, D, N), _F32)] * 4,
        compiler_params=_cparams(("parallel", "parallel", "parallel")))(c_t, d_ada, w, m, v)


def _coords():
    return lax.axis_index("x"), lax.axis_index("y"), lax.axis_index("c")


def allgather_small(x, *, with_sum, name):
    R, C = x.shape

    def body(x_ref, out_ref, *rest):
        if with_sum:
            sum_ref, send_sems, recv_sems, local_sem = rest
        else:
            send_sems, recv_sems, local_sem = rest
        px, py, pc = _coords()
        me, sibling = (px, py, pc), (px, py, 1 - pc)
        chips = [(1 - px, py), (px, 1 - py), (1 - px, 1 - py)]

        def rows(qx, qy, qc):
            return out_ref.at[4 * qx + 2 * qy + qc]

        def copy(k, block, to, src=None):
            return pltpu.make_async_remote_copy(
                src_ref=rows(*block) if src is None else src, dst_ref=rows(*block),
                send_sem=send_sems.at[k], recv_sem=recv_sems.at[k], device_id=to, device_id_type=MESH)

        mine = pltpu.make_async_copy(x_ref, rows(*me), local_sem)
        mine.start()
        first = [copy(0, me, sibling, src=x_ref)]
        first += [copy(1 + j, me, (*chip, pc), src=x_ref) for j, chip in enumerate(chips)]
        for cp in first:
            cp.start()
        passed = [copy(4 + j, (*chip, pc), sibling) for j, chip in enumerate(chips)]
        for j, chip in enumerate(chips):
            copy(1 + j, (*chip, pc), me).wait_recv()
            passed[j].start()
        copy(0, sibling, me).wait_recv()
        for j, chip in enumerate(chips):
            copy(4 + j, (*chip, 1 - pc), me).wait_recv()
        for cp in first + passed:
            cp.wait_send()
        mine.wait()
        if with_sum:
            acc = out_ref[0]
            for k in range(1, N_DEV):
                acc = acc + out_ref[k]
            sum_ref[...] = acc

    vm = pl.BlockSpec(memory_space=pltpu.VMEM)
    out_shape = [jax.ShapeDtypeStruct((N_DEV, R, C), x.dtype)]
    if with_sum:
        out_shape.append(jax.ShapeDtypeStruct((R, C), x.dtype))
    outs = pl.pallas_call(
        body, name=name, in_specs=[vm], out_specs=[vm] * len(out_shape), out_shape=out_shape,
        scratch_shapes=[pltpu.SemaphoreType.DMA((7,)), pltpu.SemaphoreType.DMA((7,)), pltpu.SemaphoreType.DMA],
        compiler_params=pltpu.CompilerParams(vmem_limit_bytes=VMEM_LIMIT_V7X))(x)
    return outs if with_sum else outs[0]


def _at(start, size, align):
    return pl.ds(pl.multiple_of(start, align) if align > 1 else start, size)


class _BigLayout:
    def __init__(self, D, INs, Ds, F2s, Fs):
        self.D, self.INs, self.Ds, self.F2s, self.Fs = D, INs, Ds, F2s, Fs
        self.Dh, self.Dsh, self.Fsh = D // 2, Ds // 2, Fs // 2
        self.piece_shapes = [(self.Dh, INs), (self.Dsh, D), (self.Dh, F2s), (self.Fsh, D)]

    def in_full(self, a, ref, k, h):
        if a == 0:
            return ref.at[k, _at(h * self.Dh, self.Dh, self.Dh), :]
        if a == 1:
            return ref.at[_at(k * self.Ds + h * self.Dsh, self.Dsh, self.Dsh), :]
        if a == 2:
            return ref.at[_at(h * self.Dh, self.Dh, self.Dh), _at(k * self.F2s, self.F2s, self.F2s)]
        return ref.at[_at(k * self.Fs + h * self.Fsh, self.Fsh, self.Fsh), :]

    def in_shard(self, a, ref, h):
        rows = self.piece_shapes[a][0]
        return ref.at[_at(h * rows, rows, rows), :]


def gather_weights(shards, *, name):
    lay = _gather_layout(shards)
    n_arr = len(shards)

    def body(*refs):
        ops = _GatherOps(lay, refs[:n_arr], refs[n_arr:2 * n_arr], *refs[2 * n_arr:])
        ops.start()
        ops.forward()
        ops.finish()

    hbm = pl.BlockSpec(memory_space=pl.ANY)
    return pl.pallas_call(
        body, name=name, in_specs=[hbm] * n_arr, out_specs=[hbm] * n_arr,
        out_shape=_gather_out_shapes(lay, shards[0].dtype), scratch_shapes=_comm_sems(n_arr),
    )(*shards)


def _gather_layout(shards):
    D, INs = shards[0].shape
    return _BigLayout(D, INs, shards[1].shape[0], shards[2].shape[1], shards[3].shape[0])


def _gather_out_shapes(lay, dt):
    return [jax.ShapeDtypeStruct((N_CHIP, lay.D, lay.INs), dt), jax.ShapeDtypeStruct((lay.D, lay.D), dt),
            jax.ShapeDtypeStruct((lay.D, lay.F2s * N_CHIP), dt), jax.ShapeDtypeStruct((lay.Fs * N_CHIP, lay.D), dt)]


def _comm_sems(n_arr):
    return [pltpu.SemaphoreType.DMA((7 * n_arr,)), pltpu.SemaphoreType.DMA((7 * n_arr,)),
            pltpu.SemaphoreType.DMA((n_arr,))]


class _GatherOps:
    def __init__(self, lay, ins, outs, send_sems, recv_sems, local_sems):
        self.lay, self.ins, self.outs = lay, ins, outs
        self.send_sems, self.recv_sems, self.local_sems = send_sems, recv_sems, local_sems
        px, py, pc = _coords()
        self.pc, self.my_chip, self.sibling = pc, 2 * px + py, (px, py, 1 - pc)
        self.chips = [(1 - px, py), (px, 1 - py), (1 - px, 1 - py)]

    def _copy(self, a, kk, k, h, to, src=None):
        dst = self.lay.in_full(a, self.outs[a], k, h)
        return pltpu.make_async_remote_copy(
            src_ref=dst if src is None else src, dst_ref=dst,
            send_sem=self.send_sems.at[7 * a + kk], recv_sem=self.recv_sems.at[7 * a + kk],
            device_id=to, device_id_type=MESH)

    def _own(self, a):
        own = self.lay.in_shard(a, self.ins[a], self.pc)
        local = pltpu.make_async_copy(own, self.lay.in_full(a, self.outs[a], self.my_chip, self.pc),
                                      self.local_sems.at[a])
        sends = [self._copy(a, 0, self.my_chip, self.pc, self.sibling, src=own)]
        sends += [self._copy(a, 1 + j, self.my_chip, self.pc, (*chip, self.pc), src=own)
                  for j, chip in enumerate(self.chips)]
        return local, sends

    def _passed_on(self, a, j):
        qx, qy = self.chips[j]
        return self._copy(a, 4 + j, 2 * qx + qy, self.pc, self.sibling)

    def start(self):
        for a in range(len(self.ins)):
            local, sends = self._own(a)
            for cp in [local] + sends:
                cp.start()

    def forward(self):
        for j, (qx, qy) in enumerate(self.chips):
            for a in range(len(self.ins)):
                self._copy(a, 1 + j, 2 * qx + qy, self.pc, self.sibling).wait_recv()
                self._passed_on(a, j).start()

    def finish(self):
        for a in range(len(self.ins)):
            self._copy(a, 0, self.my_chip, 1 - self.pc, self.sibling).wait_recv()
            for j, (qx, qy) in enumerate(self.chips):
                self._copy(a, 4 + j, 2 * qx + qy, 1 - self.pc, self.sibling).wait_recv()
        for a in range(len(self.ins)):
            local, sends = self._own(a)
            for cp in sends + [self._passed_on(a, j) for j in range(len(self.chips))]:
                cp.wait_send()
            local.wait()


def scatter_grads(partials, *, name):
    lay = _scatter_layout(partials)
    n_arr = len(partials)

    def body(*refs):
        ops = _ScatterOps(lay, refs[:n_arr], refs[n_arr:2 * n_arr], *refs[2 * n_arr:])
        ops.start()
        ops.finish()

    hbm = pl.BlockSpec(memory_space=pl.ANY)
    return pl.pallas_call(
        body, name=name, in_specs=[hbm] * n_arr, out_specs=[hbm] * n_arr,
        out_shape=_scatter_out_shapes(lay, partials[0].dtype), scratch_shapes=_comm_sems(n_arr),
    )(*partials)


def _scatter_layout(partials):
    _, D, INs = partials[0].shape
    return _BigLayout(D, INs, partials[1].shape[0] // N_CHIP, partials[2].shape[1] // N_CHIP,
                      partials[3].shape[0] // N_CHIP)


def _scatter_out_shapes(lay, dt):
    return [jax.ShapeDtypeStruct((N_DEV, *s), dt) for s in lay.piece_shapes]


class _ScatterOps:
    def __init__(self, lay, ins, outs, send_sems, recv_sems, local_sems):
        self.lay, self.ins, self.outs = lay, ins, outs
        self.send_sems, self.recv_sems, self.local_sems = send_sems, recv_sems, local_sems

    def _copies(self):
        px, py, pc = _coords()
        me = 4 * px + 2 * py + pc
        copies, mine = [], []
        for a in range(len(self.ins)):
            mine.append(pltpu.make_async_copy(self.lay.in_full(a, self.ins[a], 2 * px + py, pc),
                                              self.outs[a].at[me], self.local_sems.at[a]))
            for mask in range(1, N_DEV):
                qx = 1 - px if (mask >> 2) & 1 else px
                qy = 1 - py if (mask >> 1) & 1 else py
                qc = 1 - pc if mask & 1 else pc
                copies.append(pltpu.make_async_remote_copy(
                    src_ref=self.lay.in_full(a, self.ins[a], 2 * qx + qy, qc), dst_ref=self.outs[a].at[me],
                    send_sem=self.send_sems.at[7 * a + mask - 1], recv_sem=self.recv_sems.at[7 * a + mask - 1],
                    device_id=(qx, qy, qc), device_id_type=MESH))
        return mine, copies

    def start(self):
        mine, copies = self._copies()
        for cp in mine + copies:
            cp.start()

    def finish(self):
        mine, copies = self._copies()
        for cp in copies:
            cp.wait_recv()
        for cp in copies:
            cp.wait_send()
        for cp in mine:
            cp.wait()


SIBLING_CHUNKS = 4


def sibling_exchange(bufs, *, name):
    n_arr = len(bufs)
    n_ch = [max(n for n in (SIBLING_CHUNKS, 2, 1) if x.shape[1] % (n * SUBLANES) == 0 or n == 1) for x in bufs]
    offs = [sum(n_ch[:a]) for a in range(n_arr)]

    def body(*refs):
        outs = refs[n_arr:2 * n_arr]
        send_sems, recv_sems = refs[2 * n_arr:]
        px, py, pc = _coords()
        copies = []
        for a in range(n_arr):
            rows = bufs[a].shape[1] // n_ch[a]
            for q in range(n_ch[a]):
                mine = outs[a].at[pc, pl.ds(q * rows, rows), :]
                copies.append(pltpu.make_async_remote_copy(
                    src_ref=mine, dst_ref=mine,
                    send_sem=send_sems.at[offs[a] + q], recv_sem=recv_sems.at[offs[a] + q],
                    device_id=(px, py, 1 - pc), device_id_type=MESH))
        for cp in copies:
            cp.start()
        for cp in copies:
            cp.wait_recv()
        for cp in copies:
            cp.wait_send()

    hbm = pl.BlockSpec(memory_space=pl.ANY)
    return pl.pallas_call(
        body, name=name, in_specs=[hbm] * n_arr, out_specs=[hbm] * n_arr,
        out_shape=[jax.ShapeDtypeStruct(x.shape, x.dtype) for x in bufs],
        input_output_aliases={a: a for a in range(n_arr)},
        scratch_shapes=[pltpu.SemaphoreType.DMA((sum(n_ch),))] * 2,
    )(*bufs)


def sum_slots(x, core, *, name):
    n, R, C = x.shape
    lanes = -(-C // LANES) * LANES
    tr = _tile(R, max(BF16_ROWS, (4 << 20) // (n * 2 * lanes)), BF16_ROWS)

    def body(core_ref, x_ref, o_ref):
        acc = x_ref[0].astype(_F32)
        for k in range(1, n):
            acc = acc + x_ref[k].astype(_F32)
        o_ref[...] = acc

    return pl.pallas_call(
        body, name=name,
        grid_spec=pltpu.PrefetchScalarGridSpec(
            num_scalar_prefetch=1, grid=(R // tr,),
            in_specs=[pl.BlockSpec((n, tr, C), lambda i, core_ref: (0, i, 0))],
            out_specs=pl.BlockSpec((None, tr, C), lambda i, core_ref: (core_ref[0], i, 0))),
        out_shape=jax.ShapeDtypeStruct((2, R, C), _F32),
        compiler_params=_cparams(("parallel",)))(core, x)


def _pack_flat(arrays, quantum):
    flat = jnp.concatenate([a.reshape(-1) for a in arrays])
    pad = (-flat.shape[0]) % quantum
    return jnp.pad(flat, (0, pad)) if pad else flat


def _unpack_flat(flat, shapes):
    out, off = [], 0
    for s in shapes:
        n = math.prod(s)
        out.append(flat[off:off + n].reshape(s))
        off += n
    return out


def _small_pack(arrays):
    return _pack_flat([a.astype(_F32) for a in arrays], SUBLANES * LANES).reshape(-1, LANES)


def kernel(x, c, ada_w, ada_b, mix_norm_g, w_in, b_forget, conf_dw_w, conf_dw_b, conf_ln_g, conf_ln_b, sc_dw_w, w_out, ffn_norm_g, w_up, ffn_dw_w, ffn_dw_b, w_down, final_norm_g, loss_target, m_ada_w, m_ada_b, m_mix_norm_g, m_w_in, m_b_forget, m_conf_dw_w, m_conf_dw_b, m_conf_ln_g, m_conf_ln_b, m_sc_dw_w, m_w_out, m_ffn_norm_g, m_w_up, m_ffn_dw_w, m_ffn_dw_b, m_w_down, m_final_norm_g, v_ada_w, v_ada_b, v_mix_norm_g, v_w_in, v_b_forget, v_conf_dw_w, v_conf_dw_b, v_conf_ln_g, v_conf_ln_b, v_sc_dw_w, v_w_out, v_ffn_norm_g, v_w_up, v_ffn_dw_w, v_ffn_dw_b, v_w_down, v_final_norm_g):
    _, T, D = x.shape
    L = ada_w.shape[0]
    A = D // 2
    H = A // HEAD_DIM
    C = D // 4
    assert D - A - C == C
    IN = 3 * A + H + 5 * C
    NM = 3 * A + 5 * C
    NP = NM + LANES
    F2 = w_up.shape[2] * N_CHIP
    F = F2 // 2
    NA = ada_w.shape[2]
    assert NA * N_CHIP == 6 * D and w_in.shape[2] * N_CHIP == IN

    px, py, pc = _coords()
    chip = 2 * px + py
    me = 2 * chip + pc

    x0 = x[0]
    tgt = loss_target[0]

    c_all = allgather_small(c.reshape(-1, LANES), with_sum=False, name="gather_c").reshape(N_DEV, D)
    parts = [matmul(c_all, ada_w[l], out_dtype=_F32, name="ada_fwd", tm=N_DEV, tn=512, tk=D,
                    a_silu=True, precision=HIGHEST) for l in range(L)]
    parts = jnp.stack(parts)
    got = allgather_small(parts.reshape(-1, LANES), with_sum=False, name="gather_ada")
    got = got.reshape(N_DEV, L, N_DEV, NA)[0::2]
    mine = lax.dynamic_index_in_dim(got, me, axis=2, keepdims=False)
    ada = jnp.transpose(mine, (1, 0, 2)).reshape(L, 6 * D) + ada_b

    INs = IN // N_CHIP
    wp_l, wout_l, wup_l, wdown_l = [None] * L, [None] * L, [None] * L, [None] * L
    shards_of = lambda l: [w_in[l].astype(_MM), w_out[l].astype(_MM), w_up[l].astype(_MM), w_down[l].astype(_MM)]

    def take_gathered(l, gathered):
        wi4, wout_l[l], wup_l[l], wdown_l[l] = gathered
        wi = jnp.transpose(wi4, (1, 0, 2)).reshape(D, IN)
        wp_l[l] = jnp.concatenate(
            [wi[:, :3 * A], wi[:, 3 * A + H:], jnp.pad(wi[:, 3 * A:3 * A + H], ((0, 0), (0, LANES - H)))], axis=1)

    take_gathered(0, gather_weights(shards_of(0), name="gather_weights"))

    small_w = _small_pack([conf_dw_w, sc_dw_w, ffn_dw_w])
    sw_all = allgather_small(small_w, with_sum=False, name="gather_small_w")[0::2].reshape(N_CHIP, -1)
    sw_shapes = [conf_dw_w.shape, sc_dw_w.shape, ffn_dw_w.shape]
    sw_parts = [_unpack_flat(sw_all[k], sw_shapes) for k in range(N_CHIP)]
    conf_w_full = jnp.concatenate([p[0] for p in sw_parts], axis=-1)
    sc_w_full = jnp.concatenate([p[1] for p in sw_parts], axis=-1)
    ffn_w_full = jnp.concatenate([p[2] for p in sw_parts], axis=-1)
    bf_pad = jnp.pad(b_forget, ((0, 0), (0, LANES - H)))

    row = lambda a: a.reshape(1, -1)

    saved = []
    x_cur, branch, gate = x0, None, None
    for l in range(L):
        sh_m, sc_m, g_m, sh_f, sc_f, g_f = [row(ada[l, k * D:(k + 1) * D]) for k in range(6)]
        a1 = row(mix_norm_g[l]) * (1.0 + sc_m)
        a2 = row(ffn_norm_g[l]) * (1.0 + sc_f)
        x_in, h1 = resid_norm_fwd(x_cur, a1, sh_m, branch, gate, name="norm_mix_fwd")
        wp = wp_l[l]
        proj = matmul(h1, wp, out_dtype=_MM, name="proj_fwd", tn=512, tk=D, b_cols=(0, NM))
        flog = matmul(h1, wp, out_dtype=_F32, name="fgate_logits", tn=LANES, tk=D, b_cols=(NM, LANES))
        bf = row(bf_pad[l])
        fcum = fgate_fwd(flog, bf, name="fgate_fwd")
        f_t = fcum[:, :H].T
        fq = jnp.broadcast_to(f_t[:, :, None], (H, T, LANES))
        fk = f_t[:, None, :]
        if l + 1 < L:
            attn, lse, gathered = attn_fwd(proj, fq, fk, heads=H, name="attn_fwd_gather", gather=shards_of(l + 1))
            take_gathered(l + 1, gathered)
        else:
            attn, lse, _ = attn_fwd(proj, fq, fk, heads=H, name="attn_fwd")
        cw, cb = conf_w_full[l], row(conf_dw_b[l])
        lg, lb, sw = row(conf_ln_g[l]), row(conf_ln_b[l]), sc_w_full[l]
        cm, cc, zc = mixer_misc_fwd(proj, cw, cb, lg, lb, sw, width=C, base_col=3 * A, name="misc_fwd")
        cat = jnp.concatenate([attn, cm], axis=1)
        mixed = matmul(cat, wout_l[l], out_dtype=_F32, name="wout_fwd", tn=512, tk=D)
        x_mid, h2 = resid_norm_fwd(x_in, a2, sh_f, mixed, g_m, name="norm_ffn_fwd")
        up = matmul(h2, wup_l[l], out_dtype=_MM, name="wup_fwd", tn=512, tk=D)
        fw, fb = ffn_w_full[l], row(ffn_dw_b[l])
        act = ffn_act_fwd(up, fw, fb, name="ffn_act_fwd")
        dn = matmul(act, wdown_l[l], out_dtype=_F32, name="wdown_fwd", tk=1408)
        saved.append(dict(x_in=x_in, h1=h1, proj=proj, flog=flog, fq=fq, fk=fk, attn=attn, lse=lse, cc=cc, zc=zc,
                          cat=cat, mixed=mixed, x_mid=x_mid, h2=h2, up=up, act=act, dn=dn, a1=a1, a2=a2,
                          g_m=g_m, g_f=g_f, sc_m=sc_m, sc_f=sc_f, bf=bf, cw=cw, lg=lg, lb=lb, sw=sw, fw=fw, fb=fb))
        x_cur, branch, gate = x_mid, dn, g_f

    dx, loss_row, d_final_g = final_loss_bwd(x_cur, branch, gate, row(final_norm_g), tgt, name="loss_bwd")

    KF = ffn_dw_w.shape[1]
    g_big, recv_l = [None] * L, [None] * L
    d_ada, d_g1, d_g2, d_bf, d_cw, d_cb, d_lg, d_lb, d_sw, d_fw, d_fb = ([None] * L for _ in range(11))
    for l in reversed(range(L)):
        s = saved[l]
        ddn, dg_f = gate_bwd(dx, s["dn"], s["g_f"], name="gate_ffn_bwd")
        dact = matmul(ddn, wdown_l[l], out_dtype=_MM, name="wdown_dgrad", trans_b=True, tn=512, tk=D)
        gw_down = matmul(s["act"], ddn, out_dtype=_MM, name="wdown_wgrad", trans_a=True, tm=1408)
        du, dwb = ffn_bwd_du(s["up"], dact, s["fw"], s["fb"], name="ffn_bwd_du")
        dup = dwconv_transpose(du, s["fw"], name="ffn_bwd_dup")
        dh2 = matmul(dup, wup_l[l], out_dtype=_MM, name="wup_dgrad", trans_b=True)
        gw_up = matmul(s["h2"], dup, out_dtype=_MM, name="wup_wgrad", trans_a=True)
        dx_mid, dsh_f, da2 = norm_bwd(s["x_mid"], dh2, dx, s["a2"], name="norm_ffn_bwd")
        dmixed, dg_m = gate_bwd(dx_mid, s["mixed"], s["g_m"], name="gate_mix_bwd")
        dcat = matmul(dmixed, wout_l[l], out_dtype=_MM, name="wout_dgrad", trans_b=True, tn=512, tk=D)
        gw_out = matmul(s["cat"], dmixed, out_dtype=_MM, name="wout_wgrad", trans_a=True)
        if l + 1 < L:
            dq, dk, dv, dfq, dfk, recv_l[l + 1] = attn_bwd(
                s["proj"], dcat, s["attn"], s["fq"], s["fk"], s["lse"], heads=H, name="attn_bwd_scatter",
                scatter=g_big[l + 1])
        else:
            dq, dk, dv, dfq, dfk, _ = attn_bwd(s["proj"], dcat, s["attn"], s["fq"], s["fk"], s["lse"], heads=H,
                                               name="attn_bwd")
        dmisc, d_cw[l], d_cb[l], d_lg[l], d_lb[l], d_sw[l] = mixer_misc_bwd(
            s["proj"], dcat, s["cc"], s["zc"], s["cw"], s["lg"], s["lb"], s["sw"],
            width=C, base_col=3 * A, dbase_col=A, name="misc_bwd")
        dfk_pad = jnp.pad((dfk[:, 0, :] + dfq[:, :, 0]).T, ((0, 0), (0, LANES - H)))
        dflog, dbf = fgate_bwd(dfk_pad, s["flog"], s["bf"], name="fgate_bwd")
        dproj = jnp.concatenate([dq, dk, dv, dmisc, dflog], axis=1)
        dh1 = matmul(dproj, wp_l[l], out_dtype=_MM, name="proj_dgrad", trans_b=True, tk=1920)
        gwp = matmul(s["h1"], dproj, out_dtype=_MM, name="proj_wgrad", trans_a=True, tn=1152)
        dx, dsh_m, da1 = norm_bwd(s["x_in"], dh1, dx_mid, s["a1"], name="norm_mix_bwd")

        g1, g2 = row(mix_norm_g[l]), row(ffn_norm_g[l])
        d_ada[l] = jnp.concatenate([dsh_m, da1 * g1, dg_m, dsh_f, da2 * g2, dg_f], axis=1)[0]
        d_g1[l] = (da1 * (1.0 + s["sc_m"]))[0]
        d_g2[l] = (da2 * (1.0 + s["sc_f"]))[0]
        d_bf[l] = dbf[0, :H]
        d_fw[l] = jnp.concatenate([dwb[0, :KF], dwb[1, :KF]], axis=1)
        d_fb[l] = jnp.concatenate([dwb[0, KF], dwb[1, KF]])
        gw_in = jnp.concatenate([gwp[:, :3 * A], gwp[:, NM:NM + H], gwp[:, 3 * A:NM]], axis=1)
        gw_in4 = jnp.transpose(gw_in.reshape(D, N_CHIP, INs), (1, 0, 2))
        g_big[l] = [gw_in4, gw_out, gw_up, gw_down]

    small = [loss_row[0], jnp.stack(d_g1), jnp.stack(d_bf), jnp.stack(d_cw), jnp.stack(d_cb)[:, 0],
             jnp.stack(d_lg)[:, 0], jnp.stack(d_lb)[:, 0], jnp.stack(d_sw), jnp.stack(d_g2), jnp.stack(d_fw),
             jnp.stack(d_fb), d_final_g[0], jnp.stack(d_ada)]
    small_shapes = [a.shape for a in small]
    sm_all, sm_sum = allgather_small(_small_pack(small), with_sum=True, name="reduce_small")
    (loss_v, g_mix_norm, g_bf, g_cw_full, g_cb, g_lg, g_lb, g_sw_full, g_ffn_norm, g_fw_full, g_fb, g_final,
     g_ada_b) = _unpack_flat(sm_sum.reshape(-1), small_shapes)
    loss = loss_v[0]
    n_ada = L * 6 * D
    off_ada = sum(math.prod(sh) for sh in small_shapes[:-1])
    d_ada_all = sm_all.reshape(N_DEV, -1)[:, off_ada:off_ada + n_ada].reshape(N_DEV, L, 6 * D)
    d_ada_chip = lax.dynamic_slice_in_dim(d_ada_all, chip * NA, NA, axis=2)
    d_ada_chip = jnp.transpose(d_ada_chip, (1, 0, 2))
    cshard = lambda a: lax.dynamic_slice_in_dim(a, chip * (a.shape[-1] // N_CHIP), a.shape[-1] // N_CHIP,
                                                axis=a.ndim - 1)
    g_conf_dw_w, g_sc_dw_w, g_ffn_dw_w = cshard(g_cw_full), cshard(g_sw_full), cshard(g_fw_full)

    g_in_l, g_out_l, g_up_l, g_down_l = [], [], [], []
    core = jnp.reshape(pc, (1,)).astype(jnp.int32)
    recv_l[0] = scatter_grads(g_big[0], name="scatter_grads")
    for l in range(L):
        red = [sum_slots(r, core, name="sum_grads") for r in recv_l[l]]
        gi, go, gu, gd = sibling_exchange(red, name="sibling_grads")
        g_in_l.append(gi.reshape(w_in.shape[1:])); g_out_l.append(go.reshape(w_out.shape[1:]))
        g_up_l.append(gu.reshape(w_up.shape[1:])); g_down_l.append(gd.reshape(w_down.shape[1:]))
    g_w_in, g_w_out, g_w_up, g_w_down = (jnp.stack(t) for t in (g_in_l, g_out_l, g_up_l, g_down_l))

    g_ada_w, dl_ada_w, nm_ada_w, nv_ada_w = ada_w_update(c_all.T, d_ada_chip, ada_w, m_ada_w, v_ada_w,
                                                          name="ada_w_update")

    grads = dict(ada_b=g_ada_b, mix_norm_g=g_mix_norm, w_in=g_w_in, b_forget=g_bf, conf_dw_w=g_conf_dw_w,
                 conf_dw_b=g_cb, conf_ln_g=g_lg, conf_ln_b=g_lb, sc_dw_w=g_sc_dw_w, w_out=g_w_out,
                 ffn_norm_g=g_ffn_norm, w_up=g_w_up, ffn_dw_w=g_ffn_dw_w, ffn_dw_b=g_fb, w_down=g_w_down,
                 final_norm_g=g_final)
    weights = dict(ada_b=(ada_b, m_ada_b, v_ada_b), mix_norm_g=(mix_norm_g, m_mix_norm_g, v_mix_norm_g),
                   w_in=(w_in, m_w_in, v_w_in), b_forget=(b_forget, m_b_forget, v_b_forget),
                   conf_dw_w=(conf_dw_w, m_conf_dw_w, v_conf_dw_w), conf_dw_b=(conf_dw_b, m_conf_dw_b, v_conf_dw_b),
                   conf_ln_g=(conf_ln_g, m_conf_ln_g, v_conf_ln_g), conf_ln_b=(conf_ln_b, m_conf_ln_b, v_conf_ln_b),
                   sc_dw_w=(sc_dw_w, m_sc_dw_w, v_sc_dw_w), w_out=(w_out, m_w_out, v_w_out),
                   ffn_norm_g=(ffn_norm_g, m_ffn_norm_g, v_ffn_norm_g), w_up=(w_up, m_w_up, v_w_up),
                   ffn_dw_w=(ffn_dw_w, m_ffn_dw_w, v_ffn_dw_w), ffn_dw_b=(ffn_dw_b, m_ffn_dw_b, v_ffn_dw_b),
                   w_down=(w_down, m_w_down, v_w_down), final_norm_g=(final_norm_g, m_final_norm_g, v_final_norm_g))
    order = ["ada_w", "ada_b", "mix_norm_g", "w_in", "b_forget", "conf_dw_w", "conf_dw_b", "conf_ln_g", "conf_ln_b",
             "sc_dw_w", "w_out", "ffn_norm_g", "w_up", "ffn_dw_w", "ffn_dw_b", "w_down", "final_norm_g"]
    g_out, d_out, m_out, v_out = {}, {}, {}, {}
    g_out["ada_w"], d_out["ada_w"], m_out["ada_w"], v_out["ada_w"] = g_ada_w, dl_ada_w, nm_ada_w, nv_ada_w
    for n in order[1:]:
        w, m, v = weights[n]
        g = grads[n].reshape(w.shape)
        g_out[n] = g
        d_out[n], m_out[n], v_out[n] = adamw(w, g, m, v, name="adamw_" + n)

    return (loss, dx[None], *[g_out[n] for n in order], *[d_out[n] for n in order],
            *[m_out[n] for n in order], *[v_out[n] for n in order])
```

```python
import functools
import math

import jax
import jax.numpy as jnp
from jax import lax
from jax.experimental import pallas as pl
from jax.experimental.pallas import tpu as pltpu

_MM = jnp.bfloat16
_F32 = jnp.float32
VMEM_LIMIT_V7X = 48 * 1024 * 1024
LANES = 128
SUBLANES = 8
BF16_ROWS = 16
HEAD_DIM = 128
RMS_EPS = 1e-6
LN_EPS = 1e-5
NEG = -1e30
HIGHEST = lax.Precision.HIGHEST

ADAM_LR = 0.001
ADAM_B1 = 0.9
ADAM_B2 = 0.999
ADAM_EPS = 1e-08
ADAM_WD = 0.01
ADAM_STEP = 10

N_DEV = 8
N_CHIP = 4

ATT_BLOCK = 512
CONV_TILE = 512
CONV_HALO = 32
FFN_HALO = BF16_ROWS
NORM_TILE = 256
MM_TM = 1024
MM_TN = 1024
MM_TK = 1024

MESH = pl.DeviceIdType.MESH


def _tile(dim, pref, mult=LANES):
    t = (min(pref, dim) // mult) * mult
    while t >= mult:
        if dim % t == 0:
            return t
        t -= mult
    return dim


def _cparams(sem):
    return pltpu.CompilerParams(dimension_semantics=sem, vmem_limit_bytes=VMEM_LIMIT_V7X)


def _sigmoid(x):
    return 1.0 / (1.0 + jnp.exp(-x))


def _colsum(x):
    return jnp.sum(x, axis=0, keepdims=True)


def matmul(a, b, *, out_dtype, name, trans_a=False, trans_b=False, tm=MM_TM, tn=MM_TN, tk=MM_TK,
           a_silu=False, precision=None, b_cols=None):
    M, K = (a.shape[1], a.shape[0]) if trans_a else a.shape
    N = b.shape[0] if trans_b else b.shape[1]
    assert (b.shape[1] if trans_b else b.shape[0]) == K
    col0 = 0
    if b_cols is not None:
        assert not trans_b
        col0, N = b_cols
    tm = _tile(M, tm, SUBLANES if (M % LANES) else LANES)
    tn = _tile(math.gcd(N, col0) if col0 else N, tn)
    tk = _tile(K, tk)
    nk = K // tk
    jb = col0 // tn
    dims = (((0 if trans_a else 1,), (1 if trans_b else 0,)), ((), ()))

    def body(a_ref, b_ref, o_ref, *scratch):
        av = a_ref[...]
        if a_silu:
            av = av * _sigmoid(av)
        part = lax.dot_general(av, b_ref[...], dims, preferred_element_type=_F32, precision=precision)
        if nk == 1:
            o_ref[...] = part.astype(o_ref.dtype)
        else:
            acc_ref, = scratch
            k = pl.program_id(2)

            @pl.when(k == 0)
            def _():
                acc_ref[...] = part

            @pl.when(k > 0)
            def _():
                acc_ref[...] += part

            @pl.when(k == nk - 1)
            def _():
                o_ref[...] = acc_ref[...].astype(o_ref.dtype)

    a_spec = (pl.BlockSpec((tk, tm), lambda i, j, k: (k, i)) if trans_a
              else pl.BlockSpec((tm, tk), lambda i, j, k: (i, k)))
    b_spec = (pl.BlockSpec((tn, tk), lambda i, j, k: (j, k)) if trans_b
              else pl.BlockSpec((tk, tn), lambda i, j, k: (k, j + jb)))
    return pl.pallas_call(
        body, name=name, grid=(M // tm, N // tn, nk),
        in_specs=[a_spec, b_spec],
        out_specs=pl.BlockSpec((tm, tn), lambda i, j, k: (i, j)),
        out_shape=jax.ShapeDtypeStruct((M, N), out_dtype),
        scratch_shapes=[pltpu.VMEM((tm, tn), _F32)] if nk > 1 else [],
        compiler_params=_cparams(("parallel", "parallel", "arbitrary")),
    )(a, b)


def resid_norm_fwd(x, a, sh, branch=None, gate=None, *, name):
    T, D = x.shape
    tm = _tile(T, NORM_TILE, BF16_ROWS)
    has_res = branch is not None

    def body(*refs):
        if has_res:
            x_ref, br_ref, g_ref, a_ref, sh_ref, xo_ref, h_ref = refs
            xv = x_ref[...] + g_ref[...] * br_ref[...]
            xo_ref[...] = xv
        else:
            x_ref, a_ref, sh_ref, h_ref = refs
            xv = x_ref[...]
        r = lax.rsqrt(jnp.mean(xv * xv, axis=-1, keepdims=True) + RMS_EPS)
        h_ref[...] = (xv * r * a_ref[...] + sh_ref[...]).astype(h_ref.dtype)

    row = pl.BlockSpec((tm, D), lambda i: (i, 0))
    vec = pl.BlockSpec((1, D), lambda i: (0, 0))
    if has_res:
        xo, h = pl.pallas_call(
            body, name=name, grid=(T // tm,), in_specs=[row, row, vec, vec, vec], out_specs=[row, row],
            out_shape=[jax.ShapeDtypeStruct((T, D), _F32), jax.ShapeDtypeStruct((T, D), _MM)],
            compiler_params=_cparams(("parallel",)))(x, branch, gate, a, sh)
        return xo, h
    h = pl.pallas_call(
        body, name=name, grid=(T // tm,), in_specs=[row, vec, vec], out_specs=row,
        out_shape=jax.ShapeDtypeStruct((T, D), _MM),
        compiler_params=_cparams(("parallel",)))(x, a, sh)
    return x, h


def norm_bwd(x, dh, dx_in, a, *, name):
    T, D = x.shape
    tm = _tile(T, NORM_TILE, BF16_ROWS)

    def body(x_ref, dh_ref, dxi_ref, a_ref, dxo_ref, dsh_ref, da_ref):
        i = pl.program_id(0)
        xv = x_ref[...]
        r = lax.rsqrt(jnp.mean(xv * xv, axis=-1, keepdims=True) + RMS_EPS)
        n = xv * r
        dhv = dh_ref[...].astype(_F32)
        dn = dhv * a_ref[...]
        dxo_ref[...] = dxi_ref[...] + r * (dn - n * jnp.mean(dn * n, axis=-1, keepdims=True))

        @pl.when(i == 0)
        def _():
            dsh_ref[...] = jnp.zeros_like(dsh_ref)
            da_ref[...] = jnp.zeros_like(da_ref)

        dsh_ref[...] += _colsum(dhv)
        da_ref[...] += _colsum(dhv * n)

    row = pl.BlockSpec((tm, D), lambda i: (i, 0))
    vec = pl.BlockSpec((1, D), lambda i: (0, 0))
    return pl.pallas_call(
        body, name=name, grid=(T // tm,), in_specs=[row, row, row, vec], out_specs=[row, vec, vec],
        out_shape=[jax.ShapeDtypeStruct((T, D), _F32), jax.ShapeDtypeStruct((1, D), _F32),
                   jax.ShapeDtypeStruct((1, D), _F32)],
        compiler_params=_cparams(("arbitrary",)))(x, dh, dx_in, a)


def gate_bwd(dx, branch, gate, *, name):
    T, D = dx.shape
    tm = _tile(T, NORM_TILE, BF16_ROWS)

    def body(dx_ref, br_ref, g_ref, db_ref, dg_ref):
        i = pl.program_id(0)
        dxv = dx_ref[...]
        db_ref[...] = (dxv * g_ref[...]).astype(db_ref.dtype)

        @pl.when(i == 0)
        def _():
            dg_ref[...] = jnp.zeros_like(dg_ref)

        dg_ref[...] += _colsum(dxv * br_ref[...])

    row = pl.BlockSpec((tm, D), lambda i: (i, 0))
    vec = pl.BlockSpec((1, D), lambda i: (0, 0))
    return pl.pallas_call(
        body, name=name, grid=(T // tm,), in_specs=[row, row, vec], out_specs=[row, vec],
        out_shape=[jax.ShapeDtypeStruct((T, D), _MM), jax.ShapeDtypeStruct((1, D), _F32)],
        compiler_params=_cparams(("arbitrary",)))(dx, branch, gate)


def final_loss_bwd(x, branch, gate, gfin, tgt, *, name):
    T, D = x.shape
    tm = _tile(T, NORM_TILE, BF16_ROWS)

    def body(x_ref, br_ref, g_ref, gf_ref, t_ref, dx_ref, loss_ref, dgf_ref):
        i = pl.program_id(0)
        xv = x_ref[...] + g_ref[...] * br_ref[...]
        r = lax.rsqrt(jnp.mean(xv * xv, axis=-1, keepdims=True) + RMS_EPS)
        n = xv * r
        e = n * gf_ref[...] - t_ref[...]
        dy = e * (1.0 / D)
        dn = dy * gf_ref[...]
        dx_ref[...] = r * (dn - n * jnp.mean(dn * n, axis=-1, keepdims=True))

        @pl.when(i == 0)
        def _():
            loss_ref[...] = jnp.zeros_like(loss_ref)
            dgf_ref[...] = jnp.zeros_like(dgf_ref)

        per_row = jnp.mean(e * e, axis=-1, keepdims=True)
        loss_ref[...] += jnp.broadcast_to(0.5 * _colsum(per_row), loss_ref.shape)
        dgf_ref[...] += _colsum(dy * n)

    row = pl.BlockSpec((tm, D), lambda i: (i, 0))
    vec = pl.BlockSpec((1, D), lambda i: (0, 0))
    lvec = pl.BlockSpec((1, LANES), lambda i: (0, 0))
    return pl.pallas_call(
        body, name=name, grid=(T // tm,), in_specs=[row, row, vec, vec, row], out_specs=[row, lvec, vec],
        out_shape=[jax.ShapeDtypeStruct((T, D), _F32), jax.ShapeDtypeStruct((1, LANES), _F32),
                   jax.ShapeDtypeStruct((1, D), _F32)],
        compiler_params=_cparams(("arbitrary",)))(x, branch, gate, gfin, tgt)


def _log_sigmoid(x):
    return jnp.minimum(x, 0.0) - jnp.log(1.0 + jnp.exp(-jnp.abs(x)))


def fgate_fwd(flog, bf, *, name):
    T = flog.shape[0]
    tt = _tile(T, 256)

    def body(x_ref, b_ref, f_ref, carry):
        i = pl.program_id(0)

        @pl.when(i == 0)
        def _():
            carry[...] = jnp.zeros_like(carry)

        lf = _log_sigmoid(x_ref[...] + b_ref[...])
        rows = lax.broadcasted_iota(jnp.int32, (tt, tt), 0)
        cols = lax.broadcasted_iota(jnp.int32, (tt, tt), 1)
        tri = (cols <= rows).astype(_F32)
        f_ref[...] = jnp.dot(tri, lf, preferred_element_type=_F32, precision=HIGHEST) + carry[0:1, :]
        carry[0:1, :] = f_ref[tt - 1:tt, :]

    return pl.pallas_call(
        body, name=name, grid=(T // tt,),
        in_specs=[pl.BlockSpec((tt, LANES), lambda i: (i, 0)), pl.BlockSpec((1, LANES), lambda i: (0, 0))],
        out_specs=pl.BlockSpec((tt, LANES), lambda i: (i, 0)),
        out_shape=jax.ShapeDtypeStruct((T, LANES), _F32),
        scratch_shapes=[pltpu.VMEM((SUBLANES, LANES), _F32)],
        compiler_params=_cparams(("arbitrary",)))(flog, bf)


def fgate_bwd(dfk, flog, bf, *, name):
    T = flog.shape[0]
    tt = _tile(T, 256)
    nb = T // tt

    def body(d_ref, x_ref, b_ref, o_ref, db_ref, carry):
        i = pl.program_id(0)

        @pl.when(i == 0)
        def _():
            carry[...] = jnp.zeros_like(carry)
            db_ref[...] = jnp.zeros_like(db_ref)

        rows = lax.broadcasted_iota(jnp.int32, (tt, tt), 0)
        cols = lax.broadcasted_iota(jnp.int32, (tt, tt), 1)
        upper = (cols >= rows).astype(_F32)
        dlf = jnp.dot(upper, d_ref[...], preferred_element_type=_F32, precision=HIGHEST) + carry[0:1, :]
        carry[0:1, :] = dlf[0:1, :]
        dfl = dlf * _sigmoid(-(x_ref[...] + b_ref[...]))
        o_ref[...] = dfl.astype(o_ref.dtype)
        db_ref[...] += _colsum(dfl)

    rev = pl.BlockSpec((tt, LANES), lambda i: (nb - 1 - i, 0))
    vec = pl.BlockSpec((1, LANES), lambda i: (0, 0))
    return pl.pallas_call(
        body, name=name, grid=(nb,), in_specs=[rev, rev, vec], out_specs=[rev, vec],
        out_shape=[jax.ShapeDtypeStruct((T, LANES), _MM), jax.ShapeDtypeStruct((1, LANES), _F32)],
        scratch_shapes=[pltpu.VMEM((SUBLANES, LANES), _F32)],
        compiler_params=_cparams(("arbitrary",)))(dfk, flog, bf)


def _att_scores(q, k, fq, fk, rep, masked):
    s = lax.dot_general(q, k, (((1,), (1,)), ((), ())), preferred_element_type=_F32)
    s = s * (HEAD_DIM ** -0.5) + (jnp.tile(fq, (1, rep)) - fk)
    if masked:
        rows = lax.broadcasted_iota(jnp.int32, s.shape, 0)
        cols = lax.broadcasted_iota(jnp.int32, s.shape, 1)
        s = jnp.where(cols <= rows, s, NEG)
    return s


def _fold_q(r, t, nb):
    first = t <= r
    return jnp.where(first, r, nb - 1 - r), jnp.where(first, t, t - r - 1)


def _fold_k(r, t, nb):
    first = t < nb - r
    return jnp.where(first, r, nb - 1 - r), jnp.where(first, r + t, t - 1)


def _first_step():
    return (pl.program_id(0) == 0) & (pl.program_id(1) == 0) & (pl.program_id(2) == 0)


def _step_is(h, r, t):
    return (pl.program_id(0) == h) & (pl.program_id(1) == r) & (pl.program_id(2) == t)


def attn_fwd(proj, fq, fk, *, heads, name, gather=None):
    T = proj.shape[0]
    H = heads
    tb = _tile(T, ATT_BLOCK)
    nb = T // tb
    assert nb % 2 == 0
    rep = tb // LANES
    n_g = len(gather) if gather else 0
    lay = _gather_layout(gather) if gather else None

    def body(*refs):
        q_ref, k_ref, v_ref, fq_ref, fk_ref = refs[:5]
        o_ref, lse_ref = refs[5 + n_g:7 + n_g]
        m_s, l_s, acc_s = refs[7 + 2 * n_g:10 + 2 * n_g]
        i, j = _fold_q(pl.program_id(1), pl.program_id(2), nb)
        if n_g:
            ops = _GatherOps(lay, refs[5:5 + n_g], refs[7 + n_g:7 + 2 * n_g], *refs[10 + 2 * n_g:])
            pl.when(_first_step())(ops.start)
            pl.when(_step_is(H // 2, 0, 0))(ops.forward)

        @pl.when(j == 0)
        def _():
            m_s[...] = jnp.full_like(m_s, NEG)
            l_s[...] = jnp.zeros_like(l_s)
            acc_s[...] = jnp.zeros_like(acc_s)

        def step(masked):
            s = _att_scores(q_ref[...], k_ref[...], fq_ref[...], fk_ref[...], rep, masked)
            m_prev = m_s[...]
            m_new = jnp.maximum(m_prev, jnp.max(s, axis=-1, keepdims=True))
            alpha = jnp.exp(m_prev - m_new)
            p = jnp.exp(s - jnp.tile(m_new, (1, rep)))
            l_s[...] = alpha * l_s[...] + jnp.sum(p, axis=-1, keepdims=True)
            v = v_ref[...]
            acc_s[...] = alpha * acc_s[...] + jnp.dot(p.astype(v.dtype), v, preferred_element_type=_F32)
            m_s[...] = m_new

        @pl.when(j < i)
        def _():
            step(False)

        @pl.when(j == i)
        def _():
            step(True)
            o_ref[...] = (acc_s[...] / l_s[...]).astype(o_ref.dtype)
            lse_ref[...] = m_s[...] + jnp.log(l_s[...])

        if n_g:
            pl.when(_step_is(H - 1, nb // 2 - 1, nb))(ops.finish)

    qi = lambda r, t: _fold_q(r, t, nb)[0]
    kj = lambda r, t: _fold_q(r, t, nb)[1]
    qs = pl.BlockSpec((tb, HEAD_DIM), lambda h, r, t: (qi(r, t), h))
    ks = pl.BlockSpec((tb, HEAD_DIM), lambda h, r, t: (kj(r, t), H + h))
    vs = pl.BlockSpec((tb, HEAD_DIM), lambda h, r, t: (kj(r, t), 2 * H + h))
    fqs = pl.BlockSpec((None, tb, LANES), lambda h, r, t: (h, qi(r, t), 0))
    fks = pl.BlockSpec((None, 1, tb), lambda h, r, t: (h, 0, kj(r, t)))
    hbm = pl.BlockSpec(memory_space=pl.ANY)
    outs = pl.pallas_call(
        body, name=name, grid=(H, nb // 2, nb + 1),
        in_specs=[qs, ks, vs, fqs, fks] + [hbm] * n_g,
        out_specs=[qs, fqs] + [hbm] * n_g,
        out_shape=[jax.ShapeDtypeStruct((T, H * HEAD_DIM), _MM), jax.ShapeDtypeStruct((H, T, LANES), _F32)]
        + (_gather_out_shapes(lay, gather[0].dtype) if n_g else []),
        scratch_shapes=[pltpu.VMEM((tb, LANES), _F32), pltpu.VMEM((tb, LANES), _F32),
                        pltpu.VMEM((tb, HEAD_DIM), _F32)] + (_comm_sems(n_g) if n_g else []),
        compiler_params=_cparams(("arbitrary",) * 3 if n_g else ("parallel", "parallel", "arbitrary")))(
            proj, proj, proj, fq, fk, *(gather or []))
    return outs[0], outs[1], outs[2:]


def _att_p_ds(q, k, v, do, o, fq, fk, lse, rep, masked):
    s = _att_scores(q, k, fq, fk, rep, masked)
    p = jnp.exp(s - jnp.tile(lse, (1, rep)))
    delta = jnp.sum(do.astype(_F32) * o.astype(_F32), axis=-1, keepdims=True)
    dp = lax.dot_general(do, v, (((1,), (1,)), ((), ())), preferred_element_type=_F32)
    ds = p * (dp - delta)
    return p, ds


def attn_bwd(proj, dcat, attn, fq, fk, lse, *, heads, name, scatter=None):
    T = proj.shape[0]
    H = heads
    tb = _tile(T, ATT_BLOCK)
    nb = T // tb
    assert nb % 2 == 0
    rep = tb // LANES
    scale = HEAD_DIM ** -0.5
    n_s = len(scatter) if scatter else 0
    lay = _scatter_layout(scatter) if scatter else None

    def body(*refs):
        q_ref, k_ref, v_ref, do_ref, o_ref, fq_ref, fk_ref, lse_ref = refs[:8]
        dq_ref, dk_ref, dv_ref, dfq_ref, dfk_ref = refs[8 + n_s:13 + n_s]
        dk_s, dv_s, dfk_s, dq_s, dfq_s = refs[13 + 2 * n_s:18 + 2 * n_s]
        r, t = pl.program_id(1), pl.program_id(2)
        j, i = _fold_k(r, t, nb)
        if n_s:
            ops = _ScatterOps(lay, refs[8:8 + n_s], refs[13 + n_s:13 + 2 * n_s], *refs[18 + 2 * n_s:])
            pl.when(_first_step())(ops.start)

        @pl.when((r == 0) & (t == 0))
        def _():
            dq_s[...] = jnp.zeros_like(dq_s)
            dfq_s[...] = jnp.zeros_like(dfq_s)

        @pl.when(i == j)
        def _():
            dk_s[...] = jnp.zeros_like(dk_s)
            dv_s[...] = jnp.zeros_like(dv_s)
            dfk_s[...] = jnp.zeros_like(dfk_s)

        def step(masked):
            q = q_ref[...]
            k = k_ref[...]
            do = do_ref[...]
            p, ds = _att_p_ds(q, k, v_ref[...], do, o_ref[...], fq_ref[...], fk_ref[...],
                              lse_ref[...], rep, masked)
            dsm = ds.astype(q.dtype)
            tn = (((0,), (0,)), ((), ()))
            dv_s[...] += lax.dot_general(p.astype(do.dtype), do, tn, preferred_element_type=_F32)
            dk_s[...] += lax.dot_general(dsm, q, tn, preferred_element_type=_F32)
            dfk_s[0:1, :] += -_colsum(ds)
            rows = pl.ds(pl.multiple_of(i * tb, tb), tb)
            dq_s[rows, :] += jnp.dot(dsm, k, preferred_element_type=_F32)
            dfq_s[rows, :] += jnp.sum(ds, axis=-1, keepdims=True)

        @pl.when(i > j)
        def _():
            step(False)

        @pl.when(i == j)
        def _():
            step(True)

        @pl.when(i == nb - 1)
        def _():
            dk_ref[...] = (dk_s[...] * scale).astype(dk_ref.dtype)
            dv_ref[...] = dv_s[...].astype(dv_ref.dtype)
            dfk_ref[...] = dfk_s[0:1, :]

        @pl.when((r == nb // 2 - 1) & (t == nb))
        def _():
            dq_ref[...] = (dq_s[...] * scale).astype(dq_ref.dtype)
            dfq_ref[...] = dfq_s[...]

        if n_s:
            pl.when(_step_is(H - 1, nb // 2 - 1, nb))(ops.finish)

    kj = lambda r, t: _fold_k(r, t, nb)[0]
    qi = lambda r, t: _fold_k(r, t, nb)[1]
    qs = pl.BlockSpec((tb, HEAD_DIM), lambda h, r, t: (qi(r, t), h))
    ks = pl.BlockSpec((tb, HEAD_DIM), lambda h, r, t: (kj(r, t), H + h))
    vs = pl.BlockSpec((tb, HEAD_DIM), lambda h, r, t: (kj(r, t), 2 * H + h))
    stat = pl.BlockSpec((None, tb, LANES), lambda h, r, t: (h, qi(r, t), 0))
    fks = pl.BlockSpec((None, 1, tb), lambda h, r, t: (h, 0, kj(r, t)))
    kout = pl.BlockSpec((tb, HEAD_DIM), lambda h, r, t: (kj(r, t), h))
    head_q = pl.BlockSpec((T, HEAD_DIM), lambda h, r, t: (0, h))
    head_stat = pl.BlockSpec((None, T, LANES), lambda h, r, t: (h, 0, 0))
    hbm = pl.BlockSpec(memory_space=pl.ANY)
    A = H * HEAD_DIM
    outs = pl.pallas_call(
        body, name=name, grid=(H, nb // 2, nb + 1),
        in_specs=[qs, ks, vs, qs, qs, stat, fks, stat] + [hbm] * n_s,
        out_specs=[head_q, kout, kout, head_stat, fks] + [hbm] * n_s,
        out_shape=[jax.ShapeDtypeStruct((T, A), _MM), jax.ShapeDtypeStruct((T, A), _MM),
                   jax.ShapeDtypeStruct((T, A), _MM), jax.ShapeDtypeStruct((H, T, LANES), _F32),
                   jax.ShapeDtypeStruct((H, 1, T), _F32)]
        + (_scatter_out_shapes(lay, scatter[0].dtype) if n_s else []),
        scratch_shapes=[pltpu.VMEM((tb, HEAD_DIM), _F32), pltpu.VMEM((tb, HEAD_DIM), _F32),
                        pltpu.VMEM((SUBLANES, tb), _F32), pltpu.VMEM((T, HEAD_DIM), _F32),
                        pltpu.VMEM((T, LANES), _F32)] + (_comm_sems(n_s) if n_s else []),
        compiler_params=_cparams(("arbitrary",) * 3 if n_s else ("parallel", "arbitrary", "arbitrary")))(
            proj, proj, proj, dcat, attn, fq, fk, lse, *(scatter or []))
    return (*outs[:5], outs[5:])


def _causal_taps(w_ref, scr, halo, tt, width):
    acc = w_ref[width - 1:width, :] * scr[halo:halo + tt, :]
    for j in range(1, width):
        acc = acc + w_ref[width - 1 - j:width - j, :] * scr[halo - j:halo - j + tt, :]
    return acc


def _anticausal_taps(w_ref, scr, tt, width):
    acc = w_ref[width - 1:width, :] * scr[0:tt, :]
    for j in range(1, width):
        acc = acc + w_ref[width - 1 - j:width - j, :] * scr[j:j + tt, :]
    return acc


def _ln_fwd(cc, g, b):
    mu = jnp.mean(cc, axis=-1, keepdims=True)
    xc = cc - mu
    rstd = lax.rsqrt(jnp.mean(xc * xc, axis=-1, keepdims=True) + LN_EPS)
    xhat = xc * rstd
    return xhat, rstd, xhat * g + b


def _ln_silu_bwd(cc, dconf, g, b):
    xhat, rstd, ln = _ln_fwd(cc, g, b)
    s = _sigmoid(ln)
    dln = dconf * (s * (1.0 + ln * (1.0 - s)))
    dxh = dln * g
    dcc = rstd * (dxh - jnp.mean(dxh, axis=-1, keepdims=True)
                  - xhat * jnp.mean(dxh * xhat, axis=-1, keepdims=True))
    return dcc, dln, xhat


def mixer_misc_fwd(proj, cw, cb, lg, lb, sw, *, width, base_col, name):
    T = proj.shape[0]
    C = width
    tt = _tile(T, CONV_TILE)
    HB = CONV_HALO
    per = tt // HB
    KC, KS = cw.shape[0], sw.shape[0]
    b0 = base_col // C

    def body(cv, cg, sx, sb, sc, cvh, cgh, sxh, sch, cw_ref, cb_ref, lg_ref, lb_ref, sw_ref,
             cm_ref, cc_ref, zc_ref, gscr, zscr):
        keep = (pl.program_id(0) > 0).astype(_F32)
        f = lambda r: r[...].astype(_F32)
        gscr[0:HB, :] = f(cvh) * _sigmoid(f(cgh)) * keep
        gscr[HB:HB + tt, :] = f(cv) * _sigmoid(f(cg))
        cc = _causal_taps(cw_ref, gscr, HB, tt, KC) + cb_ref[...]
        cc_ref[...] = cc
        _, _, ln = _ln_fwd(cc, lg_ref[...], lb_ref[...])
        cm_ref[:, 0:C] = (ln * _sigmoid(ln)).astype(cm_ref.dtype)
        zscr[0:HB, :] = f(sch) * f(sxh) * keep
        zscr[HB:HB + tt, :] = f(sc) * f(sx)
        zc = _causal_taps(sw_ref, zscr, HB, tt, KS)
        zc_ref[...] = zc
        cm_ref[:, C:2 * C] = (f(sb) * zc).astype(cm_ref.dtype)

    main = lambda k: pl.BlockSpec((tt, C), lambda i: (i, b0 + k))
    halo = lambda k: pl.BlockSpec((HB, C), lambda i: (jnp.maximum(i * per - 1, 0), b0 + k))
    full = lambda a: pl.BlockSpec(a.shape, lambda i: (0, 0))
    return pl.pallas_call(
        body, name=name, grid=(T // tt,),
        in_specs=[main(0), main(1), main(2), main(3), main(4), halo(0), halo(1), halo(2), halo(4),
                  full(cw), full(cb), full(lg), full(lb), full(sw)],
        out_specs=[pl.BlockSpec((tt, 2 * C), lambda i: (i, 0)), pl.BlockSpec((tt, C), lambda i: (i, 0)),
                   pl.BlockSpec((tt, C), lambda i: (i, 0))],
        out_shape=[jax.ShapeDtypeStruct((T, 2 * C), _MM), jax.ShapeDtypeStruct((T, C), _F32),
                   jax.ShapeDtypeStruct((T, C), _F32)],
        scratch_shapes=[pltpu.VMEM((tt + HB, C), _F32), pltpu.VMEM((tt + HB, C), _F32)],
        compiler_params=_cparams(("parallel",)))(
            proj, proj, proj, proj, proj, proj, proj, proj, proj, cw, cb, lg, lb, sw)


def mixer_misc_bwd(proj, dcat, cc, zc, cw, lg, lb, sw, *, width, base_col, dbase_col, name):
    T = proj.shape[0]
    C = width
    tt = _tile(T, CONV_TILE)
    nt = T // tt
    HB = CONV_HALO
    per = tt // HB
    KC, KS = cw.shape[0], sw.shape[0]
    b0 = base_col // C
    d0 = dbase_col // C
    last_hb = T // HB - 1

    def body(cv, cg, sx, sb, sc, cvh, cgh, sxh, sch, sbn, dcf, dsv, dcfn, dsvn, cc_ref, ccn_ref, zc_ref,
             cw_ref, lg_ref, lb_ref, sw_ref,
             dm_ref, dcw_ref, dcb_ref, dlg_ref, dlb_ref, dsw_ref, gscr, dscr, zscr, zdscr):
        i = pl.program_id(0)
        keep = (i > 0).astype(_F32)
        ahead = (i < nt - 1).astype(_F32)
        f = lambda r: r[...].astype(_F32)

        @pl.when(i == 0)
        def _():
            for r in (dcw_ref, dcb_ref, dlg_ref, dlb_ref, dsw_ref):
                r[...] = jnp.zeros_like(r)

        g, b = lg_ref[...], lb_ref[...]
        dcc, dln, xhat = _ln_silu_bwd(cc_ref[...], f(dcf), g, b)
        dcc_next, _, _ = _ln_silu_bwd(ccn_ref[...], f(dcfn), g, b)
        dlg_ref[...] += _colsum(dln * xhat)
        dlb_ref[...] += _colsum(dln)
        dcb_ref[...] += _colsum(dcc)
        dscr[0:tt, :] = dcc
        dscr[tt:tt + HB, :] = dcc_next * ahead
        dglu = _anticausal_taps(cw_ref, dscr, tt, KC)
        cvv = f(cv)
        sig = _sigmoid(f(cg))
        dm_ref[:, 0:C] = (dglu * sig).astype(dm_ref.dtype)
        dm_ref[:, C:2 * C] = (dglu * cvv * sig * (1.0 - sig)).astype(dm_ref.dtype)
        gscr[0:HB, :] = f(cvh) * _sigmoid(f(cgh)) * keep
        gscr[HB:HB + tt, :] = cvv * sig
        for j in range(KC):
            dcw_ref[KC - 1 - j:KC - j, :] += _colsum(dcc * gscr[HB - j:HB - j + tt, :])
        dsc_out = f(dsv)
        sbv = f(sb)
        dzc = dsc_out * sbv
        zdscr[0:tt, :] = dzc
        zdscr[tt:tt + HB, :] = f(dsvn) * f(sbn) * ahead
        dz = _anticausal_taps(sw_ref, zdscr, tt, KS)
        sxv, scv = f(sx), f(sc)
        dm_ref[:, 2 * C:3 * C] = (dz * scv).astype(dm_ref.dtype)
        dm_ref[:, 3 * C:4 * C] = (dsc_out * zc_ref[...]).astype(dm_ref.dtype)
        dm_ref[:, 4 * C:5 * C] = (dz * sxv).astype(dm_ref.dtype)
        zscr[0:HB, :] = f(sch) * f(sxh) * keep
        zscr[HB:HB + tt, :] = scv * sxv
        for j in range(KS):
            dsw_ref[KS - 1 - j:KS - j, :] += _colsum(dzc * zscr[HB - j:HB - j + tt, :])

    main = lambda col: pl.BlockSpec((tt, C), lambda i: (i, col))
    prev = lambda col: pl.BlockSpec((HB, C), lambda i: (jnp.maximum(i * per - 1, 0), col))
    nxt = lambda col: pl.BlockSpec((HB, C), lambda i: (jnp.minimum((i + 1) * per, last_hb), col))
    full = lambda a: pl.BlockSpec(a.shape, lambda i: (0, 0))
    vec = pl.BlockSpec((1, C), lambda i: (0, 0))
    return pl.pallas_call(
        body, name=name, grid=(nt,),
        in_specs=[main(b0), main(b0 + 1), main(b0 + 2), main(b0 + 3), main(b0 + 4),
                  prev(b0), prev(b0 + 1), prev(b0 + 2), prev(b0 + 4), nxt(b0 + 3),
                  main(d0), main(d0 + 1), nxt(d0), nxt(d0 + 1),
                  main(0), nxt(0), main(0),
                  full(cw), full(lg), full(lb), full(sw)],
        out_specs=[pl.BlockSpec((tt, 5 * C), lambda i: (i, 0)), full(cw), vec, vec, vec, full(sw)],
        out_shape=[jax.ShapeDtypeStruct((T, 5 * C), _MM), jax.ShapeDtypeStruct(cw.shape, _F32),
                   jax.ShapeDtypeStruct((1, C), _F32), jax.ShapeDtypeStruct((1, C), _F32),
                   jax.ShapeDtypeStruct((1, C), _F32), jax.ShapeDtypeStruct(sw.shape, _F32)],
        scratch_shapes=[pltpu.VMEM((tt + HB, C), _F32)] * 4,
        compiler_params=_cparams(("arbitrary",)))(
            proj, proj, proj, proj, proj, proj, proj, proj, proj, proj,
            dcat, dcat, dcat, dcat, cc, cc, zc, cw, lg, lb, sw)


def _ffn_u(main_ref, halo_ref, w_ref, b_ref, scr, keep, tt, width):
    HB = FFN_HALO
    scr[0:HB, :] = halo_ref[...].astype(_F32) * keep
    scr[HB:HB + tt, :] = main_ref[...].astype(_F32)
    return _causal_taps(w_ref, scr, HB, tt, width) + b_ref[...]


def ffn_act_fwd(up, w, b, *, name):
    T, F2 = up.shape
    F = F2 // 2
    K = w.shape[0]
    tt = _tile(T, CONV_TILE)
    tc = _tile(F, 512)
    nb = F // tc
    per = tt // FFN_HALO

    def body(g_ref, v_ref, gh_ref, vh_ref, wg_ref, wv_ref, bg_ref, bv_ref, o_ref, gscr, vscr):
        keep = (pl.program_id(0) > 0).astype(_F32)
        ug = _ffn_u(g_ref, gh_ref, wg_ref, bg_ref, gscr, keep, tt, K)
        uv = _ffn_u(v_ref, vh_ref, wv_ref, bv_ref, vscr, keep, tt, K)
        o_ref[...] = (ug * _sigmoid(ug) * uv).astype(o_ref.dtype)

    main = lambda off: pl.BlockSpec((tt, tc), lambda i, j: (i, j + off))
    halo = lambda off: pl.BlockSpec((FFN_HALO, tc), lambda i, j: (jnp.maximum(i * per - 1, 0), j + off))
    wsp = lambda off: pl.BlockSpec((K, tc), lambda i, j: (0, j + off))
    bsp = lambda off: pl.BlockSpec((1, tc), lambda i, j: (0, j + off))
    return pl.pallas_call(
        body, name=name, grid=(T // tt, nb),
        in_specs=[main(0), main(nb), halo(0), halo(nb), wsp(0), wsp(nb), bsp(0), bsp(nb)],
        out_specs=pl.BlockSpec((tt, tc), lambda i, j: (i, j)),
        out_shape=jax.ShapeDtypeStruct((T, F), _MM),
        scratch_shapes=[pltpu.VMEM((tt + FFN_HALO, tc), _F32)] * 2,
        compiler_params=_cparams(("parallel", "parallel")))(up, up, up, up, w, w, b, b)


def ffn_bwd_du(up, dact, w, b, *, name):
    T, F2 = up.shape
    F = F2 // 2
    K = w.shape[0]
    tt = _tile(T, CONV_TILE)
    tc = _tile(F, 512)
    nb = F // tc
    per = tt // FFN_HALO
    HB = FFN_HALO

    def body(g_ref, v_ref, gh_ref, vh_ref, da_ref, wg_ref, wv_ref, bg_ref, bv_ref, du_ref, dwb_ref,
             gscr, vscr):
        i = pl.program_id(1)
        keep = (i > 0).astype(_F32)
        ug = _ffn_u(g_ref, gh_ref, wg_ref, bg_ref, gscr, keep, tt, K)
        uv = _ffn_u(v_ref, vh_ref, wv_ref, bv_ref, vscr, keep, tt, K)
        s = _sigmoid(ug)
        da = da_ref[...].astype(_F32)
        du_g = da * uv * s * (1.0 + ug * (1.0 - s))
        du_v = da * ug * s
        du_ref[0] = du_g.astype(du_ref.dtype)
        du_ref[1] = du_v.astype(du_ref.dtype)

        @pl.when(i == 0)
        def _():
            dwb_ref[...] = jnp.zeros_like(dwb_ref)

        for half, (du, scr) in enumerate(((du_g, gscr), (du_v, vscr))):
            for j in range(K):
                dwb_ref[half, K - 1 - j:K - j, :] += _colsum(du * scr[HB - j:HB - j + tt, :])
            dwb_ref[half, K:K + 1, :] += _colsum(du)

    main = lambda off: pl.BlockSpec((tt, tc), lambda j, i: (i, j + off))
    halo = lambda off: pl.BlockSpec((HB, tc), lambda j, i: (jnp.maximum(i * per - 1, 0), j + off))
    wsp = lambda off: pl.BlockSpec((K, tc), lambda j, i: (0, j + off))
    bsp = lambda off: pl.BlockSpec((1, tc), lambda j, i: (0, j + off))
    return pl.pallas_call(
        body, name=name, grid=(nb, T // tt),
        in_specs=[main(0), main(nb), halo(0), halo(nb), main(0), wsp(0), wsp(nb), bsp(0), bsp(nb)],
        out_specs=[pl.BlockSpec((2, tt, tc), lambda j, i: (0, i, j)),
                   pl.BlockSpec((2, SUBLANES, tc), lambda j, i: (0, 0, j))],
        out_shape=[jax.ShapeDtypeStruct((2, T, F), _MM), jax.ShapeDtypeStruct((2, SUBLANES, F), _F32)],
        scratch_shapes=[pltpu.VMEM((tt + HB, tc), _F32)] * 2,
        compiler_params=_cparams(("parallel", "arbitrary")))(up, up, up, up, dact, w, w, b, b)


def dwconv_transpose(du, w, *, name):
    _, T, F = du.shape
    K = w.shape[0]
    tt = _tile(T, CONV_TILE)
    tc = _tile(F, 1408)
    nb = F // tc
    per = tt // FFN_HALO
    HB = FFN_HALO
    nt = T // tt
    last_hb = T // HB - 1

    def body(d_ref, dn_ref, w_ref, o_ref, scr):
        ahead = (pl.program_id(1) < nt - 1).astype(_F32)
        scr[0:tt, :] = d_ref[...].astype(_F32)
        scr[tt:tt + HB, :] = dn_ref[...].astype(_F32) * ahead
        o_ref[...] = _anticausal_taps(w_ref, scr, tt, K).astype(o_ref.dtype)

    return pl.pallas_call(
        body, name=name, grid=(2, nt, nb),
        in_specs=[pl.BlockSpec((None, tt, tc), lambda s, i, j: (s, i, j)),
                  pl.BlockSpec((None, HB, tc), lambda s, i, j: (s, jnp.minimum((i + 1) * per, last_hb), j)),
                  pl.BlockSpec((K, tc), lambda s, i, j: (0, s * nb + j))],
        out_specs=pl.BlockSpec((tt, tc), lambda s, i, j: (i, s * nb + j)),
        out_shape=jax.ShapeDtypeStruct((T, 2 * F), _MM),
        scratch_shapes=[pltpu.VMEM((tt + HB, tc), _F32)],
        compiler_params=_cparams(("parallel", "parallel", "parallel")))(du, du, w)


def _adamw_math(w, g, m, v):
    m = ADAM_B1 * m + (1.0 - ADAM_B1) * g
    v = ADAM_B2 * v + (1.0 - ADAM_B2) * (g * g)
    m_hat = m / (1.0 - ADAM_B1 ** ADAM_STEP)
    v_hat = v / (1.0 - ADAM_B2 ** ADAM_STEP)
    delta = -ADAM_LR * (m_hat / (jnp.sqrt(v_hat) + ADAM_EPS) + ADAM_WD * w)
    return delta, m, v


def _as2d(a):
    return a.reshape(1, -1) if a.ndim == 1 else a.reshape(-1, a.shape[-1])


def adamw(w, g, m, v, *, name):
    shape = w.shape
    w2, g2, m2, v2 = _as2d(w), _as2d(g), _as2d(m), _as2d(v)
    R, C = w2.shape
    lanes = -(-C // LANES) * LANES
    tr = _tile(R, max(SUBLANES, (1 << 20) // (4 * lanes)), SUBLANES)

    def body(w_ref, g_ref, m_ref, v_ref, d_ref, mo_ref, vo_ref):
        d, mn, vn = _adamw_math(w_ref[...], g_ref[...], m_ref[...], v_ref[...])
        d_ref[...] = d
        mo_ref[...] = mn
        vo_ref[...] = vn

    blk = pl.BlockSpec((tr, C), lambda i: (i, 0))
    outs = pl.pallas_call(
        body, name=name, grid=(R // tr,), in_specs=[blk] * 4, out_specs=[blk] * 3,
        out_shape=[jax.ShapeDtypeStruct((R, C), _F32)] * 3,
        compiler_params=_cparams(("parallel",)))(w2, g2, m2, v2)
    return tuple(o.reshape(shape) for o in outs)


def ada_w_update(c_t, d_ada, w, m, v, *, name):
    L, D, N = w.shape
    B = c_t.shape[1]
    tr = _tile(D, 256, SUBLANES)
    tn = _tile(N, 1024)

    def body(c_ref, a_ref, w_ref, m_ref, v_ref, g_ref, d_ref, mo_ref, vo_ref):
        cv = c_ref[...]
        cv = cv * _sigmoid(cv)
        g = jnp.dot(cv, a_ref[...], preferred_element_type=_F32, precision=HIGHEST)
        g_ref[...] = g
        d, mn, vn = _adamw_math(w_ref[...], g, m_ref[...], v_ref[...])
        d_ref[...] = d
        mo_ref[...] = mn
        vo_ref[...] = vn

    blk = pl.BlockSpec((None, tr, tn), lambda l, i, j: (l, i, j))
    return pl.pallas_call(
        body, name=name, grid=(L, D // tr, N // tn),
        in_specs=[pl.BlockSpec((tr, B), lambda l, i, j: (i, 0)),
                  pl.BlockSpec((None, B, tn), lambda l, i, j: (l, 0, j)), blk, blk, blk],
        out_specs=[blk] * 4,
        out_shape=[jax.ShapeDtypeStruct((L, D, N), _F32)] * 4,
        compiler_params=_cparams(("parallel", "parallel", "parallel")))(c_t, d_ada, w, m, v)


def _coords():
    return lax.axis_index("x"), lax.axis_index("y"), lax.axis_index("c")


def allgather_small(x, *, with_sum, name):
    R, C = x.shape

    def body(x_ref, out_ref, *rest):
        if with_sum:
            sum_ref, send_sems, recv_sems, local_sem = rest
        else:
            send_sems, recv_sems, local_sem = rest
        px, py, pc = _coords()
        me, sibling = (px, py, pc), (px, py, 1 - pc)
        chips = [(1 - px, py), (px, 1 - py), (1 - px, 1 - py)]

        def rows(qx, qy, qc):
            return out_ref.at[4 * qx + 2 * qy + qc]

        def copy(k, block, to, src=None):
            return pltpu.make_async_remote_copy(
                src_ref=rows(*block) if src is None else src, dst_ref=rows(*block),
                send_sem=send_sems.at[k], recv_sem=recv_sems.at[k], device_id=to, device_id_type=MESH)

        mine = pltpu.make_async_copy(x_ref, rows(*me), local_sem)
        mine.start()
        first = [copy(0, me, sibling, src=x_ref)]
        first += [copy(1 + j, me, (*chip, pc), src=x_ref) for j, chip in enumerate(chips)]
        for cp in first:
            cp.start()
        passed = [copy(4 + j, (*chip, pc), sibling) for j, chip in enumerate(chips)]
        for j, chip in enumerate(chips):
            copy(1 + j, (*chip, pc), me).wait_recv()
            passed[j].start()
        copy(0, sibling, me).wait_recv()
        for j, chip in enumerate(chips):
            copy(4 + j, (*chip, 1 - pc), me).wait_recv()
        for cp in first + passed:
            cp.wait_send()
        mine.wait()
        if with_sum:
            acc = out_ref[0]
            for k in range(1, N_DEV):
                acc = acc + out_ref[k]
            sum_ref[...] = acc

    vm = pl.BlockSpec(memory_space=pltpu.VMEM)
    out_shape = [jax.ShapeDtypeStruct((N_DEV, R, C), x.dtype)]
    if with_sum:
        out_shape.append(jax.ShapeDtypeStruct((R, C), x.dtype))
    outs = pl.pallas_call(
        body, name=name, in_specs=[vm], out_specs=[vm] * len(out_shape), out_shape=out_shape,
        scratch_shapes=[pltpu.SemaphoreType.DMA((7,)), pltpu.SemaphoreType.DMA((7,)), pltpu.SemaphoreType.DMA],
        compiler_params=pltpu.CompilerParams(vmem_limit_bytes=VMEM_LIMIT_V7X))(x)
    return outs if with_sum else outs[0]


def _at(start, size, align):
    return pl.ds(pl.multiple_of(start, align) if align > 1 else start, size)


class _BigLayout:
    def __init__(self, D, INs, Ds, F2s, Fs):
        self.D, self.INs, self.Ds, self.F2s, self.Fs = D, INs, Ds, F2s, Fs
        self.Dh, self.Dsh, self.Fsh = D // 2, Ds // 2, Fs // 2
        self.piece_shapes = [(self.Dh, INs), (self.Dsh, D), (self.Dh, F2s), (self.Fsh, D)]

    def in_full(self, a, ref, k, h):
        if a == 0:
            return ref.at[k, _at(h * self.Dh, self.Dh, self.Dh), :]
        if a == 1:
            return ref.at[_at(k * self.Ds + h * self.Dsh, self.Dsh, self.Dsh), :]
        if a == 2:
            return ref.at[_at(h * self.Dh, self.Dh, self.Dh), _at(k * self.F2s, self.F2s, self.F2s)]
        return ref.at[_at(k * self.Fs + h * self.Fsh, self.Fsh, self.Fsh), :]

    def in_shard(self, a, ref, h):
        rows = self.piece_shapes[a][0]
        return ref.at[_at(h * rows, rows, rows), :]


def gather_weights(shards, *, name):
    lay = _gather_layout(shards)
    n_arr = len(shards)

    def body(*refs):
        ops = _GatherOps(lay, refs[:n_arr], refs[n_arr:2 * n_arr], *refs[2 * n_arr:])
        ops.start()
        ops.forward()
        ops.finish()

    hbm = pl.BlockSpec(memory_space=pl.ANY)
    return pl.pallas_call(
        body, name=name, in_specs=[hbm] * n_arr, out_specs=[hbm] * n_arr,
        out_shape=_gather_out_shapes(lay, shards[0].dtype), scratch_shapes=_comm_sems(n_arr),
    )(*shards)


def _gather_layout(shards):
    D, INs = shards[0].shape
    return _BigLayout(D, INs, shards[1].shape[0], shards[2].shape[1], shards[3].shape[0])


def _gather_out_shapes(lay, dt):
    return [jax.ShapeDtypeStruct((N_CHIP, lay.D, lay.INs), dt), jax.ShapeDtypeStruct((lay.D, lay.D), dt),
            jax.ShapeDtypeStruct((lay.D, lay.F2s * N_CHIP), dt), jax.ShapeDtypeStruct((lay.Fs * N_CHIP, lay.D), dt)]


def _comm_sems(n_arr):
    return [pltpu.SemaphoreType.DMA((7 * n_arr,)), pltpu.SemaphoreType.DMA((7 * n_arr,)),
            pltpu.SemaphoreType.DMA((n_arr,))]


class _GatherOps:
    def __init__(self, lay, ins, outs, send_sems, recv_sems, local_sems):
        self.lay, self.ins, self.outs = lay, ins, outs
        self.send_sems, self.recv_sems, self.local_sems = send_sems, recv_sems, local_sems
        px, py, pc = _coords()
        self.pc, self.my_chip, self.sibling = pc, 2 * px + py, (px, py, 1 - pc)
        self.chips = [(1 - px, py), (px, 1 - py), (1 - px, 1 - py)]

    def _copy(self, a, kk, k, h, to, src=None):
        dst = self.lay.in_full(a, self.outs[a], k, h)
        return pltpu.make_async_remote_copy(
            src_ref=dst if src is None else src, dst_ref=dst,
            send_sem=self.send_sems.at[7 * a + kk], recv_sem=self.recv_sems.at[7 * a + kk],
            device_id=to, device_id_type=MESH)

    def _own(self, a):
        own = self.lay.in_shard(a, self.ins[a], self.pc)
        local = pltpu.make_async_copy(own, self.lay.in_full(a, self.outs[a], self.my_chip, self.pc),
                                      self.local_sems.at[a])
        sends = [self._copy(a, 0, self.my_chip, self.pc, self.sibling, src=own)]
        sends += [self._copy(a, 1 + j, self.my_chip, self.pc, (*chip, self.pc), src=own)
                  for j, chip in enumerate(self.chips)]
        return local, sends

    def _passed_on(self, a, j):
        qx, qy = self.chips[j]
        return self._copy(a, 4 + j, 2 * qx + qy, self.pc, self.sibling)

    def start(self):
        for a in range(len(self.ins)):
            local, sends = self._own(a)
            for cp in [local] + sends:
                cp.start()

    def forward(self):
        for j, (qx, qy) in enumerate(self.chips):
            for a in range(len(self.ins)):
                self._copy(a, 1 + j, 2 * qx + qy, self.pc, self.sibling).wait_recv()
                self._passed_on(a, j).start()

    def finish(self):
        for a in range(len(self.ins)):
            self._copy(a, 0, self.my_chip, 1 - self.pc, self.sibling).wait_recv()
            for j, (qx, qy) in enumerate(self.chips):
                self._copy(a, 4 + j, 2 * qx + qy, 1 - self.pc, self.sibling).wait_recv()
        for a in range(len(self.ins)):
            local, sends = self._own(a)
            for cp in sends + [self._passed_on(a, j) for j in range(len(self.chips))]:
                cp.wait_send()
            local.wait()


def scatter_grads(partials, *, name):
    lay = _scatter_layout(partials)
    n_arr = len(partials)

    def body(*refs):
        ops = _ScatterOps(lay, refs[:n_arr], refs[n_arr:2 * n_arr], *refs[2 * n_arr:])
        ops.start()
        ops.finish()

    hbm = pl.BlockSpec(memory_space=pl.ANY)
    return pl.pallas_call(
        body, name=name, in_specs=[hbm] * n_arr, out_specs=[hbm] * n_arr,
        out_shape=_scatter_out_shapes(lay, partials[0].dtype), scratch_shapes=_comm_sems(n_arr),
    )(*partials)


def _scatter_layout(partials):
    _, D, INs = partials[0].shape
    return _BigLayout(D, INs, partials[1].shape[0] // N_CHIP, partials[2].shape[1] // N_CHIP,
                      partials[3].shape[0] // N_CHIP)


def _scatter_out_shapes(lay, dt):
    return [jax.ShapeDtypeStruct((N_DEV, *s), dt) for s in lay.piece_shapes]


class _ScatterOps:
    def __init__(self, lay, ins, outs, send_sems, recv_sems, local_sems):
        self.lay, self.ins, self.outs = lay, ins, outs
        self.send_sems, self.recv_sems, self.local_sems = send_sems, recv_sems, local_sems

    def _copies(self):
        px, py, pc = _coords()
        me = 4 * px + 2 * py + pc
        copies, mine = [], []
        for a in range(len(self.ins)):
            mine.append(pltpu.make_async_copy(self.lay.in_full(a, self.ins[a], 2 * px + py, pc),
                                              self.outs[a].at[me], self.local_sems.at[a]))
            for mask in range(1, N_DEV):
                qx = 1 - px if (mask >> 2) & 1 else px
                qy = 1 - py if (mask >> 1) & 1 else py
                qc = 1 - pc if mask & 1 else pc
                copies.append(pltpu.make_async_remote_copy(
                    src_ref=self.lay.in_full(a, self.ins[a], 2 * qx + qy, qc), dst_ref=self.outs[a].at[me],
                    send_sem=self.send_sems.at[7 * a + mask - 1], recv_sem=self.recv_sems.at[7 * a + mask - 1],
                    device_id=(qx, qy, qc), device_id_type=MESH))
        return mine, copies

    def start(self):
        mine, copies = self._copies()
        for cp in mine + copies:
            cp.start()

    def finish(self):
        mine, copies = self._copies()
        for cp in copies:
            cp.wait_recv()
        for cp in copies:
            cp.wait_send()
        for cp in mine:
            cp.wait()


SIBLING_CHUNKS = 4


def sibling_exchange(bufs, *, name):
    n_arr = len(bufs)
    n_ch = [max(n for n in (SIBLING_CHUNKS, 2, 1) if x.shape[1] % (n * SUBLANES) == 0 or n == 1) for x in bufs]
    offs = [sum(n_ch[:a]) for a in range(n_arr)]

    def body(*refs):
        outs = refs[n_arr:2 * n_arr]
        send_sems, recv_sems = refs[2 * n_arr:]
        px, py, pc = _coords()
        copies = []
        for a in range(n_arr):
            rows = bufs[a].shape[1] // n_ch[a]
            for q in range(n_ch[a]):
                mine = outs[a].at[pc, pl.ds(q * rows, rows), :]
                copies.append(pltpu.make_async_remote_copy(
                    src_ref=mine, dst_ref=mine,
                    send_sem=send_sems.at[offs[a] + q], recv_sem=recv_sems.at[offs[a] + q],
                    device_id=(px, py, 1 - pc), device_id_type=MESH))
        for cp in copies:
            cp.start()
        for cp in copies:
            cp.wait_recv()
        for cp in copies:
            cp.wait_send()

    hbm = pl.BlockSpec(memory_space=pl.ANY)
    return pl.pallas_call(
        body, name=name, in_specs=[hbm] * n_arr, out_specs=[hbm] * n_arr,
        out_shape=[jax.ShapeDtypeStruct(x.shape, x.dtype) for x in bufs],
        input_output_aliases={a: a for a in range(n_arr)},
        scratch_shapes=[pltpu.SemaphoreType.DMA((sum(n_ch),))] * 2,
    )(*bufs)


def sum_slots(x, core, *, name):
    n, R, C = x.shape
    lanes = -(-C // LANES) * LANES
    tr = _tile(R, max(BF16_ROWS, (4 << 20) // (n * 2 * lanes)), BF16_ROWS)

    def body(core_ref, x_ref, o_ref):
        acc = x_ref[0].astype(_F32)
        for k in range(1, n):
            acc = acc + x_ref[k].astype(_F32)
        o_ref[...] = acc

    return pl.pallas_call(
        body, name=name,
        grid_spec=pltpu.PrefetchScalarGridSpec(
            num_scalar_prefetch=1, grid=(R // tr,),
            in_specs=[pl.BlockSpec((n, tr, C), lambda i, core_ref: (0, i, 0))],
            out_specs=pl.BlockSpec((None, tr, C), lambda i, core_ref: (core_ref[0], i, 0))),
        out_shape=jax.ShapeDtypeStruct((2, R, C), _F32),
        compiler_params=_cparams(("parallel",)))(core, x)


def _pack_flat(arrays, quantum):
    flat = jnp.concatenate([a.reshape(-1) for a in arrays])
    pad = (-flat.shape[0]) % quantum
    return jnp.pad(flat, (0, pad)) if pad else flat


def _unpack_flat(flat, shapes):
    out, off = [], 0
    for s in shapes:
        n = math.prod(s)
        out.append(flat[off:off + n].reshape(s))
        off += n
    return out


def _small_pack(arrays):
    return _pack_flat([a.astype(_F32) for a in arrays], SUBLANES * LANES).reshape(-1, LANES)


def kernel(x, c, ada_w, ada_b, mix_norm_g, w_in, b_forget, conf_dw_w, conf_dw_b, conf_ln_g, conf_ln_b, sc_dw_w, w_out, ffn_norm_g, w_up, ffn_dw_w, ffn_dw_b, w_down, final_norm_g, loss_target, m_ada_w, m_ada_b, m_mix_norm_g, m_w_in, m_b_forget, m_conf_dw_w, m_conf_dw_b, m_conf_ln_g, m_conf_ln_b, m_sc_dw_w, m_w_out, m_ffn_norm_g, m_w_up, m_ffn_dw_w, m_ffn_dw_b, m_w_down, m_final_norm_g, v_ada_w, v_ada_b, v_mix_norm_g, v_w_in, v_b_forget, v_conf_dw_w, v_conf_dw_b, v_conf_ln_g, v_conf_ln_b, v_sc_dw_w, v_w_out, v_ffn_norm_g, v_w_up, v_ffn_dw_w, v_ffn_dw_b, v_w_down, v_final_norm_g):
    _, T, D = x.shape
    L = ada_w.shape[0]
    A = D // 2
    H = A // HEAD_DIM
    C = D // 4
    assert D - A - C == C
    IN = 3 * A + H + 5 * C
    NM = 3 * A + 5 * C
    NP = NM + LANES
    F2 = w_up.shape[2] * N_CHIP
    F = F2 // 2
    NA = ada_w.shape[2]
    assert NA * N_CHIP == 6 * D and w_in.shape[2] * N_CHIP == IN

    px, py, pc = _coords()
    chip = 2 * px + py
    me = 2 * chip + pc

    x0 = x[0]
    tgt = loss_target[0]

    c_all = allgather_small(c.reshape(-1, LANES), with_sum=False, name="gather_c").reshape(N_DEV, D)
    parts = [matmul(c_all, ada_w[l], out_dtype=_F32, name="ada_fwd", tm=N_DEV, tn=512, tk=D,
                    a_silu=True, precision=HIGHEST) for l in range(L)]
    parts = jnp.stack(parts)
    got = allgather_small(parts.reshape(-1, LANES), with_sum=False, name="gather_ada")
    got = got.reshape(N_DEV, L, N_DEV, NA)[0::2]
    mine = lax.dynamic_index_in_dim(got, me, axis=2, keepdims=False)
    ada = jnp.transpose(mine, (1, 0, 2)).reshape(L, 6 * D) + ada_b

    INs = IN // N_CHIP
    wp_l, wout_l, wup_l, wdown_l = [None] * L, [None] * L, [None] * L, [None] * L
    shards_of = lambda l: [w_in[l].astype(_MM), w_out[l].astype(_MM), w_up[l].astype(_MM), w_down[l].astype(_MM)]

    def take_gathered(l, gathered):
        wi4, wout_l[l], wup_l[l], wdown_l[l] = gathered
        wi = jnp.transpose(wi4, (1, 0, 2)).reshape(D, IN)
        wp_l[l] = jnp.concatenate(
            [wi[:, :3 * A], wi[:, 3 * A + H:], jnp.pad(wi[:, 3 * A:3 * A + H], ((0, 0), (0, LANES - H)))], axis=1)

    take_gathered(0, gather_weights(shards_of(0), name="gather_weights"))

    small_w = _small_pack([conf_dw_w, sc_dw_w, ffn_dw_w])
    sw_all = allgather_small(small_w, with_sum=False, name="gather_small_w")[0::2].reshape(N_CHIP, -1)
    sw_shapes = [conf_dw_w.shape, sc_dw_w.shape, ffn_dw_w.shape]
    sw_parts = [_unpack_flat(sw_all[k], sw_shapes) for k in range(N_CHIP)]
    conf_w_full = jnp.concatenate([p[0] for p in sw_parts], axis=-1)
    sc_w_full = jnp.concatenate([p[1] for p in sw_parts], axis=-1)
    ffn_w_full = jnp.concatenate([p[2] for p in sw_parts], axis=-1)
    bf_pad = jnp.pad(b_forget, ((0, 0), (0, LANES - H)))

    row = lambda a: a.reshape(1, -1)

    saved = []
    x_cur, branch, gate = x0, None, None
    for l in range(L):
        sh_m, sc_m, g_m, sh_f, sc_f, g_f = [row(ada[l, k * D:(k + 1) * D]) for k in range(6)]
        a1 = row(mix_norm_g[l]) * (1.0 + sc_m)
        a2 = row(ffn_norm_g[l]) * (1.0 + sc_f)
        x_in, h1 = resid_norm_fwd(x_cur, a1, sh_m, branch, gate, name="norm_mix_fwd")
        wp = wp_l[l]
        proj = matmul(h1, wp, out_dtype=_MM, name="proj_fwd", tn=512, tk=D, b_cols=(0, NM))
        flog = matmul(h1, wp, out_dtype=_F32, name="fgate_logits", tn=LANES, tk=D, b_cols=(NM, LANES))
        bf = row(bf_pad[l])
        fcum = fgate_fwd(flog, bf, name="fgate_fwd")
        f_t = fcum[:, :H].T
        fq = jnp.broadcast_to(f_t[:, :, None], (H, T, LANES))
        fk = f_t[:, None, :]
        if l + 1 < L:
            attn, lse, gathered = attn_fwd(proj, fq, fk, heads=H, name="attn_fwd_gather", gather=shards_of(l + 1))
            take_gathered(l + 1, gathered)
        else:
            attn, lse, _ = attn_fwd(proj, fq, fk, heads=H, name="attn_fwd")
        cw, cb = conf_w_full[l], row(conf_dw_b[l])
        lg, lb, sw = row(conf_ln_g[l]), row(conf_ln_b[l]), sc_w_full[l]
        cm, cc, zc = mixer_misc_fwd(proj, cw, cb, lg, lb, sw, width=C, base_col=3 * A, name="misc_fwd")
        cat = jnp.concatenate([attn, cm], axis=1)
        mixed = matmul(cat, wout_l[l], out_dtype=_F32, name="wout_fwd", tn=512, tk=D)
        x_mid, h2 = resid_norm_fwd(x_in, a2, sh_f, mixed, g_m, name="norm_ffn_fwd")
        up = matmul(h2, wup_l[l], out_dtype=_MM, name="wup_fwd", tn=512, tk=D)
        fw, fb = ffn_w_full[l], row(ffn_dw_b[l])
        act = ffn_act_fwd(up, fw, fb, name="ffn_act_fwd")
        dn = matmul(act, wdown_l[l], out_dtype=_F32, name="wdown_fwd", tk=2816)
        saved.append(dict(x_in=x_in, h1=h1, proj=proj, flog=flog, fq=fq, fk=fk, attn=attn, lse=lse, cc=cc, zc=zc,
                          cat=cat, mixed=mixed, x_mid=x_mid, h2=h2, up=up, act=act, dn=dn, a1=a1, a2=a2,
                          g_m=g_m, g_f=g_f, sc_m=sc_m, sc_f=sc_f, bf=bf, cw=cw, lg=lg, lb=lb, sw=sw, fw=fw, fb=fb))
        x_cur, branch, gate = x_mid, dn, g_f

    dx, loss_row, d_final_g = final_loss_bwd(x_cur, branch, gate, row(final_norm_g), tgt, name="loss_bwd")

    KF = ffn_dw_w.shape[1]
    g_big, recv_l = [None] * L, [None] * L
    d_ada, d_g1, d_g2, d_bf, d_cw, d_cb, d_lg, d_lb, d_sw, d_fw, d_fb = ([None] * L for _ in range(11))
    for l in reversed(range(L)):
        s = saved[l]
        ddn, dg_f = gate_bwd(dx, s["dn"], s["g_f"], name="gate_ffn_bwd")
        dact = matmul(ddn, wdown_l[l], out_dtype=_MM, name="wdown_dgrad", trans_b=True, tn=512, tk=D)
        gw_down = matmul(s["act"], ddn, out_dtype=_MM, name="wdown_wgrad", trans_a=True, tm=1408)
        du, dwb = ffn_bwd_du(s["up"], dact, s["fw"], s["fb"], name="ffn_bwd_du")
        dup = dwconv_transpose(du, s["fw"], name="ffn_bwd_dup")
        dh2 = matmul(dup, wup_l[l], out_dtype=_MM, name="wup_dgrad", trans_b=True, tk=2816)
        gw_up = matmul(s["h2"], dup, out_dtype=_MM, name="wup_wgrad", trans_a=True)
        dx_mid, dsh_f, da2 = norm_bwd(s["x_mid"], dh2, dx, s["a2"], name="norm_ffn_bwd")
        dmixed, dg_m = gate_bwd(dx_mid, s["mixed"], s["g_m"], name="gate_mix_bwd")
        dcat = matmul(dmixed, wout_l[l], out_dtype=_MM, name="wout_dgrad", trans_b=True, tn=512, tk=D)
        gw_out = matmul(s["cat"], dmixed, out_dtype=_MM, name="wout_wgrad", trans_a=True)
        if l + 1 < L:
            dq, dk, dv, dfq, dfk, recv_l[l + 1] = attn_bwd(
                s["proj"], dcat, s["attn"], s["fq"], s["fk"], s["lse"], heads=H, name="attn_bwd_scatter",
                scatter=g_big[l + 1])
        else:
            dq, dk, dv, dfq, dfk, _ = attn_bwd(s["proj"], dcat, s["attn"], s["fq"], s["fk"], s["lse"], heads=H,
                                               name="attn_bwd")
        dmisc, d_cw[l], d_cb[l], d_lg[l], d_lb[l], d_sw[l] = mixer_misc_bwd(
            s["proj"], dcat, s["cc"], s["zc"], s["cw"], s["lg"], s["lb"], s["sw"],
            width=C, base_col=3 * A, dbase_col=A, name="misc_bwd")
        dfk_pad = jnp.pad((dfk[:, 0, :] + dfq[:, :, 0]).T, ((0, 0), (0, LANES - H)))
        dflog, dbf = fgate_bwd(dfk_pad, s["flog"], s["bf"], name="fgate_bwd")
        dproj = jnp.concatenate([dq, dk, dv, dmisc, dflog], axis=1)
        dh1 = matmul(dproj, wp_l[l], out_dtype=_MM, name="proj_dgrad", trans_b=True, tk=1920)
        gwp = matmul(s["h1"], dproj, out_dtype=_MM, name="proj_wgrad", trans_a=True, tn=1152)
        dx, dsh_m, da1 = norm_bwd(s["x_in"], dh1, dx_mid, s["a1"], name="norm_mix_bwd")

        g1, g2 = row(mix_norm_g[l]), row(ffn_norm_g[l])
        d_ada[l] = jnp.concatenate([dsh_m, da1 * g1, dg_m, dsh_f, da2 * g2, dg_f], axis=1)[0]
        d_g1[l] = (da1 * (1.0 + s["sc_m"]))[0]
        d_g2[l] = (da2 * (1.0 + s["sc_f"]))[0]
        d_bf[l] = dbf[0, :H]
        d_fw[l] = jnp.concatenate([dwb[0, :KF], dwb[1, :KF]], axis=1)
        d_fb[l] = jnp.concatenate([dwb[0, KF], dwb[1, KF]])
        gw_in = jnp.concatenate([gwp[:, :3 * A], gwp[:, NM:NM + H], gwp[:, 3 * A:NM]], axis=1)
        gw_in4 = jnp.transpose(gw_in.reshape(D, N_CHIP, INs), (1, 0, 2))
        g_big[l] = [gw_in4, gw_out, gw_up, gw_down]

    small = [loss_row[0], jnp.stack(d_g1), jnp.stack(d_bf), jnp.stack(d_cw), jnp.stack(d_cb)[:, 0],
             jnp.stack(d_lg)[:, 0], jnp.stack(d_lb)[:, 0], jnp.stack(d_sw), jnp.stack(d_g2), jnp.stack(d_fw),
             jnp.stack(d_fb), d_final_g[0], jnp.stack(d_ada)]
    small_shapes = [a.shape for a in small]
    sm_all, sm_sum = allgather_small(_small_pack(small), with_sum=True, name="reduce_small")
    (loss_v, g_mix_norm, g_bf, g_cw_full, g_cb, g_lg, g_lb, g_sw_full, g_ffn_norm, g_fw_full, g_fb, g_final,
     g_ada_b) = _unpack_flat(sm_sum.reshape(-1), small_shapes)
    loss = loss_v[0]
    n_ada = L * 6 * D
    off_ada = sum(math.prod(sh) for sh in small_shapes[:-1])
    d_ada_all = sm_all.reshape(N_DEV, -1)[:, off_ada:off_ada + n_ada].reshape(N_DEV, L, 6 * D)
    d_ada_chip = lax.dynamic_slice_in_dim(d_ada_all, chip * NA, NA, axis=2)
    d_ada_chip = jnp.transpose(d_ada_chip, (1, 0, 2))
    cshard = lambda a: lax.dynamic_slice_in_dim(a, chip * (a.shape[-1] // N_CHIP), a.shape[-1] // N_CHIP,
                                                axis=a.ndim - 1)
    g_conf_dw_w, g_sc_dw_w, g_ffn_dw_w = cshard(g_cw_full), cshard(g_sw_full), cshard(g_fw_full)

    g_in_l, g_out_l, g_up_l, g_down_l = [], [], [], []
    core = jnp.reshape(pc, (1,)).astype(jnp.int32)
    recv_l[0] = scatter_grads(g_big[0], name="scatter_grads")
    for l in range(L):
        red = [sum_slots(r, core, name="sum_grads") for r in recv_l[l]]
        gi, go, gu, gd = sibling_exchange(red, name="sibling_grads")
        g_in_l.append(gi.reshape(w_in.shape[1:])); g_out_l.append(go.reshape(w_out.shape[1:]))
        g_up_l.append(gu.reshape(w_up.shape[1:])); g_down_l.append(gd.reshape(w_down.shape[1:]))
    g_w_in, g_w_out, g_w_up, g_w_down = (jnp.stack(t) for t in (g_in_l, g_out_l, g_up_l, g_down_l))

    g_ada_w, dl_ada_w, nm_ada_w, nv_ada_w = ada_w_update(c_all.T, d_ada_chip, ada_w, m_ada_w, v_ada_w,
                                                          name="ada_w_update")

    grads = dict(ada_b=g_ada_b, mix_norm_g=g_mix_norm, w_in=g_w_in, b_forget=g_bf, conf_dw_w=g_conf_dw_w,
                 conf_dw_b=g_cb, conf_ln_g=g_lg, conf_ln_b=g_lb, sc_dw_w=g_sc_dw_w, w_out=g_w_out,
                 ffn_norm_g=g_ffn_norm, w_up=g_w_up, ffn_dw_w=g_ffn_dw_w, ffn_dw_b=g_fb, w_down=g_w_down,
                 final_norm_g=g_final)
    weights = dict(ada_b=(ada_b, m_ada_b, v_ada_b), mix_norm_g=(mix_norm_g, m_mix_norm_g, v_mix_norm_g),
                   w_in=(w_in, m_w_in, v_w_in), b_forget=(b_forget, m_b_forget, v_b_forget),
                   conf_dw_w=(conf_dw_w, m_conf_dw_w, v_conf_dw_w), conf_dw_b=(conf_dw_b, m_conf_dw_b, v_conf_dw_b),
                   conf_ln_g=(conf_ln_g, m_conf_ln_g, v_conf_ln_g), conf_ln_b=(conf_ln_b, m_conf_ln_b, v_conf_ln_b),
                   sc_dw_w=(sc_dw_w, m_sc_dw_w, v_sc_dw_w), w_out=(w_out, m_w_out, v_w_out),
                   ffn_norm_g=(ffn_norm_g, m_ffn_norm_g, v_ffn_norm_g), w_up=(w_up, m_w_up, v_w_up),
                   ffn_dw_w=(ffn_dw_w, m_ffn_dw_w, v_ffn_dw_w), ffn_dw_b=(ffn_dw_b, m_ffn_dw_b, v_ffn_dw_b),
                   w_down=(w_down, m_w_down, v_w_down), final_norm_g=(final_norm_g, m_final_norm_g, v_final_norm_g))
    order = ["ada_w", "ada_b", "mix_norm_g", "w_in", "b_forget", "conf_dw_w", "conf_dw_b", "conf_ln_g", "conf_ln_b",
             "sc_dw_w", "w_out", "ffn_norm_g", "w_up", "ffn_dw_w", "ffn_dw_b", "w_down", "final_norm_g"]
    g_out, d_out, m_out, v_out = {}, {}, {}, {}
    g_out["ada_w"], d_out["ada_w"], m_out["ada_w"], v_out["ada_w"] = g_ada_w, dl_ada_w, nm_ada_w, nv_ada_w
    for n in order[1:]:
        w, m, v = weights[n]
        g = grads[n].reshape(w.shape)
        g_out[n] = g
        d_out[n], m_out[n], v_out[n] = adamw(w, g, m, v, name="adamw_" + n)

    return (loss, dx[None], *[g_out[n] for n in order], *[d_out[n] for n in order],
            *[m_out[n] for n in order], *[v_out[n] for n in order])
```
